```python
import math
import jax
import jax.numpy as jnp
from jax import lax
import numpy as np

D_MODEL = 1024
BATCH = 32
SEQ = 2048
DEPTH = 4

CHUNK = 64
Q_BLOCK = 128
HEAD_DIM = 64
ROPE_DIM = HEAD_DIM // 4
ROPE_THETA = 500000.0
RMS_EPS = 1e-6
A_HEADS = D_MODEL // (2 * HEAD_DIM)
A_WIDTH = A_HEADS * HEAD_DIM
B_HEADS = D_MODEL // (4 * HEAD_DIM)
B_QK_WIDTH = B_HEADS * 2 * HEAD_DIM
B_VDIM = 2 * HEAD_DIM
B_WIDTH = B_HEADS * B_VDIM
C_WIDTH = D_MODEL // 2
CONV_W = 3
D_HEADS = D_MODEL // (2 * HEAD_DIM)
D_WIDTH = D_HEADS * HEAD_DIM
D_LEFT_CHUNKS = 8
D_BAND = (D_LEFT_CHUNKS + 1) * CHUNK
REL_CLIP = 128
X_HEADS = 4
X_HEAD_DIM = D_MODEL // X_HEADS
N_MEM = 256
D_FF = 2816
N_EVEN = (DEPTH + 1) // 2
N_ODD = DEPTH // 2
EVEN_SIZES = (A_WIDTH, A_WIDTH, A_WIDTH, A_HEADS, B_QK_WIDTH, B_QK_WIDTH, B_WIDTH)
ODD_SIZES = (C_WIDTH, C_WIDTH, C_WIDTH, D_WIDTH, D_WIDTH, D_WIDTH)
EVEN_IN = 3 * A_WIDTH + A_HEADS + 2 * B_QK_WIDTH + B_WIDTH
ODD_IN = 3 * C_WIDTH + 3 * D_WIDTH
EVEN_MIX = A_WIDTH + B_WIDTH
ODD_MIX = C_WIDTH + D_WIDTH
MAX_POS_OFFSET = 65536

kernel_name = 'hybrid_fox_diff_conv_chunkattn_macaron'


def rms_norm(x, g):
    xf = x.astype(jnp.float32)
    y = xf * lax.rsqrt(jnp.mean(xf * xf, axis=-1, keepdims=True) + RMS_EPS)
    return (y * g.astype(jnp.float32)).astype(x.dtype)


def split_cols(y, sizes):
    out, start = [], 0
    for s in sizes:
        out.append(y[..., start:start + s])
        start += s
    return out


def rope_tables(positions):
    inv = ROPE_THETA ** (-jnp.arange(0, ROPE_DIM, 2, dtype=jnp.float32) / ROPE_DIM)
    ang = positions.astype(jnp.float32)[..., None] * inv
    return jnp.cos(ang), jnp.sin(ang)


def apply_partial_rope(x, cos, sin):
    half = ROPE_DIM // 2
    bshape = cos.shape[:2] + (1,) * (x.ndim - 3) + (half,)
    c, s = cos.reshape(bshape), sin.reshape(bshape)
    xr = x[..., :ROPE_DIM].astype(jnp.float32)
    x1, x2 = xr[..., :half], xr[..., half:]
    rot = jnp.concatenate([x1 * c - x2 * s, x2 * c + x1 * s], axis=-1).astype(x.dtype)
    return jnp.concatenate([rot, x[..., ROPE_DIM:]], axis=-1)


def swiglu(h, w_in, w_out):
    g, u = split_cols(h @ w_in, (D_FF, D_FF))
    return (jax.nn.silu(g) * u) @ w_out


def forgetting_attention(q, k, v, cum_logf):
    seq = q.shape[1]
    scale = q.shape[-1] ** -0.5
    outs = []
    for qs in range(0, seq, Q_BLOCK):
        qe = qs + Q_BLOCK
        s = jnp.einsum('bqhd,bkhd->bhqk', q[:, qs:qe], k[:, :qe],
                       preferred_element_type=jnp.float32) * scale
        s = s + cum_logf[:, :, qs:qe, None] - cum_logf[:, :, None, :qe]
        causal = jnp.arange(qs, qe)[:, None] >= jnp.arange(qe)[None, :]
        p = jax.nn.softmax(jnp.where(causal, s, -jnp.inf), axis=-1).astype(v.dtype)
        outs.append(jnp.einsum('bhqk,bkhd->bqhd', p, v[:, :qe]))
    return jnp.concatenate(outs, axis=1)


def differential_attention(q, k, v, lam):
    seq = q.shape[1]
    scale = q.shape[-1] ** -0.5
    outs = []
    for qs in range(0, seq, Q_BLOCK):
        qe = qs + Q_BLOCK
        s = jnp.einsum('bqhjd,bkhjd->bhjqk', q[:, qs:qe], k[:, :qe],
                       preferred_element_type=jnp.float32) * scale
        allowed = (jnp.arange(qs, qe)[:, None] // CHUNK) >= (jnp.arange(qe)[None, :] // CHUNK)
        p = jax.nn.softmax(jnp.where(allowed, s, -jnp.inf), axis=-1)
        p = (p[:, :, 0] - lam * p[:, :, 1]).astype(v.dtype)
        outs.append(jnp.einsum('bhqk,bkhe->bqhe', p, v[:, :qe]))
    return jnp.concatenate(outs, axis=1)


def short_conv_mixer(gate_b, gate_c, h, conv_w):
    u = gate_c * h
    y = lax.conv_general_dilated(u, conv_w[:, None, :], window_strides=(1,),
                                 padding=[(CONV_W - 1, 0)],
                                 dimension_numbers=('NWC', 'WIO', 'NWC'),
                                 feature_group_count=u.shape[-1])
    return gate_b * y


def chunk_band_attention(q, k, v, rel_table):
    bsz, seq, heads, hd = q.shape
    n_chunks = seq // CHUNK
    left = D_LEFT_CHUNKS * CHUNK
    pad = ((0, 0), (left, 0), (0, 0), (0, 0))
    kp, vp = jnp.pad(k, pad), jnp.pad(v, pad)
    rel = jnp.arange(CHUNK)[:, None] - jnp.arange(D_BAND)[None, :] + left
    rel_idx = jnp.clip(rel, -REL_CLIP, REL_CLIP) + REL_CLIP
    bias = rel_table[:, rel_idx].astype(jnp.float32)
    scale = hd ** -0.5

    def one_chunk(c):
        start = c * CHUNK
        qc = lax.dynamic_slice_in_dim(q, start, CHUNK, axis=1)
        kc = lax.dynamic_slice_in_dim(kp, start, D_BAND, axis=1)
        vc = lax.dynamic_slice_in_dim(vp, start, D_BAND, axis=1)
        s = jnp.einsum('bqhd,bkhd->bhqk', qc, kc, preferred_element_type=jnp.float32) * scale + bias
        valid = (start - left + jnp.arange(D_BAND)) >= 0
        p = jax.nn.softmax(jnp.where(valid, s, -jnp.inf), axis=-1).astype(v.dtype)
        return jnp.einsum('bhqk,bkhd->bqhd', p, vc)

    out = lax.map(one_chunk, jnp.arange(n_chunks))
    return out.transpose(1, 0, 2, 3, 4).reshape(bsz, seq, heads * hd)


def even_mixer(h, w_in, f_bias, qk_gains, lam_params, subln_gain, lambda_init, w_out, cos, sin):
    bsz, seq, _ = h.shape
    a_q, a_k, a_v, a_f, b_q, b_k, b_v = split_cols(h @ w_in, EVEN_SIZES)
    aq = rms_norm(a_q.reshape(bsz, seq, A_HEADS, HEAD_DIM), qk_gains[0])
    ak = rms_norm(a_k.reshape(bsz, seq, A_HEADS, HEAD_DIM), qk_gains[1])
    av = a_v.reshape(bsz, seq, A_HEADS, HEAD_DIM)
    logf = jax.nn.log_sigmoid((a_f + f_bias).astype(jnp.float32))
    cum_logf = jnp.cumsum(logf, axis=1).transpose(0, 2, 1)
    a_out = forgetting_attention(aq, ak, av, cum_logf).reshape(bsz, seq, A_WIDTH)
    bq = rms_norm(b_q.reshape(bsz, seq, B_HEADS, 2, HEAD_DIM), qk_gains[2])
    bk = rms_norm(b_k.reshape(bsz, seq, B_HEADS, 2, HEAD_DIM), qk_gains[3])
    bq, bk = apply_partial_rope(bq, cos, sin), apply_partial_rope(bk, cos, sin)
    bv = b_v.reshape(bsz, seq, B_HEADS, B_VDIM)
    lp = lam_params.astype(jnp.float32)
    lam = jnp.exp(jnp.sum(lp[0] * lp[1])) - jnp.exp(jnp.sum(lp[2] * lp[3])) + lambda_init
    b_out = differential_attention(bq, bk, bv, lam)
    b_out = (rms_norm(b_out, subln_gain) * (1.0 - lambda_init)).reshape(bsz, seq, B_WIDTH)
    return jnp.concatenate([a_out, b_out], axis=-1) @ w_out


def odd_mixer(h, w_in, conv_w, qk_gains, rel_table, w_out):
    bsz, seq, _ = h.shape
    c_b, c_c, c_h, d_q, d_k, d_v = split_cols(h @ w_in, ODD_SIZES)
    c_out = short_conv_mixer(c_b, c_c, c_h, conv_w)
    dq = rms_norm(d_q.reshape(bsz, seq, D_HEADS, HEAD_DIM), qk_gains[0])
    dk = rms_norm(d_k.reshape(bsz, seq, D_HEADS, HEAD_DIM), qk_gains[1])
    dv = d_v.reshape(bsz, seq, D_HEADS, HEAD_DIM)
    d_out = chunk_band_attention(dq, dk, dv, rel_table)
    return jnp.concatenate([c_out, d_out], axis=-1) @ w_out


def memory_cross_attention(h, mem_n, w_q, w_kv, qk_gains, w_o):
    bsz, seq, _ = h.shape
    q = rms_norm((h @ w_q).reshape(bsz, seq, X_HEADS, X_HEAD_DIM), qk_gains[0])
    k, v = split_cols(mem_n @ w_kv, (D_MODEL, D_MODEL))
    k = rms_norm(k.reshape(bsz, -1, X_HEADS, X_HEAD_DIM), qk_gains[1])
    v = v.reshape(bsz, -1, X_HEADS, X_HEAD_DIM)
    s = jnp.einsum('bqhd,bkhd->bhqk', q, k, preferred_element_type=jnp.float32) * X_HEAD_DIM ** -0.5
    p = jax.nn.softmax(s, axis=-1).astype(v.dtype)
    o = jnp.einsum('bhqk,bkhd->bqhd', p, v).reshape(bsz, seq, D_MODEL)
    return o @ w_o


def setup_inputs(seed: int = 0) -> dict:
    key = jax.random.key(seed)
    ks = jax.random.split(key, 24)
    f32 = jnp.float32

    def dense(k, shape):
        return jax.random.normal(k, shape, f32) * shape[-2] ** -0.5

    def gain(k, shape):
        return 1.0 + 0.05 * jax.random.normal(k, shape, f32)

    x = jax.random.normal(ks[0], (BATCH, SEQ, D_MODEL), f32)
    mem = jax.random.normal(ks[1], (BATCH, N_MEM, D_MODEL), f32)
    offset = jax.random.randint(ks[2], (BATCH, 1), 0, MAX_POS_OFFSET, dtype=jnp.int32)
    positions = offset + jnp.arange(SEQ, dtype=jnp.int32)[None, :]
    return {
        'x': x,
        'mem': mem,
        'positions': positions,
        'ln_gains': gain(ks[3], (DEPTH, 5, D_MODEL)),
        'ffn1_w_in': dense(ks[4], (DEPTH, D_MODEL, 2 * D_FF)),
        'ffn1_w_out': dense(ks[5], (DEPTH, D_FF, D_MODEL)),
        'ffn2_w_in': dense(ks[6], (DEPTH, D_MODEL, 2 * D_FF)),
        'ffn2_w_out': dense(ks[7], (DEPTH, D_FF, D_MODEL)),
        'even_w_in': dense(ks[8], (N_EVEN, D_MODEL, EVEN_IN)),
        'even_f_bias': jax.random.uniform(ks[9], (N_EVEN, A_HEADS), f32, 1.0, 5.0),
        'even_qk_gains': gain(ks[10], (N_EVEN, 4, HEAD_DIM)),
        'even_lambda': 0.1 * jax.random.normal(ks[11], (N_EVEN, 4, HEAD_DIM), f32),
        'even_subln_gain': gain(ks[12], (N_EVEN, B_VDIM)),
        'even_w_out': dense(ks[13], (N_EVEN, EVEN_MIX, D_MODEL)),
        'odd_w_in': dense(ks[14], (N_ODD, D_MODEL, ODD_IN)),
        'odd_conv_w': jax.random.normal(ks[15], (N_ODD, CONV_W, C_WIDTH), f32) * CONV_W ** -0.5,
        'odd_qk_gains': gain(ks[16], (N_ODD, 2, HEAD_DIM)),
        'odd_rel_bias': 0.5 * jax.random.normal(ks[17], (N_ODD, D_HEADS, 2 * REL_CLIP + 1), f32),
        'odd_w_out': dense(ks[18], (N_ODD, ODD_MIX, D_MODEL)),
        'x_w_q': dense(ks[19], (DEPTH, D_MODEL, D_MODEL)),
        'x_w_kv': dense(ks[20], (DEPTH, D_MODEL, 2 * D_MODEL)),
        'x_qk_gains': gain(ks[21], (DEPTH, 2, X_HEAD_DIM)),
        'x_w_o': dense(ks[22], (DEPTH, D_MODEL, D_MODEL)),
    }


def reference(x, mem, positions, ln_gains, ffn1_w_in, ffn1_w_out, ffn2_w_in, ffn2_w_out,
              even_w_in, even_f_bias, even_qk_gains, even_lambda, even_subln_gain, even_w_out,
              odd_w_in, odd_conv_w, odd_qk_gains, odd_rel_bias, odd_w_out,
              x_w_q, x_w_kv, x_qk_gains, x_w_o):
    cos, sin = rope_tables(positions)
    for layer in range(DEPTH):
        g = ln_gains[layer]
        x = x + 0.5 * swiglu(rms_norm(x, g[0]), ffn1_w_in[layer], ffn1_w_out[layer])
        h = rms_norm(x, g[1])
        if layer % 2 == 0:
            e = layer // 2
            lambda_init = 0.8 - 0.6 * math.exp(-0.3 * layer)
            mixed = even_mixer(h, even_w_in[e], even_f_bias[e], even_qk_gains[e], even_lambda[e],
                               even_subln_gain[e], lambda_init, even_w_out[e], cos, sin)
        else:
            o = layer // 2
            mixed = odd_mixer(h, odd_w_in[o], odd_conv_w[o], odd_qk_gains[o], odd_rel_bias[o],
                              odd_w_out[o])
        x = x + mixed
        x = x + memory_cross_attention(rms_norm(x, g[2]), rms_norm(mem, g[3]), x_w_q[layer],
                                       x_w_kv[layer], x_qk_gains[layer], x_w_o[layer])
        x = x + 0.5 * swiglu(rms_norm(x, g[4]), ffn2_w_in[layer], ffn2_w_out[layer])
    return x
```

```python
import functools
import math

import jax
import jax.numpy as jnp
from jax import lax
from jax.experimental import pallas as pl
from jax.experimental.pallas import tpu as pltpu

D_MODEL = 1024
CHUNK = 64
HEAD_DIM = 64
ROPE_DIM = HEAD_DIM // 4
ROPE_THETA = 500000.0
RMS_EPS = 1e-6
D_FF = 2816
A_HEADS = 8
MIX_W = 512
LANES = 128
D_LEFT = 8 * CHUNK
REL_CLIP = 128
X_HEADS = 4
X_HEAD_DIM = D_MODEL // X_HEADS
NEG = -1e30

VMEM_LIMIT = 56 * 1024 * 1024
FFN_CHUNKS = (768, 768, 768, 512)

BF16 = jnp.bfloat16
F32 = jnp.float32


def _params(n_axes):
    return pltpu.CompilerParams(dimension_semantics=("arbitrary",) * n_axes,
                                vmem_limit_bytes=VMEM_LIMIT)


def _const_spec(shape):
    nd = len(shape)
    return pl.BlockSpec(shape, lambda *_: (0,) * nd, pipeline_mode=pl.Buffered(1))


def _rms(x):
    return x * lax.rsqrt(jnp.mean(x * x, axis=-1, keepdims=True) + RMS_EPS)


def _dot(a, b):
    return jnp.dot(a, b, preferred_element_type=F32)


def _dot_nt(a, b):
    return lax.dot_general(a, b, (((1,), (1,)), ((), ())), preferred_element_type=F32)


def _ffn_kernel(x_ref, g_ref, wg_ref, wu_ref, wo_ref, o_ref):
    x = x_ref[...]
    xn = (_rms(x) * g_ref[...]).astype(BF16)
    y = jnp.zeros_like(x)
    start = 0
    for width in FFN_CHUNKS:
        gate = _dot(xn, wg_ref[:, start:start + width])
        up = _dot(xn, wu_ref[:, start:start + width])
        act = (gate * (1.0 / (1.0 + jnp.exp(-gate))) * up).astype(BF16)
        y = y + _dot(act, wo_ref[start:start + width, :])
        start += width
    o_ref[...] = x + 0.5 * y


def _ffn(x, gain, w_in, w_out, tm=512):
    t, d = x.shape
    wg = w_in[:, :D_FF].astype(BF16)
    wu = w_in[:, D_FF:].astype(BF16)
    wo = w_out.astype(BF16)
    return pl.pallas_call(
        _ffn_kernel,
        out_shape=jax.ShapeDtypeStruct((t, d), F32),
        grid=(t // tm,),
        in_specs=[pl.BlockSpec((tm, d), lambda i: (i, 0)),
                  _const_spec((1, d)),
                  _const_spec((d, D_FF)), _const_spec((d, D_FF)), _const_spec((D_FF, d))],
        out_specs=pl.BlockSpec((tm, d), lambda i: (i, 0)),
        compiler_params=_params(1),
        name="ffn",
    )(x, gain.reshape(1, d), wg, wu, wo)


def _head_norm(y, ones_bd, gain):
    msq = _dot((y * y).astype(BF16), ones_bd) * (1.0 / HEAD_DIM)
    return y * lax.rsqrt(msq + RMS_EPS) * gain


def _block_diag_ones():
    r = jnp.arange(MIX_W) // HEAD_DIM
    return (r[:, None] == r[None, :]).astype(BF16)


def _tile_heads(v):
    return jnp.tile(v, (1, MIX_W // HEAD_DIM))


def _even_prep_kernel(x_ref, g_ref, wa_ref, wf_ref, fb_ref, wb_ref, gains_ref, bd_ref,
                      ra_ref, rm_ref, rp_ref,
                      aq_ref, ak_ref, av_ref, bq_ref, bk_ref, bv_ref, fcol_ref, frow_ref,
                      carry_ref, *, tk):
    tm = x_ref.shape[0]

    @pl.when(pl.program_id(1) == 0)
    def _():
        carry_ref[...] = jnp.zeros_like(carry_ref)

    h = (_rms(x_ref[...]) * g_ref[...]).astype(BF16)
    bd = bd_ref[...]
    gains = gains_ref[...]

    ya = _dot(h, wa_ref[...])
    scale = HEAD_DIM ** -0.5
    aq_ref[...] = (_head_norm(ya[:, :MIX_W], bd, gains[0:1]) * scale).astype(BF16)
    ak_ref[...] = _head_norm(ya[:, MIX_W:2 * MIX_W], bd, gains[1:2]).astype(BF16)
    av_ref[...] = ya[:, 2 * MIX_W:].astype(BF16)

    z = _dot(h, wf_ref[...]) + fb_ref[...]
    logf = jnp.minimum(z, 0.0) - jnp.log1p(jnp.exp(-jnp.abs(z)))
    row = lax.broadcasted_iota(jnp.int32, logf.shape, 0)
    step = 1
    while step < tm:
        logf = logf + jnp.where(row >= step, pltpu.roll(logf, step, axis=0), 0.0)
        step *= 2
    cum = logf + carry_ref[...]
    carry_ref[...] = cum[tm - 1:tm, :]
    fcol_ref[...] = cum[:, :A_HEADS]
    for i in range(tm // tk):
        frow_ref[i] = cum[i * tk:(i + 1) * tk, :].T[:A_HEADS, :]

    yb = _dot(h, wb_ref[...])
    rep = MIX_W // LANES
    ra = jnp.tile(ra_ref[...], (1, rep))
    rm = jnp.tile(rm_ref[...], (1, rep))
    rp = jnp.tile(rp_ref[...], (1, rep))

    def rope(v):
        half = ROPE_DIM // 2
        return v * ra + pltpu.roll(v, MIX_W - half, axis=1) * rm + pltpu.roll(v, half, axis=1) * rp

    bq = rope(_head_norm(yb[:, :MIX_W], bd, gains[2:3]))
    bk = rope(_head_norm(yb[:, MIX_W:2 * MIX_W], bd, gains[3:4]))
    bq_ref[...] = (bq * scale).astype(BF16)
    bk_ref[...] = bk.astype(BF16)
    bv_ref[...] = yb[:, 2 * MIX_W:].astype(BF16)


def _rope_tables(positions):
    half = ROPE_DIM // 2
    inv = ROPE_THETA ** (-jnp.arange(0, ROPE_DIM, 2, dtype=F32) / ROPE_DIM)
    ang = positions.astype(F32).reshape(-1, 1) * inv
    cos, sin = jnp.cos(ang), jnp.sin(ang)
    t = cos.shape[0]
    pad = jnp.zeros((t, HEAD_DIM - ROPE_DIM), F32)
    zero = jnp.zeros((t, half), F32)
    ra = jnp.concatenate([cos, cos, pad + 1.0], axis=1)
    rm = jnp.concatenate([-sin, zero, pad], axis=1)
    rp = jnp.concatenate([zero, sin, pad], axis=1)
    dup = lambda a: jnp.concatenate([a, a], axis=1)
    return dup(ra), dup(rm), dup(rp)


def _even_prep(x, gain, w_in, f_bias, qk_gains, rope, bsz, seq, tm=512, tk=256):
    t, d = x.shape
    a_w = 3 * MIX_W
    wa = w_in[:, :a_w].astype(BF16)
    wf = jnp.pad(w_in[:, a_w:a_w + A_HEADS], ((0, 0), (0, LANES - A_HEADS))).astype(BF16)
    wb = w_in[:, a_w + A_HEADS:].astype(BF16)
    fb = jnp.pad(f_bias, (0, LANES - A_HEADS)).reshape(1, LANES)
    n_s = seq // tm
    tok = lambda w: pl.BlockSpec((tm, w), lambda b, s: (b * n_s + s, 0))
    stream = jax.ShapeDtypeStruct((t, MIX_W), BF16)
    return pl.pallas_call(
        functools.partial(_even_prep_kernel, tk=tk),
        out_shape=[stream] * 6 + [jax.ShapeDtypeStruct((t, A_HEADS), F32),
                                  jax.ShapeDtypeStruct((bsz, seq // tk, A_HEADS, tk), F32)],
        grid=(bsz, n_s),
        in_specs=[tok(d), _const_spec((1, d)), _const_spec((d, a_w)), _const_spec((d, LANES)),
                  _const_spec((1, LANES)), _const_spec((d, a_w)), _const_spec((4, MIX_W)),
                  _const_spec((MIX_W, MIX_W)), tok(LANES), tok(LANES), tok(LANES)],
        out_specs=[tok(MIX_W)] * 6 + [tok(A_HEADS),
                                      pl.BlockSpec((None, tm // tk, A_HEADS, tk),
                                                   lambda b, s: (b, s, 0, 0))],
        scratch_shapes=[pltpu.VMEM((1, LANES), F32)],
        compiler_params=_params(2),
        name="even_prep",
    )(x, gain.reshape(1, d), wa, wf, fb, wb, _tile_heads(qk_gains), _block_diag_ones(), *rope)


def _lane_lo(shape):
    return lax.broadcasted_iota(jnp.int32, shape, len(shape) - 1) < HEAD_DIM


def _online_step(s, v_blk, m, l, acc):
    m_new = jnp.maximum(m, jnp.max(s, axis=-1, keepdims=True))
    alpha = jnp.exp(m - m_new)
    p = jnp.exp(s - m_new)
    l = alpha * l + jnp.sum(p, axis=-1, keepdims=True)
    acc = alpha * acc + _dot(p.astype(BF16), v_blk)
    return m_new, l, acc


def _softmax_init(tq):
    return (jnp.full((tq, 1), NEG, F32), jnp.zeros((tq, 1), F32), jnp.zeros((tq, LANES), F32))


def _fox_kernel(q_ref, k_ref, v_ref, fcol_ref, frow_ref, o_ref):
    tq = q_ref.shape[0]
    pair, qi = pl.program_id(1), pl.program_id(2)
    q = q_ref[...]
    lo = _lane_lo(q.shape)
    fcol = fcol_ref[...]
    col_head = lax.broadcasted_iota(jnp.int32, fcol.shape, 1)
    row_head = lax.broadcasted_iota(jnp.int32, (A_HEADS, tq), 0)
    causal = (lax.broadcasted_iota(jnp.int32, (tq, tq), 0)
              >= lax.broadcasted_iota(jnp.int32, (tq, tq), 1))
    outs = []
    for sub in range(2):
        head = 2 * pair + sub
        qm = jnp.where(lo if sub == 0 else ~lo, q, jnp.zeros_like(q))
        f_t = jnp.sum(jnp.where(col_head == head, fcol, 0.0), axis=1, keepdims=True)

        def scores(j):
            k_blk = k_ref[pl.ds(pl.multiple_of(j * tq, tq), tq), :]
            f_s = jnp.sum(jnp.where(row_head == head, frow_ref[j], 0.0), axis=0, keepdims=True)
            return _dot_nt(qm, k_blk) + f_t - f_s

        def v_block(j):
            return v_ref[pl.ds(pl.multiple_of(j * tq, tq), tq), :]

        def body(j, carry):
            return _online_step(scores(j), v_block(j), *carry)

        carry = lax.fori_loop(0, qi, body, _softmax_init(tq))
        s = jnp.where(causal, scores(qi), NEG)
        _, l, acc = _online_step(s, v_block(qi), *carry)
        outs.append(acc / l)
    o_ref[...] = jnp.where(lo, outs[0], outs[1]).astype(o_ref.dtype)


def _fox_attention(q, k, v, fcol, frow, bsz, seq, tq=256):
    n_q = seq // tq
    q3, k3, v3 = (a.reshape(bsz, seq, MIX_W) for a in (q, k, v))
    q_spec = pl.BlockSpec((None, tq, LANES), lambda b, p, i: (b, i, p))
    kv_spec = pl.BlockSpec((None, seq, LANES), lambda b, p, i: (b, 0, p))
    out = pl.pallas_call(
        _fox_kernel,
        out_shape=jax.ShapeDtypeStruct((bsz, seq, MIX_W), BF16),
        grid=(bsz, MIX_W // LANES, n_q),
        in_specs=[q_spec, kv_spec, kv_spec,
                  pl.BlockSpec((None, tq, A_HEADS), lambda b, p, i: (b, i, 0)),
                  pl.BlockSpec((None, n_q, A_HEADS, tq), lambda b, p, i: (b, 0, 0, 0))],
        out_specs=q_spec,
        compiler_params=_params(3),
        name="fox_attention",
    )(q3, k3, v3, fcol.reshape(bsz, seq, A_HEADS), frow)
    return out.reshape(bsz * seq, MIX_W)


def _diff_kernel(q_ref, k_ref, v_ref, lam_ref, sg_ref, o_ref, *, lambda_init):
    tq = q_ref.shape[0]
    qi = pl.program_id(2)
    q = q_ref[...]
    lo = _lane_lo(q.shape)
    lp = lam_ref[...]
    lam = (jnp.exp(jnp.sum(lp[0:1] * lp[1:2], axis=1, keepdims=True))
           - jnp.exp(jnp.sum(lp[2:3] * lp[3:4], axis=1, keepdims=True)) + lambda_init)
    allowed = (lax.broadcasted_iota(jnp.int32, (tq, tq), 0) // CHUNK
               >= lax.broadcasted_iota(jnp.int32, (tq, tq), 1) // CHUNK)
    outs = []
    for sub in range(2):
        qm = jnp.where(lo if sub == 0 else ~lo, q, jnp.zeros_like(q))

        def scores(j):
            return _dot_nt(qm, k_ref[pl.ds(pl.multiple_of(j * tq, tq), tq), :])

        def v_block(j):
            return v_ref[pl.ds(pl.multiple_of(j * tq, tq), tq), :]

        def body(j, carry):
            return _online_step(scores(j), v_block(j), *carry)

        carry = lax.fori_loop(0, qi, body, _softmax_init(tq))
        s = jnp.where(allowed, scores(qi), NEG)
        _, l, acc = _online_step(s, v_block(qi), *carry)
        outs.append(acc / l)
    o = outs[0] - lam * outs[1]
    o_ref[...] = (_rms(o) * sg_ref[...] * (1.0 - lambda_init)).astype(o_ref.dtype)


def _diff_attention(q, k, v, lam_params, subln_gain, lambda_init, bsz, seq, tq=256):
    n_q = seq // tq
    q3, k3, v3 = (a.reshape(bsz, seq, MIX_W) for a in (q, k, v))
    q_spec = pl.BlockSpec((None, tq, LANES), lambda b, h, i: (b, i, h))
    kv_spec = pl.BlockSpec((None, seq, LANES), lambda b, h, i: (b, 0, h))
    out = pl.pallas_call(
        functools.partial(_diff_kernel, lambda_init=lambda_init),
        out_shape=jax.ShapeDtypeStruct((bsz, seq, MIX_W), BF16),
        grid=(bsz, MIX_W // LANES, n_q),
        in_specs=[q_spec, kv_spec, kv_spec, _const_spec((4, HEAD_DIM)), _const_spec((1, LANES))],
        out_specs=q_spec,
        compiler_params=_params(3),
        name="diff_attention",
    )(q3, k3, v3, lam_params, subln_gain.reshape(1, LANES))
    return out.reshape(bsz * seq, MIX_W)


def _odd_prep_kernel(x_ref, g_ref, wc_ref, wd_ref, cw_ref, gains_ref, bd_ref,
                     c_ref, dq_ref, dk_ref, dv_ref, carry_ref):
    tm = x_ref.shape[0]

    @pl.when(pl.program_id(1) == 0)
    def _():
        carry_ref[...] = jnp.zeros_like(carry_ref)

    h = (_rms(x_ref[...]) * g_ref[...]).astype(BF16)
    yc = _dot(h, wc_ref[...])
    u = yc[:, MIX_W:2 * MIX_W] * yc[:, 2 * MIX_W:]
    prev = carry_ref[...]
    carry_ref[...] = u[tm - 8:, :]
    row = lax.broadcasted_iota(jnp.int32, u.shape, 0)
    u1 = jnp.where(row == 0, prev[7:8], pltpu.roll(u, 1, axis=0))
    u2 = jnp.where(row == 0, prev[6:7], jnp.where(row == 1, prev[7:8], pltpu.roll(u, 2, axis=0)))
    cw = cw_ref[...]
    conv = cw[0:1] * u2 + cw[1:2] * u1 + cw[2:3] * u
    c_ref[...] = (yc[:, :MIX_W] * conv).astype(BF16)

    yd = _dot(h, wd_ref[...])
    bd = bd_ref[...]
    gains = gains_ref[...]
    dq_ref[...] = (_head_norm(yd[:, :MIX_W], bd, gains[0:1]) * HEAD_DIM ** -0.5).astype(BF16)
    dk_ref[...] = _head_norm(yd[:, MIX_W:2 * MIX_W], bd, gains[1:2]).astype(BF16)
    dv_ref[...] = yd[:, 2 * MIX_W:].astype(BF16)


def _odd_prep(x, gain, w_in, conv_w, qk_gains, bsz, seq, tm=512):
    t, d = x.shape
    w3 = 3 * MIX_W
    n_s = seq // tm
    tok = lambda w: pl.BlockSpec((tm, w), lambda b, s: (b * n_s + s, 0))
    stream = jax.ShapeDtypeStruct((t, MIX_W), BF16)
    return pl.pallas_call(
        _odd_prep_kernel,
        out_shape=[stream] * 4,
        grid=(bsz, n_s),
        in_specs=[tok(d), _const_spec((1, d)), _const_spec((d, w3)), _const_spec((d, w3)),
                  _const_spec((3, MIX_W)), _const_spec((2, MIX_W)), _const_spec((MIX_W, MIX_W))],
        out_specs=[tok(MIX_W)] * 4,
        scratch_shapes=[pltpu.VMEM((8, MIX_W), F32)],
        compiler_params=_params(2),
        name="odd_prep",
    )(x, gain.reshape(1, d), w_in[:, :w3].astype(BF16), w_in[:, w3:].astype(BF16), conv_w,
      _tile_heads(qk_gains), _block_diag_ones())


def _band_kernel(q_ref, k_ref, v_ref, bias_ref, o_ref):
    tq = q_ref.shape[0]
    width = tq + D_LEFT
    qs = pl.multiple_of(pl.program_id(2) * tq, tq)
    q = q_ref[...]
    lo = _lane_lo(q.shape)
    k_win = k_ref[pl.ds(qs, width), :]
    v_win = v_ref[pl.ds(qs, width), :]
    valid = lax.broadcasted_iota(jnp.int32, (tq, width), 1) + qs >= D_LEFT
    outs = []
    for sub in range(2):
        qm = jnp.where(lo if sub == 0 else ~lo, q, jnp.zeros_like(q))
        s = jnp.where(valid, _dot_nt(qm, k_win) + bias_ref[sub], NEG)
        p = jnp.exp(s - jnp.max(s, axis=-1, keepdims=True))
        l = jnp.sum(p, axis=-1, keepdims=True)
        outs.append(_dot(p.astype(BF16), v_win) / l)
    o_ref[...] = jnp.where(lo, outs[0], outs[1]).astype(o_ref.dtype)


def _band_bias(rel_table, tq):
    width = tq + D_LEFT
    t_loc = jnp.arange(tq)[:, None]
    s_loc = jnp.arange(width)[None, :] - D_LEFT
    rel = t_loc - s_loc
    idx = jnp.clip(rel, -REL_CLIP, REL_CLIP) + REL_CLIP
    chunk_start = (t_loc // CHUNK) * CHUNK
    in_band = (s_loc >= chunk_start - D_LEFT) & (s_loc < chunk_start + CHUNK)
    return jnp.where(in_band[None], rel_table[:, idx].astype(F32), NEG)


def _band_attention(q, k, v, rel_table, bsz, seq, tq=256):
    n_q = seq // tq
    pad = ((0, 0), (D_LEFT, 0), (0, 0))
    q3 = q.reshape(bsz, seq, MIX_W)
    kp = jnp.pad(k.reshape(bsz, seq, MIX_W), pad)
    vp = jnp.pad(v.reshape(bsz, seq, MIX_W), pad)
    q_spec = pl.BlockSpec((None, tq, LANES), lambda b, p, i: (b, i, p))
    kv_spec = pl.BlockSpec((None, seq + D_LEFT, LANES), lambda b, p, i: (b, 0, p))
    out = pl.pallas_call(
        _band_kernel,
        out_shape=jax.ShapeDtypeStruct((bsz, seq, MIX_W), BF16),
        grid=(bsz, MIX_W // LANES, n_q),
        in_specs=[q_spec, kv_spec, kv_spec,
                  pl.BlockSpec((2, tq, tq + D_LEFT), lambda b, p, i: (p, 0, 0))],
        out_specs=q_spec,
        compiler_params=_params(3),
        name="band_attention",
    )(q3, kp, vp, _band_bias(rel_table, tq))
    return out.reshape(bsz * seq, MIX_W)


def _proj_kernel(x_ref, a_ref, b_ref, wa_ref, wb_ref, o_ref):
    o_ref[...] = x_ref[...] + _dot(a_ref[...], wa_ref[...]) + _dot(b_ref[...], wb_ref[...])


def _proj_residual(x, a, b, w_out, tm=1024):
    t, d = x.shape
    w = w_out.astype(BF16)
    tok = lambda width: pl.BlockSpec((tm, width), lambda i: (i, 0))
    return pl.pallas_call(
        _proj_kernel,
        out_shape=jax.ShapeDtypeStruct((t, d), F32),
        grid=(t // tm,),
        in_specs=[tok(d), tok(MIX_W), tok(MIX_W), _const_spec((MIX_W, d)), _const_spec((MIX_W, d))],
        out_specs=tok(d),
        compiler_params=_params(1),
        name="proj_residual",
    )(x, a, b, w[:MIX_W], w[MIX_W:])


def _mem_kv_kernel(mem_ref, g_ref, w_ref, kg_ref, k_ref, v_ref):
    mem_n = (_rms(mem_ref[...]) * g_ref[...]).astype(BF16)
    kv = _dot(mem_n, w_ref[...])
    kg = kg_ref[...]
    for hd in range(X_HEADS):
        sl = slice(hd * X_HEAD_DIM, (hd + 1) * X_HEAD_DIM)
        k_ref[:, sl] = (_rms(kv[:, sl]) * kg).astype(BF16)
    v_ref[...] = kv[:, D_MODEL:].astype(BF16)


def _mem_kv(mem, gain, w_kv, k_gain):
    bsz, n_mem, d = mem.shape
    blk = pl.BlockSpec((None, n_mem, d), lambda b: (b, 0, 0))
    out = jax.ShapeDtypeStruct((bsz, n_mem, d), BF16)
    return pl.pallas_call(
        _mem_kv_kernel,
        out_shape=[out, out],
        grid=(bsz,),
        in_specs=[blk, _const_spec((1, d)), _const_spec((d, 2 * d)), _const_spec((1, X_HEAD_DIM))],
        out_specs=[blk, blk],
        compiler_params=_params(1),
        name="mem_kv",
    )(mem, gain.reshape(1, d), w_kv.astype(BF16), k_gain.reshape(1, X_HEAD_DIM))


def _cross_kernel(x_ref, g_ref, wq_ref, qg_ref, k_ref, v_ref, wo_ref, o_ref):
    x = x_ref[...]
    h = (_rms(x) * g_ref[...]).astype(BF16)
    q = _dot(h, wq_ref[...])
    qg = qg_ref[...]
    heads = []
    for hd in range(X_HEADS):
        sl = slice(hd * X_HEAD_DIM, (hd + 1) * X_HEAD_DIM)
        qh = (_rms(q[:, sl]) * qg * X_HEAD_DIM ** -0.5).astype(BF16)
        s = _dot_nt(qh, k_ref[:, sl])
        p = jnp.exp(s - jnp.max(s, axis=-1, keepdims=True))
        l = jnp.sum(p, axis=-1, keepdims=True)
        heads.append((_dot(p.astype(BF16), v_ref[:, sl]) / l).astype(BF16))
    o_ref[...] = x + _dot(jnp.concatenate(heads, axis=1), wo_ref[...])


def _cross_attention(x, gain, w_q, q_gain, k, v, w_o, bsz, seq, tq=512):
    t, d = x.shape
    n_s = seq // tq
    n_mem = k.shape[1]
    tok = pl.BlockSpec((tq, d), lambda b, s: (b * n_s + s, 0))
    mem_spec = pl.BlockSpec((None, n_mem, d), lambda b, s: (b, 0, 0))
    return pl.pallas_call(
        _cross_kernel,
        out_shape=jax.ShapeDtypeStruct((t, d), F32),
        grid=(bsz, n_s),
        in_specs=[tok, _const_spec((1, d)), _const_spec((d, d)), _const_spec((1, X_HEAD_DIM)),
                  mem_spec, mem_spec, _const_spec((d, d))],
        out_specs=tok,
        compiler_params=_params(2),
        name="cross_attention",
    )(x, gain.reshape(1, d), w_q.astype(BF16), q_gain.reshape(1, X_HEAD_DIM), k, v,
      w_o.astype(BF16))


def kernel(x, mem, positions, ln_gains, ffn1_w_in, ffn1_w_out, ffn2_w_in, ffn2_w_out, even_w_in, even_f_bias, even_qk_gains, even_lambda, even_subln_gain, even_w_out, odd_w_in, odd_conv_w, odd_qk_gains, odd_rel_bias, odd_w_out, x_w_q, x_w_kv, x_qk_gains, x_w_o):
    bsz, seq, d = x.shape
    depth = ln_gains.shape[0]
    rope = _rope_tables(positions)
    x = x.reshape(bsz * seq, d)
    for layer in range(depth):
        g = ln_gains[layer]
        x = _ffn(x, g[0], ffn1_w_in[layer], ffn1_w_out[layer])
        if layer % 2 == 0:
            e = layer // 2
            lambda_init = 0.8 - 0.6 * math.exp(-0.3 * layer)
            aq, ak, av, bq, bk, bv, fcol, frow = _even_prep(
                x, g[1], even_w_in[e], even_f_bias[e], even_qk_gains[e], rope, bsz, seq)
            left = _fox_attention(aq, ak, av, fcol, frow, bsz, seq)
            right = _diff_attention(bq, bk, bv, even_lambda[e], even_subln_gain[e], lambda_init,
                                    bsz, seq)
            x = _proj_residual(x, left, right, even_w_out[e])
        else:
            o = layer // 2
            left, dq, dk, dv = _odd_prep(x, g[1], odd_w_in[o], odd_conv_w[o], odd_qk_gains[o],
                                         bsz, seq)
            right = _band_attention(dq, dk, dv, odd_rel_bias[o], bsz, seq)
            x = _proj_residual(x, left, right, odd_w_out[o])
        mk, mv = _mem_kv(mem, g[3], x_w_kv[layer], x_qk_gains[layer, 1])
        x = _cross_attention(x, g[2], x_w_q[layer], x_qk_gains[layer, 0], mk, mv, x_w_o[layer],
                             bsz, seq)
        x = _ffn(x, g[4], ffn2_w_in[layer], ffn2_w_out[layer])
    return x.reshape(bsz, seq, d)
```

```python
import functools
import math

import jax
import jax.numpy as jnp
from jax import lax
from jax.experimental import pallas as pl
from jax.experimental.pallas import tpu as pltpu

D_MODEL = 1024
CHUNK = 64
HEAD_DIM = 64
ROPE_DIM = HEAD_DIM // 4
ROPE_THETA = 500000.0
RMS_EPS = 1e-6
D_FF = 2816
A_HEADS = 8
MIX_W = 512
LANES = 128
D_LEFT = 8 * CHUNK
REL_CLIP = 128
X_HEADS = 4
X_HEAD_DIM = D_MODEL // X_HEADS
NEG = -1e30
LOG2E = math.log2(math.e)
Q_SCALE = HEAD_DIM ** -0.5 * LOG2E

VMEM_LIMIT = 56 * 1024 * 1024
FFN_CHUNKS = (768, 768, 768, 512)
TOKEN_TILE = 512
ATT_TILE = 256
GROUPS_PER_STEP = 2

BF16 = jnp.bfloat16
F32 = jnp.float32


def _params(n_axes):
    return pltpu.CompilerParams(dimension_semantics=("arbitrary",) * n_axes,
                                vmem_limit_bytes=VMEM_LIMIT)


def _const_spec(shape):
    nd = len(shape)
    return pl.BlockSpec(shape, lambda *_: (0,) * nd, pipeline_mode=pl.Buffered(1))


def _rms(x):
    return x * lax.rsqrt(jnp.mean(x * x, axis=-1, keepdims=True) + RMS_EPS)


def _dot(a, b):
    return jnp.dot(a, b, preferred_element_type=F32)


def _dot_nt(a, b):
    return lax.dot_general(a, b, (((1,), (1,)), ((), ())), preferred_element_type=F32)


def _ffn_kernel(x_ref, g_ref, wg_ref, wu_ref, wo_ref, o_ref):
    x = x_ref[...]
    xn = (_rms(x) * g_ref[...]).astype(BF16)
    y = jnp.zeros_like(x)
    start = 0
    for width in FFN_CHUNKS:
        gate = _dot(xn, wg_ref[:, start:start + width])
        up = _dot(xn, wu_ref[:, start:start + width])
        act = (gate * (1.0 / (1.0 + jnp.exp(-gate))) * up).astype(BF16)
        y = y + _dot(act, wo_ref[start:start + width, :])
        start += width
    o_ref[...] = x + 0.5 * y


def _ffn(x, gain, w_in, w_out):
    t, d = x.shape
    tm = TOKEN_TILE
    wg = w_in[:, :D_FF].astype(BF16)
    wu = w_in[:, D_FF:].astype(BF16)
    wo = w_out.astype(BF16)
    return pl.pallas_call(
        _ffn_kernel,
        out_shape=jax.ShapeDtypeStruct((t, d), F32),
        grid=(t // tm,),
        in_specs=[pl.BlockSpec((tm, d), lambda i: (i, 0)),
                  _const_spec((1, d)),
                  _const_spec((d, D_FF)), _const_spec((d, D_FF)), _const_spec((D_FF, d))],
        out_specs=pl.BlockSpec((tm, d), lambda i: (i, 0)),
        compiler_params=_params(1),
        name="ffn",
    )(x, gain.reshape(1, d), wg, wu, wo)


def _head_norm(y, ones_bd, gain):
    msq = _dot((y * y).astype(BF16), ones_bd) * (1.0 / HEAD_DIM)
    return y * lax.rsqrt(msq + RMS_EPS) * gain


def _block_diag_ones():
    r = jnp.arange(MIX_W) // HEAD_DIM
    return (r[:, None] == r[None, :]).astype(BF16)


def _tile_heads(v):
    return jnp.tile(v, (1, MIX_W // HEAD_DIM))


def _store_transposed(vt_ref, v):
    for i in range(v.shape[0] // ATT_TILE):
        vt_ref[i] = v[i * ATT_TILE:(i + 1) * ATT_TILE, :].T.astype(BF16)


def _split3(f):
    hi = f.astype(BF16).astype(F32)
    rest = f - hi
    mid = rest.astype(BF16).astype(F32)
    return hi, mid, rest - mid


def _even_prep_kernel(x_ref, g_ref, wa_ref, wf_ref, fb_ref, wb_ref, gains_ref, bd_ref,
                      ra_ref, rm_ref, rp_ref,
                      aq_ref, ak_ref, avt_ref, bq_ref, bk_ref, bvt_ref, carry_ref):
    tm = x_ref.shape[0]

    @pl.when(pl.program_id(1) == 0)
    def _():
        carry_ref[...] = jnp.zeros_like(carry_ref)

    h = (_rms(x_ref[...]) * g_ref[...]).astype(BF16)
    bd = bd_ref[...]
    gains = gains_ref[...]

    ya = _dot(h, wa_ref[...])
    qn = _head_norm(ya[:, :MIX_W], bd, gains[0:1]) * Q_SCALE
    kn = _head_norm(ya[:, MIX_W:2 * MIX_W], bd, gains[1:2])
    _store_transposed(avt_ref, ya[:, 2 * MIX_W:])

    z = _dot(h, wf_ref[...]) + fb_ref[...]
    logf = jnp.minimum(z, 0.0) - jnp.log1p(jnp.exp(-jnp.abs(z)))
    row = lax.broadcasted_iota(jnp.int32, logf.shape, 0)
    step = 1
    while step < tm:
        logf = logf + jnp.where(row >= step, pltpu.roll(logf, step, axis=0), 0.0)
        step *= 2
    cum = logf + carry_ref[...]
    carry_ref[...] = cum[tm - 1:tm, :]

    hi, mid, lo = _split3(cum * LOG2E)
    lane = lax.broadcasted_iota(jnp.int32, (tm, LANES), 1)
    is_hi = (lane == HEAD_DIM) | (lane == HEAD_DIM + 3)
    is_mid = (lane == HEAD_DIM + 1) | (lane == HEAD_DIM + 4)
    ones_hi = jnp.where(lane < HEAD_DIM + 6, 1.0, 0.0)
    for hd in range(A_HEADS):
        pieces = jnp.where(is_hi, hi[:, hd:hd + 1],
                           jnp.where(is_mid, mid[:, hd:hd + 1], lo[:, hd:hd + 1]))
        blk = slice((hd // 2) * LANES, (hd // 2 + 1) * LANES)
        sq, sk = qn[:, blk], kn[:, blk]
        if hd % 2:
            sq, sk = pltpu.roll(sq, HEAD_DIM, axis=1), pltpu.roll(sk, HEAD_DIM, axis=1)
        qa = jnp.where(lane < HEAD_DIM, sq, jnp.where(lane < HEAD_DIM + 3, pieces, ones_hi))
        ka = jnp.where(lane < HEAD_DIM, sk, jnp.where(lane < HEAD_DIM + 3, 1.0,
                                                      jnp.where(lane < HEAD_DIM + 6, -pieces, 0.0)))
        aq_ref[:, hd * LANES:(hd + 1) * LANES] = qa.astype(BF16)
        ak_ref[:, hd * LANES:(hd + 1) * LANES] = ka.astype(BF16)

    yb = _dot(h, wb_ref[...])
    rep = MIX_W // LANES
    ra = jnp.tile(ra_ref[...], (1, rep))
    rm = jnp.tile(rm_ref[...], (1, rep))
    rp = jnp.tile(rp_ref[...], (1, rep))

    def rope(v):
        half = ROPE_DIM // 2
        return v * ra + pltpu.roll(v, MIX_W - half, axis=1) * rm + pltpu.roll(v, half, axis=1) * rp

    bq = rope(_head_norm(yb[:, :MIX_W], bd, gains[2:3]))
    bk = rope(_head_norm(yb[:, MIX_W:2 * MIX_W], bd, gains[3:4]))
    bq_ref[...] = (bq * Q_SCALE).astype(BF16)
    bk_ref[...] = bk.astype(BF16)
    _store_transposed(bvt_ref, yb[:, 2 * MIX_W:])


def _rope_tables(positions):
    half = ROPE_DIM // 2
    inv = ROPE_THETA ** (-jnp.arange(0, ROPE_DIM, 2, dtype=F32) / ROPE_DIM)
    ang = positions.astype(F32).reshape(-1, 1) * inv
    cos, sin = jnp.cos(ang), jnp.sin(ang)
    t = cos.shape[0]
    pad = jnp.zeros((t, HEAD_DIM - ROPE_DIM), F32)
    zero = jnp.zeros((t, half), F32)
    ra = jnp.concatenate([cos, cos, pad + 1.0], axis=1)
    rm = jnp.concatenate([-sin, zero, pad], axis=1)
    rp = jnp.concatenate([zero, sin, pad], axis=1)
    dup = lambda a: jnp.concatenate([a, a], axis=1)
    return dup(ra), dup(rm), dup(rp)


def _even_prep(x, gain, w_in, f_bias, qk_gains, rope, bsz, seq):
    t, d = x.shape
    tm = TOKEN_TILE
    a_w = 3 * MIX_W
    wa = w_in[:, :a_w].astype(BF16)
    wf = jnp.pad(w_in[:, a_w:a_w + A_HEADS], ((0, 0), (0, LANES - A_HEADS))).astype(BF16)
    wb = w_in[:, a_w + A_HEADS:].astype(BF16)
    fb = jnp.pad(f_bias, (0, LANES - A_HEADS)).reshape(1, LANES)
    n_s = seq // tm
    tok = lambda w: pl.BlockSpec((tm, w), lambda b, s: (b * n_s + s, 0))
    vt_spec = pl.BlockSpec((None, tm // ATT_TILE, MIX_W, ATT_TILE), lambda b, s: (b, s, 0, 0))
    stream = lambda w: jax.ShapeDtypeStruct((t, w), BF16)
    vt_shape = jax.ShapeDtypeStruct((bsz, seq // ATT_TILE, MIX_W, ATT_TILE), BF16)
    return pl.pallas_call(
        _even_prep_kernel,
        out_shape=[stream(A_HEADS * LANES), stream(A_HEADS * LANES), vt_shape,
                   stream(MIX_W), stream(MIX_W), vt_shape],
        grid=(bsz, n_s),
        in_specs=[tok(d), _const_spec((1, d)), _const_spec((d, a_w)), _const_spec((d, LANES)),
                  _const_spec((1, LANES)), _const_spec((d, a_w)), _const_spec((4, MIX_W)),
                  _const_spec((MIX_W, MIX_W)), tok(LANES), tok(LANES), tok(LANES)],
        out_specs=[tok(A_HEADS * LANES), tok(A_HEADS * LANES), vt_spec,
                   tok(MIX_W), tok(MIX_W), vt_spec],
        scratch_shapes=[pltpu.VMEM((1, LANES), F32)],
        compiler_params=_params(2),
        name="even_prep",
    )(x, gain.reshape(1, d), wa, wf, fb, wb, _tile_heads(qk_gains), _block_diag_ones(), *rope)


def _lane_lo(shape):
    return lax.broadcasted_iota(jnp.int32, shape, len(shape) - 1) < HEAD_DIM


def _key_rows(g):
    return pl.ds(pl.multiple_of(g * ATT_TILE, ATT_TILE), ATT_TILE)


def _two_pass_attention(qi, n_sub, scores, mask_delta, group_delta, value_t, s_ref, v_rows):
    tq = ATT_TILE
    n_iter = (qi + GROUPS_PER_STEP) // GROUPS_PER_STEP

    def pass1(it, ms):
        ms = list(ms)
        for u in range(GROUPS_PER_STEP):
            g = it * GROUPS_PER_STEP + u
            visible = mask_delta <= (qi - g) * group_delta
            for sub in range(n_sub):
                s = jnp.where(visible, scores(g, sub), NEG)
                s_ref[sub, _key_rows(g), :] = s
                ms[sub] = jnp.maximum(ms[sub], jnp.max(s, axis=0, keepdims=True))
        return tuple(ms)

    maxes = lax.fori_loop(0, n_iter, pass1, (jnp.full((1, tq), NEG, F32),) * n_sub)

    def pass2(it, carry):
        carry = list(carry)
        for u in range(GROUPS_PER_STEP):
            g = it * GROUPS_PER_STEP + u
            for sub in range(n_sub):
                l, acc = carry[sub]
                p = jnp.exp2(s_ref[sub, _key_rows(g), :] - maxes[sub])
                l = l + jnp.sum(p, axis=0, keepdims=True)
                acc = acc + _dot(value_t(g, sub), p.astype(BF16))
                carry[sub] = (l, acc)
        return tuple(carry)

    init = ((jnp.zeros((1, tq), F32), jnp.zeros((v_rows, tq), F32)),) * n_sub
    return lax.fori_loop(0, n_iter, pass2, init)


def _att_scratch(seq):
    return pltpu.VMEM((2, seq, ATT_TILE), F32)


def _fox_kernel(q_ref, k_ref, vt_ref, o_ref, s_ref):
    tq = ATT_TILE
    q_heads = (q_ref[:, :LANES], q_ref[:, LANES:])

    def scores(g, sub):
        return _dot_nt(k_ref[_key_rows(g), sub * LANES:(sub + 1) * LANES], q_heads[sub])

    def value_t(g, sub):
        return vt_ref[g, sub * HEAD_DIM:(sub + 1) * HEAD_DIM, :]

    key_minus_query = (lax.broadcasted_iota(jnp.int32, (tq, tq), 0)
                       - lax.broadcasted_iota(jnp.int32, (tq, tq), 1))
    res = _two_pass_attention(pl.program_id(2), 2, scores, key_minus_query, tq, value_t, s_ref,
                              HEAD_DIM)
    o_t = jnp.concatenate([acc / l for l, acc in res], axis=0)
    o_ref[...] = o_t.T.astype(o_ref.dtype)


def _fox_attention(q, k, vt, bsz, seq):
    tq = ATT_TILE
    n_q = seq // tq
    q3 = q.reshape(bsz, seq, A_HEADS * LANES)
    k3 = k.reshape(bsz, seq, A_HEADS * LANES)
    out_spec = pl.BlockSpec((None, tq, LANES), lambda b, p, i: (b, i, p))
    out = pl.pallas_call(
        _fox_kernel,
        out_shape=jax.ShapeDtypeStruct((bsz, seq, MIX_W), BF16),
        grid=(bsz, A_HEADS // 2, n_q),
        in_specs=[pl.BlockSpec((None, tq, 2 * LANES), lambda b, p, i: (b, i, p)),
                  pl.BlockSpec((None, seq, 2 * LANES), lambda b, p, i: (b, 0, p)),
                  pl.BlockSpec((None, n_q, LANES, tq), lambda b, p, i: (b, 0, p, 0))],
        out_specs=out_spec,
        scratch_shapes=[_att_scratch(seq)],
        compiler_params=_params(3),
        name="fox_attention",
    )(q3, k3, vt)
    return out.reshape(bsz * seq, MIX_W)


def _diff_kernel(q_ref, k_ref, vt_ref, lam_ref, sg_ref, o_ref, s_ref, *, lambda_init):
    tq = ATT_TILE
    q = q_ref[...]
    lo = _lane_lo(q.shape)
    q_subs = (jnp.where(lo, q, jnp.zeros_like(q)), jnp.where(lo, jnp.zeros_like(q), q))
    lp = lam_ref[...]
    lam = (jnp.exp(jnp.sum(lp[0:1] * lp[1:2], axis=1, keepdims=True))
           - jnp.exp(jnp.sum(lp[2:3] * lp[3:4], axis=1, keepdims=True)) + lambda_init)

    def scores(g, sub):
        return _dot_nt(k_ref[_key_rows(g), :], q_subs[sub])

    def value_t(g, sub):
        return vt_ref[g]

    chunk_delta = (lax.broadcasted_iota(jnp.int32, (tq, tq), 0) // CHUNK
                   - lax.broadcasted_iota(jnp.int32, (tq, tq), 1) // CHUNK)
    (l1, acc1), (l2, acc2) = _two_pass_attention(pl.program_id(2), 2, scores, chunk_delta,
                                                 tq // CHUNK, value_t, s_ref, LANES)
    o = (acc1 / l1 - lam * (acc2 / l2)).T
    o_ref[...] = (_rms(o) * sg_ref[...] * (1.0 - lambda_init)).astype(o_ref.dtype)


def _diff_attention(q, k, vt, lam_params, subln_gain, lambda_init, bsz, seq):
    tq = ATT_TILE
    n_q = seq // tq
    q3, k3 = (a.reshape(bsz, seq, MIX_W) for a in (q, k))
    q_spec = pl.BlockSpec((None, tq, LANES), lambda b, h, i: (b, i, h))
    out = pl.pallas_call(
        functools.partial(_diff_kernel, lambda_init=lambda_init),
        out_shape=jax.ShapeDtypeStruct((bsz, seq, MIX_W), BF16),
        grid=(bsz, MIX_W // LANES, n_q),
        in_specs=[q_spec,
                  pl.BlockSpec((None, seq, LANES), lambda b, h, i: (b, 0, h)),
                  pl.BlockSpec((None, n_q, LANES, tq), lambda b, h, i: (b, 0, h, 0)),
                  _const_spec((4, HEAD_DIM)), _const_spec((1, LANES))],
        out_specs=q_spec,
        scratch_shapes=[_att_scratch(seq)],
        compiler_params=_params(3),
        name="diff_attention",
    )(q3, k3, vt, lam_params, subln_gain.reshape(1, LANES))
    return out.reshape(bsz * seq, MIX_W)


def _odd_prep_kernel(x_ref, g_ref, wc_ref, wd_ref, cw_ref, gains_ref, bd_ref,
                     c_ref, dq_ref, dk_ref, dvt_ref, carry_ref):
    tm = x_ref.shape[0]

    @pl.when(pl.program_id(1) == 0)
    def _():
        carry_ref[...] = jnp.zeros_like(carry_ref)

    h = (_rms(x_ref[...]) * g_ref[...]).astype(BF16)
    yc = _dot(h, wc_ref[...])
    u = yc[:, MIX_W:2 * MIX_W] * yc[:, 2 * MIX_W:]
    prev = carry_ref[...]
    carry_ref[...] = u[tm - 8:, :]
    row = lax.broadcasted_iota(jnp.int32, u.shape, 0)
    u1 = jnp.where(row == 0, prev[7:8], pltpu.roll(u, 1, axis=0))
    u2 = jnp.where(row == 0, prev[6:7], jnp.where(row == 1, prev[7:8], pltpu.roll(u, 2, axis=0)))
    cw = cw_ref[...]
    conv = cw[0:1] * u2 + cw[1:2] * u1 + cw[2:3] * u
    c_ref[...] = (yc[:, :MIX_W] * conv).astype(BF16)

    yd = _dot(h, wd_ref[...])
    bd = bd_ref[...]
    gains = gains_ref[...]
    dq_ref[...] = (_head_norm(yd[:, :MIX_W], bd, gains[0:1]) * Q_SCALE).astype(BF16)
    dk_ref[...] = _head_norm(yd[:, MIX_W:2 * MIX_W], bd, gains[1:2]).astype(BF16)
    _store_transposed(dvt_ref, yd[:, 2 * MIX_W:])


def _odd_prep(x, gain, w_in, conv_w, qk_gains, bsz, seq):
    t, d = x.shape
    tm = TOKEN_TILE
    w3 = 3 * MIX_W
    n_s = seq // tm
    tok = lambda w: pl.BlockSpec((tm, w), lambda b, s: (b * n_s + s, 0))
    stream = jax.ShapeDtypeStruct((t, MIX_W), BF16)
    return pl.pallas_call(
        _odd_prep_kernel,
        out_shape=[stream] * 3 + [jax.ShapeDtypeStruct((bsz, seq // ATT_TILE, MIX_W, ATT_TILE), BF16)],
        grid=(bsz, n_s),
        in_specs=[tok(d), _const_spec((1, d)), _const_spec((d, w3)), _const_spec((d, w3)),
                  _const_spec((3, MIX_W)), _const_spec((2, MIX_W)), _const_spec((MIX_W, MIX_W))],
        out_specs=[tok(MIX_W)] * 3 + [pl.BlockSpec((None, tm // ATT_TILE, MIX_W, ATT_TILE),
                                                   lambda b, s: (b, s, 0, 0))],
        scratch_shapes=[pltpu.VMEM((8, MIX_W), F32)],
        compiler_params=_params(2),
        name="odd_prep",
    )(x, gain.reshape(1, d), w_in[:, :w3].astype(BF16), w_in[:, w3:].astype(BF16), conv_w,
      _tile_heads(qk_gains), _block_diag_ones())


BAND_GROUPS = 1 + D_LEFT // ATT_TILE


def _band_kernel(q_ref, k_ref, vt_ref, bias_ref, o_ref):
    tq = ATT_TILE
    width = BAND_GROUPS * tq
    qi = pl.program_id(2)
    qs = pl.multiple_of(qi * tq, tq)
    q = q_ref[...]
    lo = _lane_lo(q.shape)
    q_subs = (jnp.where(lo, q, jnp.zeros_like(q)), jnp.where(lo, jnp.zeros_like(q), q))
    k_win = k_ref[pl.ds(qs, width), :]
    valid = lax.broadcasted_iota(jnp.int32, (width, tq), 0) + qs >= D_LEFT
    outs = []
    for sub in range(2):
        s = jnp.where(valid, _dot_nt(k_win, q_subs[sub]) + bias_ref[sub], NEG)
        p = jnp.exp2(s - jnp.max(s, axis=0, keepdims=True))
        l = jnp.sum(p, axis=0, keepdims=True)
        pb = p.astype(BF16)
        acc = jnp.zeros((HEAD_DIM, tq), F32)
        for i in range(BAND_GROUPS):
            acc = acc + _dot(vt_ref[qi + i, sub * HEAD_DIM:(sub + 1) * HEAD_DIM, :],
                             pb[i * tq:(i + 1) * tq, :])
        outs.append(acc / l)
    o_ref[...] = jnp.concatenate(outs, axis=0).T.astype(o_ref.dtype)


def _band_bias_t(rel_table):
    tq = ATT_TILE
    width = BAND_GROUPS * tq
    n_head = rel_table.shape[0]
    n_lo = tq - 1 - REL_CLIP
    n_hi = width + tq - 1 - n_lo - (2 * REL_CLIP + 1)
    w = jnp.concatenate([jnp.repeat(rel_table[:, :1], n_lo, axis=1), rel_table,
                         jnp.repeat(rel_table[:, -1:], n_hi, axis=1)], axis=1)
    length = w.shape[1]
    hankel = jnp.tile(w, (1, width + 1))[:, :width * (length + 1)].reshape(n_head, width, length + 1)
    hankel = hankel[:, :, :tq]
    toeplitz = hankel[:, ::-1, :]
    key = jnp.arange(width)[:, None] - D_LEFT
    chunk_start = (jnp.arange(tq)[None, :] // CHUNK) * CHUNK
    in_band = (key >= chunk_start - D_LEFT) & (key < chunk_start + CHUNK)
    return jnp.where(in_band[None], toeplitz.astype(F32) * LOG2E, NEG)


def _band_attention(q, k, vt, rel_table, bsz, seq):
    tq = ATT_TILE
    n_q = seq // tq
    q3 = q.reshape(bsz, seq, MIX_W)
    kp = jnp.pad(k.reshape(bsz, seq, MIX_W), ((0, 0), (D_LEFT, 0), (0, 0)))
    vtp = jnp.pad(vt, ((0, 0), (BAND_GROUPS - 1, 0), (0, 0), (0, 0)))
    q_spec = pl.BlockSpec((None, tq, LANES), lambda b, p, i: (b, i, p))
    out = pl.pallas_call(
        _band_kernel,
        out_shape=jax.ShapeDtypeStruct((bsz, seq, MIX_W), BF16),
        grid=(bsz, MIX_W // LANES, n_q),
        in_specs=[q_spec,
                  pl.BlockSpec((None, seq + D_LEFT, LANES), lambda b, p, i: (b, 0, p)),
                  pl.BlockSpec((None, n_q + BAND_GROUPS - 1, LANES, tq), lambda b, p, i: (b, 0, p, 0)),
                  pl.BlockSpec((2, BAND_GROUPS * tq, tq), lambda b, p, i: (p, 0, 0))],
        out_specs=q_spec,
        compiler_params=_params(3),
        name="band_attention",
    )(q3, kp, vtp, _band_bias_t(rel_table))
    return out.reshape(bsz * seq, MIX_W)


def _proj_kernel(x_ref, a_ref, b_ref, wa_ref, wb_ref, o_ref):
    o_ref[...] = x_ref[...] + _dot(a_ref[...], wa_ref[...]) + _dot(b_ref[...], wb_ref[...])


def _proj_residual(x, a, b, w_out, tm=1024):
    t, d = x.shape
    w = w_out.astype(BF16)
    tok = lambda width: pl.BlockSpec((tm, width), lambda i: (i, 0))
    return pl.pallas_call(
        _proj_kernel,
        out_shape=jax.ShapeDtypeStruct((t, d), F32),
        grid=(t // tm,),
        in_specs=[tok(d), tok(MIX_W), tok(MIX_W), _const_spec((MIX_W, d)), _const_spec((MIX_W, d))],
        out_specs=tok(d),
        compiler_params=_params(1),
        name="proj_residual",
    )(x, a, b, w[:MIX_W], w[MIX_W:])


def _mem_kv_kernel(mem_ref, g_ref, w_ref, kg_ref, k_ref, v_ref):
    mem_n = (_rms(mem_ref[...]) * g_ref[...]).astype(BF16)
    kv = _dot(mem_n, w_ref[...])
    kg = kg_ref[...]
    for hd in range(X_HEADS):
        sl = slice(hd * X_HEAD_DIM, (hd + 1) * X_HEAD_DIM)
        k_ref[:, sl] = (_rms(kv[:, sl]) * kg).astype(BF16)
    v_ref[...] = kv[:, D_MODEL:].astype(BF16)


def _mem_kv(mem, gain, w_kv, k_gain):
    bsz, n_mem, d = mem.shape
    blk = pl.BlockSpec((None, n_mem, d), lambda b: (b, 0, 0))
    out = jax.ShapeDtypeStruct((bsz, n_mem, d), BF16)
    return pl.pallas_call(
        _mem_kv_kernel,
        out_shape=[out, out],
        grid=(bsz,),
        in_specs=[blk, _const_spec((1, d)), _const_spec((d, 2 * d)), _const_spec((1, X_HEAD_DIM))],
        out_specs=[blk, blk],
        compiler_params=_params(1),
        name="mem_kv",
    )(mem, gain.reshape(1, d), w_kv.astype(BF16), k_gain.reshape(1, X_HEAD_DIM))


def _cross_kernel(x_ref, g_ref, wq_ref, qg_ref, k_ref, v_ref, wo_ref, o_ref):
    x = x_ref[...]
    h = (_rms(x) * g_ref[...]).astype(BF16)
    q = _dot(h, wq_ref[...])
    qg = qg_ref[...]
    heads = []
    for hd in range(X_HEADS):
        sl = slice(hd * X_HEAD_DIM, (hd + 1) * X_HEAD_DIM)
        qh = (_rms(q[:, sl]) * qg * X_HEAD_DIM ** -0.5).astype(BF16)
        s = _dot_nt(qh, k_ref[:, sl])
        p = jnp.exp(s - jnp.max(s, axis=-1, keepdims=True))
        l = jnp.sum(p, axis=-1, keepdims=True)
        heads.append((_dot(p.astype(BF16), v_ref[:, sl]) / l).astype(BF16))
    o_ref[...] = x + _dot(jnp.concatenate(heads, axis=1), wo_ref[...])


def _cross_attention(x, gain, w_q, q_gain, k, v, w_o, bsz, seq):
    t, d = x.shape
    tq = TOKEN_TILE
    n_s = seq // tq
    n_mem = k.shape[1]
    tok = pl.BlockSpec((tq, d), lambda b, s: (b * n_s + s, 0))
    mem_spec = pl.BlockSpec((None, n_mem, d), lambda b, s: (b, 0, 0))
    return pl.pallas_call(
        _cross_kernel,
        out_shape=jax.ShapeDtypeStruct((t, d), F32),
        grid=(bsz, n_s),
        in_specs=[tok, _const_spec((1, d)), _const_spec((d, d)), _const_spec((1, X_HEAD_DIM)),
                  mem_spec, mem_spec, _const_spec((d, d))],
        out_specs=tok,
        compiler_params=_params(2),
        name="cross_attention",
    )(x, gain.reshape(1, d), w_q.astype(BF16), q_gain.reshape(1, X_HEAD_DIM), k, v,
      w_o.astype(BF16))


def kernel(x, mem, positions, ln_gains, ffn1_w_in, ffn1_w_out, ffn2_w_in, ffn2_w_out, even_w_in, even_f_bias, even_qk_gains, even_lambda, even_subln_gain, even_w_out, odd_w_in, odd_conv_w, odd_qk_gains, odd_rel_bias, odd_w_out, x_w_q, x_w_kv, x_qk_gains, x_w_o):
    bsz, seq, d = x.shape
    depth = ln_gains.shape[0]
    rope = _rope_tables(positions)
    x = x.reshape(bsz * seq, d)
    for layer in range(depth):
        g = ln_gains[layer]
        x = _ffn(x, g[0], ffn1_w_in[layer], ffn1_w_out[layer])
        if layer % 2 == 0:
            e = layer // 2
            lambda_init = 0.8 - 0.6 * math.exp(-0.3 * layer)
            aq, ak, avt, bq, bk, bvt = _even_prep(
                x, g[1], even_w_in[e], even_f_bias[e], even_qk_gains[e], rope, bsz, seq)
            left = _fox_attention(aq, ak, avt, bsz, seq)
            right = _diff_attention(bq, bk, bvt, even_lambda[e], even_subln_gain[e], lambda_init,
                                    bsz, seq)
            x = _proj_residual(x, left, right, even_w_out[e])
        else:
            o = layer // 2
            left, dq, dk, dvt = _odd_prep(x, g[1], odd_w_in[o], odd_conv_w[o], odd_qk_gains[o],
                                          bsz, seq)
            right = _band_attention(dq, dk, dvt, odd_rel_bias[o], bsz, seq)
            x = _proj_residual(x, left, right, odd_w_out[o])
        mk, mv = _mem_kv(mem, g[3], x_w_kv[layer], x_qk_gains[layer, 1])
        x = _cross_attention(x, g[2], x_w_q[layer], x_qk_gains[layer, 0], mk, mv, x_w_o[layer],
                             bsz, seq)
        x = _ffn(x, g[4], ffn2_w_in[layer], ffn2_w_out[layer])
    return x.reshape(bsz, seq, d)
```

```python
import functools
import math

import jax
import jax.numpy as jnp
from jax import lax
from jax.experimental import pallas as pl
from jax.experimental.pallas import tpu as pltpu

D_MODEL = 1024
CHUNK = 64
HEAD_DIM = 64
ROPE_DIM = HEAD_DIM // 4
ROPE_THETA = 500000.0
RMS_EPS = 1e-6
D_FF = 2816
A_HEADS = 8
MIX_W = 512
LANES = 128
D_LEFT = 8 * CHUNK
REL_CLIP = 128
X_HEADS = 4
X_HEAD_DIM = D_MODEL // X_HEADS
NEG = -1e30
LOG2E = math.log2(math.e)
Q_SCALE = HEAD_DIM ** -0.5 * LOG2E

VMEM_LIMIT = 56 * 1024 * 1024
FFN_CHUNKS = (768, 768, 768, 512)
TOKEN_TILE = 512
ATT_TILE = 256

BF16 = jnp.bfloat16
F32 = jnp.float32


def _params(n_axes):
    return pltpu.CompilerParams(dimension_semantics=("arbitrary",) * n_axes,
                                vmem_limit_bytes=VMEM_LIMIT)


def _const_spec(shape):
    nd = len(shape)
    return pl.BlockSpec(shape, lambda *_: (0,) * nd, pipeline_mode=pl.Buffered(1))


def _rms(x):
    return x * lax.rsqrt(jnp.mean(x * x, axis=-1, keepdims=True) + RMS_EPS)


def _dot(a, b):
    return jnp.dot(a, b, preferred_element_type=F32)


def _dot_nt(a, b):
    return lax.dot_general(a, b, (((1,), (1,)), ((), ())), preferred_element_type=F32)


def _ffn_kernel(x_ref, g_ref, wg_ref, wu_ref, wo_ref, o_ref):
    x = x_ref[...]
    xn = (_rms(x) * g_ref[...]).astype(BF16)
    y = jnp.zeros_like(x)
    start = 0
    for width in FFN_CHUNKS:
        gate = _dot(xn, wg_ref[:, start:start + width])
        up = _dot(xn, wu_ref[:, start:start + width])
        act = (gate * (1.0 / (1.0 + jnp.exp(-gate))) * up).astype(BF16)
        y = y + _dot(act, wo_ref[start:start + width, :])
        start += width
    o_ref[...] = x + 0.5 * y


def _ffn(x, gain, w_in, w_out):
    t, d = x.shape
    tm = TOKEN_TILE
    wg = w_in[:, :D_FF].astype(BF16)
    wu = w_in[:, D_FF:].astype(BF16)
    wo = w_out.astype(BF16)
    return pl.pallas_call(
        _ffn_kernel,
        out_shape=jax.ShapeDtypeStruct((t, d), F32),
        grid=(t // tm,),
        in_specs=[pl.BlockSpec((tm, d), lambda i: (i, 0)),
                  _const_spec((1, d)),
                  _const_spec((d, D_FF)), _const_spec((d, D_FF)), _const_spec((D_FF, d))],
        out_specs=pl.BlockSpec((tm, d), lambda i: (i, 0)),
        compiler_params=_params(1),
        name="ffn",
    )(x, gain.reshape(1, d), wg, wu, wo)


def _head_norm(y, ones_bd, gain):
    msq = _dot((y * y).astype(BF16), ones_bd) * (1.0 / HEAD_DIM)
    return y * lax.rsqrt(msq + RMS_EPS) * gain


def _block_diag_ones():
    r = jnp.arange(MIX_W) // HEAD_DIM
    return (r[:, None] == r[None, :]).astype(BF16)


def _tile_heads(v):
    return jnp.tile(v, (1, MIX_W // HEAD_DIM))


def _store_transposed(vt_ref, v):
    for i in range(v.shape[0] // ATT_TILE):
        vt_ref[i] = v[i * ATT_TILE:(i + 1) * ATT_TILE, :].T.astype(BF16)


def _split3(f):
    hi = f.astype(BF16).astype(F32)
    rest = f - hi
    mid = rest.astype(BF16).astype(F32)
    return hi, mid, rest - mid


def _even_prep_kernel(x_ref, g_ref, wa_ref, wf_ref, fb_ref, wb_ref, gains_ref, bd_ref,
                      ra_ref, rm_ref, rp_ref,
                      aq_ref, ak_ref, avt_ref, bq_ref, bk_ref, bvt_ref, carry_ref):
    tm = x_ref.shape[0]

    @pl.when(pl.program_id(1) == 0)
    def _():
        carry_ref[...] = jnp.zeros_like(carry_ref)

    h = (_rms(x_ref[...]) * g_ref[...]).astype(BF16)
    bd = bd_ref[...]
    gains = gains_ref[...]

    ya = _dot(h, wa_ref[...])
    qn = _head_norm(ya[:, :MIX_W], bd, gains[0:1]) * Q_SCALE
    kn = _head_norm(ya[:, MIX_W:2 * MIX_W], bd, gains[1:2])
    _store_transposed(avt_ref, ya[:, 2 * MIX_W:])

    z = _dot(h, wf_ref[...]) + fb_ref[...]
    logf = jnp.minimum(z, 0.0) - jnp.log1p(jnp.exp(-jnp.abs(z)))
    row = lax.broadcasted_iota(jnp.int32, logf.shape, 0)
    step = 1
    while step < tm:
        logf = logf + jnp.where(row >= step, pltpu.roll(logf, step, axis=0), 0.0)
        step *= 2
    cum = logf + carry_ref[...]
    carry_ref[...] = cum[tm - 1:tm, :]

    hi, mid, lo = _split3(cum * LOG2E)
    lane = lax.broadcasted_iota(jnp.int32, (tm, LANES), 1)
    is_hi = (lane == HEAD_DIM) | (lane == HEAD_DIM + 3)
    is_mid = (lane == HEAD_DIM + 1) | (lane == HEAD_DIM + 4)
    ones_hi = jnp.where(lane < HEAD_DIM + 6, 1.0, 0.0)
    for hd in range(A_HEADS):
        pieces = jnp.where(is_hi, hi[:, hd:hd + 1],
                           jnp.where(is_mid, mid[:, hd:hd + 1], lo[:, hd:hd + 1]))
        blk = slice((hd // 2) * LANES, (hd // 2 + 1) * LANES)
        sq, sk = qn[:, blk], kn[:, blk]
        if hd % 2:
            sq, sk = pltpu.roll(sq, HEAD_DIM, axis=1), pltpu.roll(sk, HEAD_DIM, axis=1)
        qa = jnp.where(lane < HEAD_DIM, sq, jnp.where(lane < HEAD_DIM + 3, pieces, ones_hi))
        ka = jnp.where(lane < HEAD_DIM, sk, jnp.where(lane < HEAD_DIM + 3, 1.0,
                                                      jnp.where(lane < HEAD_DIM + 6, -pieces, 0.0)))
        aq_ref[:, hd * LANES:(hd + 1) * LANES] = qa.astype(BF16)
        ak_ref[:, hd * LANES:(hd + 1) * LANES] = ka.astype(BF16)

    yb = _dot(h, wb_ref[...])
    rep = MIX_W // LANES
    ra = jnp.tile(ra_ref[...], (1, rep))
    rm = jnp.tile(rm_ref[...], (1, rep))
    rp = jnp.tile(rp_ref[...], (1, rep))

    def rope(v):
        half = ROPE_DIM // 2
        return v * ra + pltpu.roll(v, MIX_W - half, axis=1) * rm + pltpu.roll(v, half, axis=1) * rp

    bq = rope(_head_norm(yb[:, :MIX_W], bd, gains[2:3]))
    bk = rope(_head_norm(yb[:, MIX_W:2 * MIX_W], bd, gains[3:4]))
    bq_ref[...] = (bq * Q_SCALE).astype(BF16)
    bk_ref[...] = bk.astype(BF16)
    _store_transposed(bvt_ref, yb[:, 2 * MIX_W:])


def _rope_tables(positions):
    half = ROPE_DIM // 2
    inv = ROPE_THETA ** (-jnp.arange(0, ROPE_DIM, 2, dtype=F32) / ROPE_DIM)
    ang = positions.astype(F32).reshape(-1, 1) * inv
    cos, sin = jnp.cos(ang), jnp.sin(ang)
    t = cos.shape[0]
    pad = jnp.zeros((t, HEAD_DIM - ROPE_DIM), F32)
    zero = jnp.zeros((t, half), F32)
    ra = jnp.concatenate([cos, cos, pad + 1.0], axis=1)
    rm = jnp.concatenate([-sin, zero, pad], axis=1)
    rp = jnp.concatenate([zero, sin, pad], axis=1)
    dup = lambda a: jnp.concatenate([a, a], axis=1)
    return dup(ra), dup(rm), dup(rp)


def _even_prep(x, gain, w_in, f_bias, qk_gains, rope, bsz, seq):
    t, d = x.shape
    tm = TOKEN_TILE
    a_w = 3 * MIX_W
    wa = w_in[:, :a_w].astype(BF16)
    wf = jnp.pad(w_in[:, a_w:a_w + A_HEADS], ((0, 0), (0, LANES - A_HEADS))).astype(BF16)
    wb = w_in[:, a_w + A_HEADS:].astype(BF16)
    fb = jnp.pad(f_bias, (0, LANES - A_HEADS)).reshape(1, LANES)
    n_s = seq // tm
    tok = lambda w: pl.BlockSpec((tm, w), lambda b, s: (b * n_s + s, 0))
    vt_spec = pl.BlockSpec((None, tm // ATT_TILE, MIX_W, ATT_TILE), lambda b, s: (b, s, 0, 0))
    stream = lambda w: jax.ShapeDtypeStruct((t, w), BF16)
    vt_shape = jax.ShapeDtypeStruct((bsz, seq // ATT_TILE, MIX_W, ATT_TILE), BF16)
    return pl.pallas_call(
        _even_prep_kernel,
        out_shape=[stream(A_HEADS * LANES), stream(A_HEADS * LANES), vt_shape,
                   stream(MIX_W), stream(MIX_W), vt_shape],
        grid=(bsz, n_s),
        in_specs=[tok(d), _const_spec((1, d)), _const_spec((d, a_w)), _const_spec((d, LANES)),
                  _const_spec((1, LANES)), _const_spec((d, a_w)), _const_spec((4, MIX_W)),
                  _const_spec((MIX_W, MIX_W)), tok(LANES), tok(LANES), tok(LANES)],
        out_specs=[tok(A_HEADS * LANES), tok(A_HEADS * LANES), vt_spec,
                   tok(MIX_W), tok(MIX_W), vt_spec],
        scratch_shapes=[pltpu.VMEM((1, LANES), F32)],
        compiler_params=_params(2),
        name="even_prep",
    )(x, gain.reshape(1, d), wa, wf, fb, wb, _tile_heads(qk_gains), _block_diag_ones(), *rope)


def _lane_lo(shape):
    return lax.broadcasted_iota(jnp.int32, shape, len(shape) - 1) < HEAD_DIM


def _key_rows(g):
    return pl.ds(pl.multiple_of(g * ATT_TILE, ATT_TILE), ATT_TILE)


SUM_ROWS = 16


def _with_sum_rows(value_t):
    return jnp.concatenate([value_t, jnp.ones((SUM_ROWS, value_t.shape[1]), BF16)], axis=0)


def _normalise(acc, v_rows):
    return acc[:v_rows] / acc[v_rows:v_rows + 1]


def _paired_causal_attention(a, n_tiles, n_sub, q_pair, key_block, value_t, diag_visible, s_refs,
                             v_rows):
    tq = ATT_TILE
    half = n_tiles // 2
    neg_row = jnp.full((1, tq), NEG, F32)

    def slot(n):
        if n == 0:
            return "A", None, a, True
        if n == n_tiles:
            return "B", None, n_tiles - 1 - a, True
        if n >= half:
            return "B", None, n - a - 1, False
        is_a = n <= a
        return None, is_a, jnp.where(is_a, n - 1, n - a - 1), False

    def pick(tile, is_a, on_a, on_b):
        if tile is not None:
            return on_a if tile == "A" else on_b
        return jnp.where(is_a, on_a, on_b)

    slots = [slot(n) for n in range(n_tiles + 1)]
    queries = [q_pair(sub) for sub in range(n_sub)]
    maxes = {(t, sub): neg_row for t in "AB" for sub in range(n_sub)}

    def pass1(sub, n):
        tile, is_a, g, diagonal = slots[n]
        s = _dot_nt(key_block(g, sub), pick(tile, is_a, *queries[sub]))
        if diagonal:
            s = jnp.where(diag_visible, s, NEG)
        s_refs[sub][n] = s
        mx = jnp.max(s, axis=0, keepdims=True)
        if tile is None:
            maxes["A", sub] = jnp.maximum(maxes["A", sub], jnp.where(is_a, mx, neg_row))
            maxes["B", sub] = jnp.maximum(maxes["B", sub], jnp.where(is_a, neg_row, mx))
        else:
            maxes[tile, sub] = jnp.maximum(maxes[tile, sub], mx)

    zero_acc = jnp.zeros((v_rows + SUM_ROWS, tq), F32)
    accs = {(t, sub): zero_acc for t in "AB" for sub in range(n_sub)}

    def probs(sub, n):
        tile, is_a, _, _ = slots[n]
        p = jnp.exp2(s_refs[sub][n] - pick(tile, is_a, maxes["A", sub], maxes["B", sub]))
        return p.astype(BF16)

    def accumulate(sub, n, p):
        tile, is_a, g, _ = slots[n]
        acc_n = _dot(_with_sum_rows(value_t(g, sub)), p)
        if tile is None:
            accs["A", sub] = accs["A", sub] + jnp.where(is_a, acc_n, 0.0)
            accs["B", sub] = accs["B", sub] + jnp.where(is_a, 0.0, acc_n)
        else:
            accs[tile, sub] = accs[tile, sub] + acc_n

    for n in range(len(slots)):
        for sub in range(n_sub):
            pass1(sub, n)
    for n in range(len(slots)):
        for sub in range(n_sub):
            accumulate(sub, n, probs(sub, n))
    return {t: [accs[t, sub] for sub in range(n_sub)] for t in "AB"}


def _att_scratch(n_tiles):
    return [pltpu.VMEM((n_tiles + 1, ATT_TILE, ATT_TILE), F32)] * 2


def _paired_specs(n_tiles, width, col):
    tq = ATT_TILE
    half = n_tiles // 2
    q_a = pl.BlockSpec((None, tq, width), lambda b, p, a: (b, a, col(p)))
    q_b = pl.BlockSpec((None, tq, width), lambda b, p, a: (b, n_tiles - 1 - a, col(p)))
    o_a = pl.BlockSpec((None, tq, LANES), lambda b, p, a: (b, a, p))
    o_b = pl.BlockSpec((None, tq, LANES), lambda b, p, a: (b, half - 1 - a, p))
    return q_a, q_b, o_a, o_b


def _fox_kernel(qa_ref, qb_ref, k_ref, vt_ref, oa_ref, ob_ref, s0_ref, s1_ref, *, n_tiles):
    tq = ATT_TILE

    def q_pair(sub):
        cols = slice(sub * LANES, (sub + 1) * LANES)
        return qa_ref[:, cols], qb_ref[:, cols]

    def key_block(g, sub):
        return k_ref[_key_rows(g), sub * LANES:(sub + 1) * LANES]

    def value_t(g, sub):
        return vt_ref[g, sub * HEAD_DIM:(sub + 1) * HEAD_DIM, :]

    causal = (lax.broadcasted_iota(jnp.int32, (tq, tq), 0)
              <= lax.broadcasted_iota(jnp.int32, (tq, tq), 1))
    res = _paired_causal_attention(pl.program_id(2), n_tiles, 2, q_pair, key_block, value_t,
                                   causal, (s0_ref, s1_ref), HEAD_DIM)
    for tile, o_ref in (("A", oa_ref), ("B", ob_ref)):
        o_t = jnp.concatenate([_normalise(acc, HEAD_DIM) for acc in res[tile]], axis=0)
        o_ref[...] = o_t.T.astype(o_ref.dtype)


def _fox_attention(q, k, vt, bsz, seq):
    tq = ATT_TILE
    n_tiles = seq // tq
    q3 = q.reshape(bsz, seq, A_HEADS * LANES)
    k3 = k.reshape(bsz, seq, A_HEADS * LANES)
    q_a, q_b, o_a, o_b = _paired_specs(n_tiles, 2 * LANES, lambda p: p)
    half_out = jax.ShapeDtypeStruct((bsz, seq // 2, MIX_W), BF16)
    out_a, out_b = pl.pallas_call(
        functools.partial(_fox_kernel, n_tiles=n_tiles),
        out_shape=[half_out, half_out],
        grid=(bsz, A_HEADS // 2, n_tiles // 2),
        in_specs=[q_a, q_b,
                  pl.BlockSpec((None, seq, 2 * LANES), lambda b, p, a: (b, 0, p)),
                  pl.BlockSpec((None, n_tiles, LANES, tq), lambda b, p, a: (b, 0, p, 0))],
        out_specs=[o_a, o_b],
        scratch_shapes=_att_scratch(n_tiles),
        compiler_params=_params(3),
        name="fox_attention",
    )(q3, q3, k3, vt)
    return jnp.concatenate([out_a, out_b], axis=1).reshape(bsz * seq, MIX_W)


def _diff_kernel(qa_ref, qb_ref, k_ref, vt_ref, lam_ref, sg_ref, oa_ref, ob_ref, s0_ref, s1_ref, *,
                 lambda_init, n_tiles):
    tq = ATT_TILE
    lo = _lane_lo((tq, LANES))
    lp = lam_ref[...]
    lam = (jnp.exp(jnp.sum(lp[0:1] * lp[1:2], axis=1, keepdims=True))
           - jnp.exp(jnp.sum(lp[2:3] * lp[3:4], axis=1, keepdims=True)) + lambda_init)

    def half_of(q, sub):
        zero = jnp.zeros_like(q)
        return jnp.where(lo, q, zero) if sub == 0 else jnp.where(lo, zero, q)

    def q_pair(sub):
        return half_of(qa_ref[...], sub), half_of(qb_ref[...], sub)

    def key_block(g, sub):
        return k_ref[_key_rows(g), :]

    def value_t(g, sub):
        return vt_ref[g]

    chunk_causal = (lax.broadcasted_iota(jnp.int32, (tq, tq), 0) // CHUNK
                    <= lax.broadcasted_iota(jnp.int32, (tq, tq), 1) // CHUNK)
    res = _paired_causal_attention(pl.program_id(2), n_tiles, 2, q_pair, key_block, value_t,
                                   chunk_causal, (s0_ref, s1_ref), LANES)
    for tile, o_ref in (("A", oa_ref), ("B", ob_ref)):
        o1, o2 = (_normalise(acc, LANES) for acc in res[tile])
        o = (o1 - lam * o2).T
        o_ref[...] = (_rms(o) * sg_ref[...] * (1.0 - lambda_init)).astype(o_ref.dtype)


def _diff_attention(q, k, vt, lam_params, subln_gain, lambda_init, bsz, seq):
    tq = ATT_TILE
    n_tiles = seq // tq
    q3, k3 = (a.reshape(bsz, seq, MIX_W) for a in (q, k))
    q_a, q_b, o_a, o_b = _paired_specs(n_tiles, LANES, lambda h: h)
    half_out = jax.ShapeDtypeStruct((bsz, seq // 2, MIX_W), BF16)
    out_a, out_b = pl.pallas_call(
        functools.partial(_diff_kernel, lambda_init=lambda_init, n_tiles=n_tiles),
        out_shape=[half_out, half_out],
        grid=(bsz, MIX_W // LANES, n_tiles // 2),
        in_specs=[q_a, q_b,
                  pl.BlockSpec((None, seq, LANES), lambda b, h, a: (b, 0, h)),
                  pl.BlockSpec((None, n_tiles, LANES, tq), lambda b, h, a: (b, 0, h, 0)),
                  _const_spec((4, HEAD_DIM)), _const_spec((1, LANES))],
        out_specs=[o_a, o_b],
        scratch_shapes=_att_scratch(n_tiles),
        compiler_params=_params(3),
        name="diff_attention",
    )(q3, q3, k3, vt, lam_params, subln_gain.reshape(1, LANES))
    return jnp.concatenate([out_a, out_b], axis=1).reshape(bsz * seq, MIX_W)


def _odd_prep_kernel(x_ref, g_ref, wc_ref, wd_ref, cw_ref, gains_ref, bd_ref,
                     c_ref, dq_ref, dk_ref, dvt_ref, carry_ref):
    tm = x_ref.shape[0]

    @pl.when(pl.program_id(1) == 0)
    def _():
        carry_ref[...] = jnp.zeros_like(carry_ref)

    h = (_rms(x_ref[...]) * g_ref[...]).astype(BF16)
    yc = _dot(h, wc_ref[...])
    u = yc[:, MIX_W:2 * MIX_W] * yc[:, 2 * MIX_W:]
    prev = carry_ref[...]
    carry_ref[...] = u[tm - 8:, :]
    row = lax.broadcasted_iota(jnp.int32, u.shape, 0)
    u1 = jnp.where(row == 0, prev[7:8], pltpu.roll(u, 1, axis=0))
    u2 = jnp.where(row == 0, prev[6:7], jnp.where(row == 1, prev[7:8], pltpu.roll(u, 2, axis=0)))
    cw = cw_ref[...]
    conv = cw[0:1] * u2 + cw[1:2] * u1 + cw[2:3] * u
    c_ref[...] = (yc[:, :MIX_W] * conv).astype(BF16)

    yd = _dot(h, wd_ref[...])
    bd = bd_ref[...]
    gains = gains_ref[...]
    dq_ref[...] = (_head_norm(yd[:, :MIX_W], bd, gains[0:1]) * Q_SCALE).astype(BF16)
    dk_ref[...] = _head_norm(yd[:, MIX_W:2 * MIX_W], bd, gains[1:2]).astype(BF16)
    _store_transposed(dvt_ref, yd[:, 2 * MIX_W:])


def _odd_prep(x, gain, w_in, conv_w, qk_gains, bsz, seq):
    t, d = x.shape
    tm = TOKEN_TILE
    w3 = 3 * MIX_W
    n_s = seq // tm
    tok = lambda w: pl.BlockSpec((tm, w), lambda b, s: (b * n_s + s, 0))
    stream = jax.ShapeDtypeStruct((t, MIX_W), BF16)
    return pl.pallas_call(
        _odd_prep_kernel,
        out_shape=[stream] * 3 + [jax.ShapeDtypeStruct((bsz, seq // ATT_TILE, MIX_W, ATT_TILE), BF16)],
        grid=(bsz, n_s),
        in_specs=[tok(d), _const_spec((1, d)), _const_spec((d, w3)), _const_spec((d, w3)),
                  _const_spec((3, MIX_W)), _const_spec((2, MIX_W)), _const_spec((MIX_W, MIX_W))],
        out_specs=[tok(MIX_W)] * 3 + [pl.BlockSpec((None, tm // ATT_TILE, MIX_W, ATT_TILE),
                                                   lambda b, s: (b, s, 0, 0))],
        scratch_shapes=[pltpu.VMEM((8, MIX_W), F32)],
        compiler_params=_params(2),
        name="odd_prep",
    )(x, gain.reshape(1, d), w_in[:, :w3].astype(BF16), w_in[:, w3:].astype(BF16), conv_w,
      _tile_heads(qk_gains), _block_diag_ones())


BAND_GROUPS = 1 + D_LEFT // ATT_TILE


BAND_TILES = 2
BAND_WIDTH = BAND_GROUPS * ATT_TILE
BIAS_ROW = BAND_WIDTH + ATT_TILE


def _band_kernel(q_ref, k_ref, vt_ref, w_ref, o_ref, bias_ref, s_ref):
    tq = ATT_TILE
    j = pl.program_id(2)

    @pl.when((pl.program_id(1) == 0) & (j == 0))
    def _():
        key = lax.broadcasted_iota(jnp.int32, (BAND_WIDTH, tq), 0)
        chunk_start = lax.broadcasted_iota(jnp.int32, (BAND_WIDTH, tq), 1) // CHUNK * CHUNK
        in_band = (key >= chunk_start) & (key < chunk_start + D_LEFT + CHUNK)
        for sub in range(2):
            rows = jnp.broadcast_to(w_ref[sub], (BAND_WIDTH, BIAS_ROW))
            skew = pltpu.roll(rows, tq + 1, axis=1, stride=1, stride_axis=0)
            table = jnp.where(in_band, skew[:, :tq] * LOG2E, NEG)
            for grp in range(BAND_GROUPS):
                bias_ref[sub, grp] = table[grp * tq:(grp + 1) * tq, :]

    lo = _lane_lo((tq, LANES))
    pad_penalty = jnp.where(j == 0, NEG, 0.0)
    maxes, slots = {}, []
    for tile in range(BAND_TILES):
        q = q_ref[tile * tq:(tile + 1) * tq, :]
        zero = jnp.zeros_like(q)
        q_subs = (jnp.where(lo, q, zero), jnp.where(lo, zero, q))
        for grp in range(BAND_GROUPS):
            g = BAND_TILES * j + tile + grp
            k_blk = k_ref[_key_rows(g), :]
            for sub in range(2):
                s = _dot_nt(k_blk, q_subs[sub]) + bias_ref[sub, grp]
                if tile + grp < BAND_GROUPS - 1:
                    s = s + pad_penalty
                s_ref[tile, sub, grp] = s
                mx = jnp.max(s, axis=0, keepdims=True)
                maxes[tile, sub] = mx if grp == 0 else jnp.maximum(maxes[tile, sub], mx)
                slots.append((tile, sub, grp, g))

    accs = {}
    for tile, sub, grp, g in slots:
        p = jnp.exp2(s_ref[tile, sub, grp] - maxes[tile, sub])
        value_t = _with_sum_rows(vt_ref[g, sub * HEAD_DIM:(sub + 1) * HEAD_DIM, :])
        acc_n = _dot(value_t, p.astype(BF16))
        accs[tile, sub] = acc_n if grp == 0 else accs[tile, sub] + acc_n
    for tile in range(BAND_TILES):
        o_t = jnp.concatenate([_normalise(accs[tile, sub], HEAD_DIM) for sub in range(2)], axis=0)
        o_ref[tile * tq:(tile + 1) * tq, :] = o_t.T.astype(o_ref.dtype)


def _band_bias_rows(rel_table):
    n_lo = ATT_TILE - 1 - REL_CLIP
    n_hi = BIAS_ROW - n_lo - (2 * REL_CLIP + 1)
    w = jnp.concatenate([jnp.repeat(rel_table[:, :1], n_lo, axis=1), rel_table,
                         jnp.repeat(rel_table[:, -1:], n_hi, axis=1)], axis=1)
    return w.astype(F32).reshape(rel_table.shape[0], 1, BIAS_ROW)


def _band_attention(q, k, vt, rel_table, bsz, seq):
    tq = ATT_TILE
    n_q = seq // tq
    q3 = q.reshape(bsz, seq, MIX_W)
    kp = jnp.pad(k.reshape(bsz, seq, MIX_W), ((0, 0), (D_LEFT, 0), (0, 0)))
    vtp = jnp.pad(vt, ((0, 0), (BAND_GROUPS - 1, 0), (0, 0), (0, 0)))
    q_spec = pl.BlockSpec((None, BAND_TILES * tq, LANES), lambda p, b, j: (b, j, p))
    out = pl.pallas_call(
        _band_kernel,
        out_shape=jax.ShapeDtypeStruct((bsz, seq, MIX_W), BF16),
        grid=(MIX_W // LANES, bsz, n_q // BAND_TILES),
        in_specs=[q_spec,
                  pl.BlockSpec((None, seq + D_LEFT, LANES), lambda p, b, j: (b, 0, p)),
                  pl.BlockSpec((None, n_q + BAND_GROUPS - 1, LANES, tq), lambda p, b, j: (b, 0, p, 0)),
                  pl.BlockSpec((2, 1, BIAS_ROW), lambda p, b, j: (p, 0, 0))],
        out_specs=q_spec,
        scratch_shapes=[pltpu.VMEM((2, BAND_GROUPS, tq, tq), F32),
                        pltpu.VMEM((BAND_TILES, 2, BAND_GROUPS, tq, tq), F32)],
        compiler_params=_params(3),
        name="band_attention",
    )(q3, kp, vtp, _band_bias_rows(rel_table))
    return out.reshape(bsz * seq, MIX_W)


def _proj_kernel(x_ref, a_ref, b_ref, wa_ref, wb_ref, o_ref):
    o_ref[...] = x_ref[...] + _dot(a_ref[...], wa_ref[...]) + _dot(b_ref[...], wb_ref[...])


def _proj_residual(x, a, b, w_out, tm=1024):
    t, d = x.shape
    w = w_out.astype(BF16)
    tok = lambda width: pl.BlockSpec((tm, width), lambda i: (i, 0))
    return pl.pallas_call(
        _proj_kernel,
        out_shape=jax.ShapeDtypeStruct((t, d), F32),
        grid=(t // tm,),
        in_specs=[tok(d), tok(MIX_W), tok(MIX_W), _const_spec((MIX_W, d)), _const_spec((MIX_W, d))],
        out_specs=tok(d),
        compiler_params=_params(1),
        name="proj_residual",
    )(x, a, b, w[:MIX_W], w[MIX_W:])


def _mem_kv_kernel(mem_ref, g_ref, w_ref, kg_ref, k_ref, v_ref):
    mem_n = (_rms(mem_ref[...]) * g_ref[...]).astype(BF16)
    kv = _dot(mem_n, w_ref[...])
    kg = kg_ref[...]
    for hd in range(X_HEADS):
        sl = slice(hd * X_HEAD_DIM, (hd + 1) * X_HEAD_DIM)
        k_ref[:, sl] = (_rms(kv[:, sl]) * kg).astype(BF16)
    v_ref[...] = kv[:, D_MODEL:].astype(BF16)


def _mem_kv(mem, gain, w_kv, k_gain):
    bsz, n_mem, d = mem.shape
    blk = pl.BlockSpec((None, n_mem, d), lambda b: (b, 0, 0))
    out = jax.ShapeDtypeStruct((bsz, n_mem, d), BF16)
    return pl.pallas_call(
        _mem_kv_kernel,
        out_shape=[out, out],
        grid=(bsz,),
        in_specs=[blk, _const_spec((1, d)), _const_spec((d, 2 * d)), _const_spec((1, X_HEAD_DIM))],
        out_specs=[blk, blk],
        compiler_params=_params(1),
        name="mem_kv",
    )(mem, gain.reshape(1, d), w_kv.astype(BF16), k_gain.reshape(1, X_HEAD_DIM))


def _cross_kernel(x_ref, g_ref, wq_ref, qg_ref, k_ref, v_ref, wo_ref, o_ref):
    x = x_ref[...]
    h = (_rms(x) * g_ref[...]).astype(BF16)
    q = _dot(h, wq_ref[...])
    qg = qg_ref[...]
    heads = []
    for hd in range(X_HEADS):
        sl = slice(hd * X_HEAD_DIM, (hd + 1) * X_HEAD_DIM)
        qh = (_rms(q[:, sl]) * qg * X_HEAD_DIM ** -0.5).astype(BF16)
        s = _dot_nt(qh, k_ref[:, sl])
        p = jnp.exp(s - jnp.max(s, axis=-1, keepdims=True))
        l = jnp.sum(p, axis=-1, keepdims=True)
        heads.append((_dot(p.astype(BF16), v_ref[:, sl]) / l).astype(BF16))
    o_ref[...] = x + _dot(jnp.concatenate(heads, axis=1), wo_ref[...])


def _cross_attention(x, gain, w_q, q_gain, k, v, w_o, bsz, seq):
    t, d = x.shape
    tq = TOKEN_TILE
    n_s = seq // tq
    n_mem = k.shape[1]
    tok = pl.BlockSpec((tq, d), lambda b, s: (b * n_s + s, 0))
    mem_spec = pl.BlockSpec((None, n_mem, d), lambda b, s: (b, 0, 0))
    return pl.pallas_call(
        _cross_kernel,
        out_shape=jax.ShapeDtypeStruct((t, d), F32),
        grid=(bsz, n_s),
        in_specs=[tok, _const_spec((1, d)), _const_spec((d, d)), _const_spec((1, X_HEAD_DIM)),
                  mem_spec, mem_spec, _const_spec((d, d))],
        out_specs=tok,
        compiler_params=_params(2),
        name="cross_attention",
    )(x, gain.reshape(1, d), w_q.astype(BF16), q_gain.reshape(1, X_HEAD_DIM), k, v,
      w_o.astype(BF16))


def kernel(x, mem, positions, ln_gains, ffn1_w_in, ffn1_w_out, ffn2_w_in, ffn2_w_out, even_w_in, even_f_bias, even_qk_gains, even_lambda, even_subln_gain, even_w_out, odd_w_in, odd_conv_w, odd_qk_gains, odd_rel_bias, odd_w_out, x_w_q, x_w_kv, x_qk_gains, x_w_o):
    bsz, seq, d = x.shape
    depth = ln_gains.shape[0]
    rope = _rope_tables(positions)
    x = x.reshape(bsz * seq, d)
    for layer in range(depth):
        g = ln_gains[layer]
        x = _ffn(x, g[0], ffn1_w_in[layer], ffn1_w_out[layer])
        if layer % 2 == 0:
            e = layer // 2
            lambda_init = 0.8 - 0.6 * math.exp(-0.3 * layer)
            aq, ak, avt, bq, bk, bvt = _even_prep(
                x, g[1], even_w_in[e], even_f_bias[e], even_qk_gains[e], rope, bsz, seq)
            left = _fox_attention(aq, ak, avt, bsz, seq)
            right = _diff_attention(bq, bk, bvt, even_lambda[e], even_subln_gain[e], lambda_init,
                                    bsz, seq)
            x = _proj_residual(x, left, right, even_w_out[e])
        else:
            o = layer // 2
            left, dq, dk, dvt = _odd_prep(x, g[1], odd_w_in[o], odd_conv_w[o], odd_qk_gains[o],
                                          bsz, seq)
            right = _band_attention(dq, dk, dvt, odd_rel_bias[o], bsz, seq)
            x = _proj_residual(x, left, right, odd_w_out[o])
        mk, mv = _mem_kv(mem, g[3], x_w_kv[layer], x_qk_gains[layer, 1])
        x = _cross_attention(x, g[2], x_w_q[layer], x_qk_gains[layer, 0], mk, mv, x_w_o[layer],
                             bsz, seq)
        x = _ffn(x, g[4], ffn2_w_in[layer], ffn2_w_out[layer])
    return x.reshape(bsz, seq, d)
```

```python
import functools
import math

import jax
import jax.numpy as jnp
from jax import lax
from jax.experimental import pallas as pl
from jax.experimental.pallas import tpu as pltpu

D_MODEL = 1024
CHUNK = 64
HEAD_DIM = 64
ROPE_DIM = HEAD_DIM // 4
ROPE_THETA = 500000.0
RMS_EPS = 1e-6
D_FF = 2816
A_HEADS = 8
MIX_W = 512
LANES = 128
D_LEFT = 8 * CHUNK
REL_CLIP = 128
X_HEADS = 4
X_HEAD_DIM = D_MODEL // X_HEADS
NEG = -1e30
LOG2E = math.log2(math.e)
Q_SCALE = HEAD_DIM ** -0.5 * LOG2E

VMEM_LIMIT = 56 * 1024 * 1024
FFN_CHUNKS = (768, 768, 768, 512)
TOKEN_TILE = 512
ATT_TILE = 256

BF16 = jnp.bfloat16
F32 = jnp.float32


def _params(n_axes):
    return pltpu.CompilerParams(dimension_semantics=("arbitrary",) * n_axes,
                                vmem_limit_bytes=VMEM_LIMIT)


def _const_spec(shape):
    nd = len(shape)
    return pl.BlockSpec(shape, lambda *_: (0,) * nd, pipeline_mode=pl.Buffered(1))


def _rms(x):
    return x * lax.rsqrt(jnp.mean(x * x, axis=-1, keepdims=True) + RMS_EPS)


def _dot(a, b):
    return jnp.dot(a, b, preferred_element_type=F32)


def _dot_nt(a, b):
    return lax.dot_general(a, b, (((1,), (1,)), ((), ())), preferred_element_type=F32)


def _ffn_kernel(x_ref, g_ref, wg_ref, wu_ref, wo_ref, o_ref):
    x = x_ref[...]
    xn = (_rms(x) * g_ref[...]).astype(BF16)
    y = jnp.zeros_like(x)
    start = 0
    for width in FFN_CHUNKS:
        gate = _dot(xn, wg_ref[:, start:start + width])
        up = _dot(xn, wu_ref[:, start:start + width])
        act = (gate * (1.0 / (1.0 + jnp.exp(-gate))) * up).astype(BF16)
        y = y + _dot(act, wo_ref[start:start + width, :])
        start += width
    o_ref[...] = x + 0.5 * y


def _ffn(x, gain, w_in, w_out):
    t, d = x.shape
    tm = TOKEN_TILE
    wg = w_in[:, :D_FF].astype(BF16)
    wu = w_in[:, D_FF:].astype(BF16)
    wo = w_out.astype(BF16)
    return pl.pallas_call(
        _ffn_kernel,
        out_shape=jax.ShapeDtypeStruct((t, d), F32),
        grid=(t // tm,),
        in_specs=[pl.BlockSpec((tm, d), lambda i: (i, 0)),
                  _const_spec((1, d)),
                  _const_spec((d, D_FF)), _const_spec((d, D_FF)), _const_spec((D_FF, d))],
        out_specs=pl.BlockSpec((tm, d), lambda i: (i, 0)),
        compiler_params=_params(1),
        name="ffn",
    )(x, gain.reshape(1, d), wg, wu, wo)


BD_WIDTH = 256


def _head_norm(y, ones_bd, gain):
    sq = (y * y).astype(BF16)
    msq = jnp.concatenate([_dot(sq[:, c:c + BD_WIDTH], ones_bd) for c in range(0, MIX_W, BD_WIDTH)],
                          axis=1) * (1.0 / HEAD_DIM)
    return y * lax.rsqrt(msq + RMS_EPS) * gain


def _block_diag_ones():
    r = jnp.arange(BD_WIDTH) // HEAD_DIM
    return (r[:, None] == r[None, :]).astype(BF16)


def _tile_heads(v):
    return jnp.tile(v, (1, MIX_W // HEAD_DIM))


def _store_transposed(vt_ref, v):
    for i in range(v.shape[0] // ATT_TILE):
        vt_ref[i] = v[i * ATT_TILE:(i + 1) * ATT_TILE, :].T.astype(BF16)


def _split3(f):
    hi = f.astype(BF16).astype(F32)
    rest = f - hi
    mid = rest.astype(BF16).astype(F32)
    return hi, mid, rest - mid


def _even_prep_kernel(x_ref, g_ref, wa_ref, wf_ref, fb_ref, wb_ref, gains_ref, bd_ref,
                      ra_ref, rm_ref, rp_ref,
                      aq_ref, ak_ref, avt_ref, bq_ref, bk_ref, bvt_ref, carry_ref):
    tm = x_ref.shape[0]

    @pl.when(pl.program_id(1) == 0)
    def _():
        carry_ref[...] = jnp.zeros_like(carry_ref)

    h = (_rms(x_ref[...]) * g_ref[...]).astype(BF16)
    bd = bd_ref[...]
    gains = gains_ref[...]

    ya = _dot(h, wa_ref[...])
    qn = _head_norm(ya[:, :MIX_W], bd, gains[0:1]) * Q_SCALE
    kn = _head_norm(ya[:, MIX_W:2 * MIX_W], bd, gains[1:2])
    _store_transposed(avt_ref, ya[:, 2 * MIX_W:])

    z = _dot(h, wf_ref[...]) + fb_ref[...]
    logf = jnp.minimum(z, 0.0) - jnp.log1p(jnp.exp(-jnp.abs(z)))
    row = lax.broadcasted_iota(jnp.int32, logf.shape, 0)
    step = 1
    while step < tm:
        logf = logf + jnp.where(row >= step, pltpu.roll(logf, step, axis=0), 0.0)
        step *= 2
    cum = logf + carry_ref[...]
    carry_ref[...] = cum[tm - 1:tm, :]

    hi, mid, lo = _split3(cum * LOG2E)
    lane = lax.broadcasted_iota(jnp.int32, (tm, LANES), 1)
    is_hi = (lane == HEAD_DIM) | (lane == HEAD_DIM + 3)
    is_mid = (lane == HEAD_DIM + 1) | (lane == HEAD_DIM + 4)
    ones_hi = jnp.where(lane < HEAD_DIM + 6, 1.0, 0.0)
    for hd in range(A_HEADS):
        pieces = jnp.where(is_hi, hi[:, hd:hd + 1],
                           jnp.where(is_mid, mid[:, hd:hd + 1], lo[:, hd:hd + 1]))
        blk = slice((hd // 2) * LANES, (hd // 2 + 1) * LANES)
        sq, sk = qn[:, blk], kn[:, blk]
        if hd % 2:
            sq, sk = pltpu.roll(sq, HEAD_DIM, axis=1), pltpu.roll(sk, HEAD_DIM, axis=1)
        qa = jnp.where(lane < HEAD_DIM, sq, jnp.where(lane < HEAD_DIM + 3, pieces, ones_hi))
        ka = jnp.where(lane < HEAD_DIM, sk, jnp.where(lane < HEAD_DIM + 3, 1.0,
                                                      jnp.where(lane < HEAD_DIM + 6, -pieces, 0.0)))
        aq_ref[:, hd * LANES:(hd + 1) * LANES] = qa.astype(BF16)
        ak_ref[:, hd * LANES:(hd + 1) * LANES] = ka.astype(BF16)

    yb = _dot(h, wb_ref[...])
    rep = MIX_W // LANES
    ra = jnp.tile(ra_ref[...], (1, rep))
    rm = jnp.tile(rm_ref[...], (1, rep))
    rp = jnp.tile(rp_ref[...], (1, rep))

    def rope(v):
        half = ROPE_DIM // 2
        return v * ra + pltpu.roll(v, MIX_W - half, axis=1) * rm + pltpu.roll(v, half, axis=1) * rp

    bq = rope(_head_norm(yb[:, :MIX_W], bd, gains[2:3]))
    bk = rope(_head_norm(yb[:, MIX_W:2 * MIX_W], bd, gains[3:4]))
    bq_ref[...] = (bq * Q_SCALE).astype(BF16)
    bk_ref[...] = bk.astype(BF16)
    _store_transposed(bvt_ref, yb[:, 2 * MIX_W:])


def _rope_tables(positions):
    half = ROPE_DIM // 2
    inv = ROPE_THETA ** (-jnp.arange(0, ROPE_DIM, 2, dtype=F32) / ROPE_DIM)
    ang = positions.astype(F32).reshape(-1, 1) * inv
    cos = jnp.tile(jnp.cos(ang), (1, LANES // half))
    sin = jnp.tile(jnp.sin(ang), (1, LANES // half))
    in_head = jnp.arange(LANES) % HEAD_DIM
    ra = jnp.where(in_head < ROPE_DIM, cos, 1.0)
    rm = jnp.where(in_head < half, -sin, 0.0)
    rp = jnp.where((in_head >= half) & (in_head < ROPE_DIM), sin, 0.0)
    return ra, rm, rp


def _even_prep(x, gain, w_in, f_bias, qk_gains, rope, bsz, seq):
    t, d = x.shape
    tm = TOKEN_TILE
    a_w = 3 * MIX_W
    wa = w_in[:, :a_w].astype(BF16)
    wf = jnp.pad(w_in[:, a_w:a_w + A_HEADS], ((0, 0), (0, LANES - A_HEADS))).astype(BF16)
    wb = w_in[:, a_w + A_HEADS:].astype(BF16)
    fb = jnp.pad(f_bias, (0, LANES - A_HEADS)).reshape(1, LANES)
    n_s = seq // tm
    tok = lambda w: pl.BlockSpec((tm, w), lambda b, s: (b * n_s + s, 0))
    vt_spec = pl.BlockSpec((None, tm // ATT_TILE, MIX_W, ATT_TILE), lambda b, s: (b, s, 0, 0))
    stream = lambda w: jax.ShapeDtypeStruct((t, w), BF16)
    vt_shape = jax.ShapeDtypeStruct((bsz, seq // ATT_TILE, MIX_W, ATT_TILE), BF16)
    return pl.pallas_call(
        _even_prep_kernel,
        out_shape=[stream(A_HEADS * LANES), stream(A_HEADS * LANES), vt_shape,
                   stream(MIX_W), stream(MIX_W), vt_shape],
        grid=(bsz, n_s),
        in_specs=[tok(d), _const_spec((1, d)), _const_spec((d, a_w)), _const_spec((d, LANES)),
                  _const_spec((1, LANES)), _const_spec((d, a_w)), _const_spec((4, MIX_W)),
                  _const_spec((BD_WIDTH, BD_WIDTH)), tok(LANES), tok(LANES), tok(LANES)],
        out_specs=[tok(A_HEADS * LANES), tok(A_HEADS * LANES), vt_spec,
                   tok(MIX_W), tok(MIX_W), vt_spec],
        scratch_shapes=[pltpu.VMEM((1, LANES), F32)],
        compiler_params=_params(2),
        name="even_prep",
    )(x, gain.reshape(1, d), wa, wf, fb, wb, _tile_heads(qk_gains), _block_diag_ones(), *rope)


def _lane_lo(shape):
    return lax.broadcasted_iota(jnp.int32, shape, len(shape) - 1) < HEAD_DIM


def _key_rows(g):
    return pl.ds(pl.multiple_of(g * ATT_TILE, ATT_TILE), ATT_TILE)


SUM_ROWS = 16
MXU_LOOKAHEAD = 8


def _with_sum_rows(value_t):
    return jnp.concatenate([value_t, jnp.ones((SUM_ROWS, value_t.shape[1]), BF16)], axis=0)


def _normalise(acc, v_rows):
    return acc[:v_rows] / acc[v_rows:v_rows + 1]


def _paired_causal_attention(a, n_tiles, n_sub, q_pair, key_block, value_t, diag_visible, v_rows):
    tq = ATT_TILE
    half = n_tiles // 2
    neg_row = jnp.full((1, tq), NEG, F32)

    def slot(n):
        if n == 0:
            return "A", None, a, True
        if n == n_tiles:
            return "B", None, n_tiles - 1 - a, True
        if n >= half:
            return "B", None, n - a - 1, False
        is_a = n <= a
        return None, is_a, jnp.where(is_a, n - 1, n - a - 1), False

    def pick(tile, is_a, on_a, on_b):
        if tile is not None:
            return on_a if tile == "A" else on_b
        return jnp.where(is_a, on_a, on_b)

    slots = [slot(n) for n in range(n_tiles + 1)]
    queries = [q_pair(sub) for sub in range(n_sub)]
    maxes = {(t, sub): neg_row for t in "AB" for sub in range(n_sub)}
    zero_acc = jnp.zeros((v_rows + SUM_ROWS, tq), F32)
    accs = {(t, sub): zero_acc for t in "AB" for sub in range(n_sub)}

    def scores(slot_, sub):
        tile, is_a, g, diagonal = slot_
        s = _dot_nt(key_block(g, sub), pick(tile, is_a, *queries[sub]))
        return jnp.where(diag_visible, s, NEG) if diagonal else s

    def absorb(slot_, sub, s):
        tile, is_a, g, _ = slot_
        m_old = pick(tile, is_a, maxes["A", sub], maxes["B", sub])
        m_new = jnp.maximum(m_old, jnp.max(s, axis=0, keepdims=True))
        alpha = jnp.exp2(m_old - m_new)
        p = jnp.exp2(s - m_new).astype(BF16)
        acc_n = _dot(_with_sum_rows(value_t(g, sub)), p)
        for t in "AB":
            if tile not in (None, t):
                continue
            updated = alpha * accs[t, sub] + acc_n
            if tile is None:
                mine = is_a if t == "A" else jnp.logical_not(is_a)
                accs[t, sub] = jnp.where(mine, updated, accs[t, sub])
                maxes[t, sub] = jnp.where(mine, m_new, maxes[t, sub])
            else:
                accs[t, sub], maxes[t, sub] = updated, m_new

    items = [(slot_, sub) for slot_ in slots for sub in range(n_sub)]
    in_flight = []
    for item in items:
        in_flight.append((item, scores(*item)))
        if len(in_flight) > MXU_LOOKAHEAD:
            (slot_, sub), s = in_flight.pop(0)
            absorb(slot_, sub, s)
    for (slot_, sub), s in in_flight:
        absorb(slot_, sub, s)
    return {t: [accs[t, sub] for sub in range(n_sub)] for t in "AB"}


HEADS_PER_STEP = 4


def _paired_specs(n_tiles, q_width, out_width):
    tq = ATT_TILE
    half = n_tiles // 2
    q_a = pl.BlockSpec((None, tq, q_width), lambda b, p, a: (b, a, p))
    q_b = pl.BlockSpec((None, tq, q_width), lambda b, p, a: (b, n_tiles - 1 - a, p))
    o_a = pl.BlockSpec((None, tq, out_width), lambda b, p, a: (b, a, p))
    o_b = pl.BlockSpec((None, tq, out_width), lambda b, p, a: (b, half - 1 - a, p))
    return q_a, q_b, o_a, o_b


def _fox_kernel(qa_ref, qb_ref, k_ref, vt_ref, oa_ref, ob_ref, *, n_tiles):
    tq = ATT_TILE

    def q_pair(sub):
        cols = slice(sub * LANES, (sub + 1) * LANES)
        return qa_ref[:, cols], qb_ref[:, cols]

    def key_block(g, sub):
        return k_ref[_key_rows(g), sub * LANES:(sub + 1) * LANES]

    def value_t(g, sub):
        return vt_ref[g, sub * HEAD_DIM:(sub + 1) * HEAD_DIM, :]

    causal = (lax.broadcasted_iota(jnp.int32, (tq, tq), 0)
              <= lax.broadcasted_iota(jnp.int32, (tq, tq), 1))
    res = _paired_causal_attention(pl.program_id(2), n_tiles, HEADS_PER_STEP, q_pair, key_block,
                                   value_t, causal, HEAD_DIM)
    for tile, o_ref in (("A", oa_ref), ("B", ob_ref)):
        o_t = jnp.concatenate([_normalise(acc, HEAD_DIM) for acc in res[tile]], axis=0)
        o_ref[...] = o_t.T.astype(o_ref.dtype)


def _fox_attention(q, k, vt, bsz, seq):
    tq = ATT_TILE
    n_tiles = seq // tq
    q3 = q.reshape(bsz, seq, A_HEADS * LANES)
    k3 = k.reshape(bsz, seq, A_HEADS * LANES)
    out_width = HEADS_PER_STEP * HEAD_DIM
    q_a, q_b, o_a, o_b = _paired_specs(n_tiles, HEADS_PER_STEP * LANES, out_width)
    half_out = jax.ShapeDtypeStruct((bsz, seq // 2, MIX_W), BF16)
    out_a, out_b = pl.pallas_call(
        functools.partial(_fox_kernel, n_tiles=n_tiles),
        out_shape=[half_out, half_out],
        grid=(bsz, A_HEADS // HEADS_PER_STEP, n_tiles // 2),
        in_specs=[q_a, q_b,
                  pl.BlockSpec((None, seq, HEADS_PER_STEP * LANES), lambda b, p, a: (b, 0, p)),
                  pl.BlockSpec((None, n_tiles, out_width, tq), lambda b, p, a: (b, 0, p, 0))],
        out_specs=[o_a, o_b],
        compiler_params=_params(3),
        name="fox_attention",
    )(q3, q3, k3, vt)
    return jnp.concatenate([out_a, out_b], axis=1).reshape(bsz * seq, MIX_W)


def _diff_kernel(qa_ref, qb_ref, k_ref, vt_ref, lam_ref, sg_ref, oa_ref, ob_ref, *,
                 lambda_init, n_tiles):
    tq = ATT_TILE
    lo = _lane_lo((tq, LANES))
    lp = lam_ref[...]
    lam = (jnp.exp(jnp.sum(lp[0:1] * lp[1:2], axis=1, keepdims=True))
           - jnp.exp(jnp.sum(lp[2:3] * lp[3:4], axis=1, keepdims=True)) + lambda_init)

    def half_of(q_ref, sub):
        q = q_ref[:, sub // 2 * LANES:(sub // 2 + 1) * LANES]
        zero = jnp.zeros_like(q)
        return jnp.where(lo, q, zero) if sub % 2 == 0 else jnp.where(lo, zero, q)

    def q_pair(sub):
        return half_of(qa_ref, sub), half_of(qb_ref, sub)

    def key_block(g, sub):
        return k_ref[_key_rows(g), sub // 2 * LANES:(sub // 2 + 1) * LANES]

    def value_t(g, sub):
        return vt_ref[g, sub // 2 * LANES:(sub // 2 + 1) * LANES, :]

    chunk_causal = (lax.broadcasted_iota(jnp.int32, (tq, tq), 0) // CHUNK
                    <= lax.broadcasted_iota(jnp.int32, (tq, tq), 1) // CHUNK)
    res = _paired_causal_attention(pl.program_id(2), n_tiles, HEADS_PER_STEP, q_pair, key_block,
                                   value_t, chunk_causal, LANES)
    for tile, o_ref in (("A", oa_ref), ("B", ob_ref)):
        for hd in range(HEADS_PER_STEP // 2):
            o1, o2 = (_normalise(acc, LANES) for acc in res[tile][2 * hd:2 * hd + 2])
            o = (o1 - lam * o2).T
            o_ref[:, hd * LANES:(hd + 1) * LANES] = (
                _rms(o) * sg_ref[...] * (1.0 - lambda_init)).astype(o_ref.dtype)


def _diff_attention(q, k, vt, lam_params, subln_gain, lambda_init, bsz, seq):
    tq = ATT_TILE
    n_tiles = seq // tq
    q3, k3 = (a.reshape(bsz, seq, MIX_W) for a in (q, k))
    width = HEADS_PER_STEP // 2 * LANES
    q_a, q_b, o_a, o_b = _paired_specs(n_tiles, width, width)
    half_out = jax.ShapeDtypeStruct((bsz, seq // 2, MIX_W), BF16)
    out_a, out_b = pl.pallas_call(
        functools.partial(_diff_kernel, lambda_init=lambda_init, n_tiles=n_tiles),
        out_shape=[half_out, half_out],
        grid=(bsz, MIX_W // width, n_tiles // 2),
        in_specs=[q_a, q_b,
                  pl.BlockSpec((None, seq, width), lambda b, h, a: (b, 0, h)),
                  pl.BlockSpec((None, n_tiles, width, tq), lambda b, h, a: (b, 0, h, 0)),
                  _const_spec((4, HEAD_DIM)), _const_spec((1, LANES))],
        out_specs=[o_a, o_b],
        compiler_params=_params(3),
        name="diff_attention",
    )(q3, q3, k3, vt, lam_params, subln_gain.reshape(1, LANES))
    return jnp.concatenate([out_a, out_b], axis=1).reshape(bsz * seq, MIX_W)


def _odd_prep_kernel(x_ref, g_ref, wc_ref, wd_ref, cw_ref, gains_ref, bd_ref,
                     c_ref, dq_ref, dk_ref, dvt_ref, carry_ref):
    tm = x_ref.shape[0]

    @pl.when(pl.program_id(1) == 0)
    def _():
        carry_ref[...] = jnp.zeros_like(carry_ref)

    h = (_rms(x_ref[...]) * g_ref[...]).astype(BF16)
    yc = _dot(h, wc_ref[...])
    u = yc[:, MIX_W:2 * MIX_W] * yc[:, 2 * MIX_W:]
    prev = carry_ref[...]
    carry_ref[...] = u[tm - 8:, :]
    row = lax.broadcasted_iota(jnp.int32, u.shape, 0)
    u1 = jnp.where(row == 0, prev[7:8], pltpu.roll(u, 1, axis=0))
    u2 = jnp.where(row == 0, prev[6:7], jnp.where(row == 1, prev[7:8], pltpu.roll(u, 2, axis=0)))
    cw = cw_ref[...]
    conv = cw[0:1] * u2 + cw[1:2] * u1 + cw[2:3] * u
    c_ref[...] = (yc[:, :MIX_W] * conv).astype(BF16)

    yd = _dot(h, wd_ref[...])
    bd = bd_ref[...]
    gains = gains_ref[...]
    dq_ref[...] = (_head_norm(yd[:, :MIX_W], bd, gains[0:1]) * Q_SCALE).astype(BF16)
    dk_ref[...] = _head_norm(yd[:, MIX_W:2 * MIX_W], bd, gains[1:2]).astype(BF16)
    _store_transposed(dvt_ref, yd[:, 2 * MIX_W:])


def _odd_prep(x, gain, w_in, conv_w, qk_gains, bsz, seq):
    t, d = x.shape
    tm = TOKEN_TILE
    w3 = 3 * MIX_W
    n_s = seq // tm
    tok = lambda w: pl.BlockSpec((tm, w), lambda b, s: (b * n_s + s, 0))
    stream = jax.ShapeDtypeStruct((t, MIX_W), BF16)
    return pl.pallas_call(
        _odd_prep_kernel,
        out_shape=[stream] * 3 + [jax.ShapeDtypeStruct((bsz, seq // ATT_TILE, MIX_W, ATT_TILE), BF16)],
        grid=(bsz, n_s),
        in_specs=[tok(d), _const_spec((1, d)), _const_spec((d, w3)), _const_spec((d, w3)),
                  _const_spec((3, MIX_W)), _const_spec((2, MIX_W)), _const_spec((BD_WIDTH, BD_WIDTH))],
        out_specs=[tok(MIX_W)] * 3 + [pl.BlockSpec((None, tm // ATT_TILE, MIX_W, ATT_TILE),
                                                   lambda b, s: (b, s, 0, 0))],
        scratch_shapes=[pltpu.VMEM((8, MIX_W), F32)],
        compiler_params=_params(2),
        name="odd_prep",
    )(x, gain.reshape(1, d), w_in[:, :w3].astype(BF16), w_in[:, w3:].astype(BF16), conv_w,
      _tile_heads(qk_gains), _block_diag_ones())


BAND_GROUPS = 1 + D_LEFT // ATT_TILE


BAND_TILES = 4
BAND_WIDTH = BAND_GROUPS * ATT_TILE
BIAS_ROW = BAND_WIDTH + ATT_TILE


def _band_kernel(q_ref, k_ref, vt_ref, w_ref, o_ref, bias_ref):
    tq = ATT_TILE
    j = pl.program_id(2)

    @pl.when((pl.program_id(1) == 0) & (j == 0))
    def _():
        key = lax.broadcasted_iota(jnp.int32, (BAND_WIDTH, tq), 0)
        chunk_start = lax.broadcasted_iota(jnp.int32, (BAND_WIDTH, tq), 1) // CHUNK * CHUNK
        in_band = (key >= chunk_start) & (key < chunk_start + D_LEFT + CHUNK)
        for sub in range(2):
            rows = jnp.broadcast_to(w_ref[sub], (BAND_WIDTH, BIAS_ROW))
            skew = pltpu.roll(rows, tq + 1, axis=1, stride=1, stride_axis=0)
            table = jnp.where(in_band, skew[:, :tq] * LOG2E, NEG)
            for grp in range(BAND_GROUPS):
                bias_ref[sub, grp] = table[grp * tq:(grp + 1) * tq, :]

    lo = _lane_lo((tq, LANES))
    before_start = jnp.where(j == 0, NEG, 0.0)
    q_subs, items = {}, []
    for tile in range(BAND_TILES):
        q = q_ref[tile * tq:(tile + 1) * tq, :]
        zero = jnp.zeros_like(q)
        q_subs[tile] = (jnp.where(lo, q, zero), jnp.where(lo, zero, q))
        items += [(tile, grp, sub) for grp in range(BAND_GROUPS) for sub in range(2)]

    def may_precede(tile, grp):
        return tile + grp < BAND_GROUPS - 1

    def group_index(tile, grp):
        g = BAND_TILES * j + tile + grp - (BAND_GROUPS - 1)
        return jnp.maximum(g, 0) if may_precede(tile, grp) else g

    def scores(tile, grp, sub):
        s = _dot_nt(k_ref[_key_rows(group_index(tile, grp)), :], q_subs[tile][sub]) + bias_ref[sub, grp]
        return s + before_start if may_precede(tile, grp) else s

    state = {}

    def absorb(tile, grp, sub, s):
        value_t = _with_sum_rows(vt_ref[group_index(tile, grp), sub * HEAD_DIM:(sub + 1) * HEAD_DIM, :])
        mx = jnp.max(s, axis=0, keepdims=True)
        if grp == 0:
            state[tile, sub] = (mx, _dot(value_t, jnp.exp2(s - mx).astype(BF16)))
            return
        m_old, acc = state[tile, sub]
        m_new = jnp.maximum(m_old, mx)
        acc = jnp.exp2(m_old - m_new) * acc + _dot(value_t, jnp.exp2(s - m_new).astype(BF16))
        state[tile, sub] = (m_new, acc)

    in_flight = []
    for item in items:
        in_flight.append((item, scores(*item)))
        if len(in_flight) > MXU_LOOKAHEAD:
            done, s = in_flight.pop(0)
            absorb(*done, s)
    for done, s in in_flight:
        absorb(*done, s)
    for tile in range(BAND_TILES):
        o_t = jnp.concatenate([_normalise(state[tile, sub][1], HEAD_DIM) for sub in range(2)], axis=0)
        o_ref[tile * tq:(tile + 1) * tq, :] = o_t.T.astype(o_ref.dtype)


def _band_bias_rows(rel_table):
    n_lo = ATT_TILE - 1 - REL_CLIP
    n_hi = BIAS_ROW - n_lo - (2 * REL_CLIP + 1)
    w = jnp.concatenate([jnp.repeat(rel_table[:, :1], n_lo, axis=1), rel_table,
                         jnp.repeat(rel_table[:, -1:], n_hi, axis=1)], axis=1)
    return w.astype(F32).reshape(rel_table.shape[0], 1, BIAS_ROW)


def _band_attention(q, k, vt, rel_table, bsz, seq):
    tq = ATT_TILE
    n_q = seq // tq
    q3, k3 = (a.reshape(bsz, seq, MIX_W) for a in (q, k))
    q_spec = pl.BlockSpec((None, BAND_TILES * tq, LANES), lambda p, b, j: (b, j, p))
    out = pl.pallas_call(
        _band_kernel,
        out_shape=jax.ShapeDtypeStruct((bsz, seq, MIX_W), BF16),
        grid=(MIX_W // LANES, bsz, n_q // BAND_TILES),
        in_specs=[q_spec,
                  pl.BlockSpec((None, seq, LANES), lambda p, b, j: (b, 0, p)),
                  pl.BlockSpec((None, n_q, LANES, tq), lambda p, b, j: (b, 0, p, 0)),
                  pl.BlockSpec((2, 1, BIAS_ROW), lambda p, b, j: (p, 0, 0))],
        out_specs=q_spec,
        scratch_shapes=[pltpu.VMEM((2, BAND_GROUPS, tq, tq), F32)],
        compiler_params=_params(3),
        name="band_attention",
    )(q3, k3, vt, _band_bias_rows(rel_table))
    return out.reshape(bsz * seq, MIX_W)


def _mem_kv_kernel(mem_ref, g_ref, w_ref, kg_ref, k_ref, v_ref):
    mem_n = (_rms(mem_ref[...]) * g_ref[...]).astype(BF16)
    kv = _dot(mem_n, w_ref[...])
    kg = kg_ref[...]
    for hd in range(X_HEADS):
        sl = slice(hd * X_HEAD_DIM, (hd + 1) * X_HEAD_DIM)
        k_ref[:, sl] = (_rms(kv[:, sl]) * kg).astype(BF16)
    v_ref[...] = kv[:, D_MODEL:].astype(BF16)


def _mem_kv(mem, gain, w_kv, k_gain):
    bsz, n_mem, d = mem.shape
    blk = pl.BlockSpec((None, n_mem, d), lambda b: (b, 0, 0))
    out = jax.ShapeDtypeStruct((bsz, n_mem, d), BF16)
    return pl.pallas_call(
        _mem_kv_kernel,
        out_shape=[out, out],
        grid=(bsz,),
        in_specs=[blk, _const_spec((1, d)), _const_spec((d, 2 * d)), _const_spec((1, X_HEAD_DIM))],
        out_specs=[blk, blk],
        compiler_params=_params(1),
        name="mem_kv",
    )(mem, gain.reshape(1, d), w_kv.astype(BF16), k_gain.reshape(1, X_HEAD_DIM))


def _cross_kernel(x_ref, left_ref, right_ref, wl_ref, wr_ref, g_ref, wq_ref, qg_ref, k_ref, v_ref,
                  wo_ref, o_ref):
    x = x_ref[...] + _dot(left_ref[...], wl_ref[...]) + _dot(right_ref[...], wr_ref[...])
    h = (_rms(x) * g_ref[...]).astype(BF16)
    q = _dot(h, wq_ref[...])
    qg = qg_ref[...]
    heads = []
    for hd in range(X_HEADS):
        sl = slice(hd * X_HEAD_DIM, (hd + 1) * X_HEAD_DIM)
        qh = (_rms(q[:, sl]) * qg * X_HEAD_DIM ** -0.5).astype(BF16)
        s = _dot_nt(qh, k_ref[:, sl])
        p = jnp.exp(s - jnp.max(s, axis=-1, keepdims=True))
        l = jnp.sum(p, axis=-1, keepdims=True)
        heads.append((_dot(p.astype(BF16), v_ref[:, sl]) / l).astype(BF16))
    o_ref[...] = x + _dot(jnp.concatenate(heads, axis=1), wo_ref[...])


def _mix_out_cross_attention(x, left, right, w_mix_out, gain, w_q, q_gain, k, v, w_o, bsz, seq):
    t, d = x.shape
    tq = TOKEN_TILE
    n_s = seq // tq
    n_mem = k.shape[1]
    tok = lambda width: pl.BlockSpec((tq, width), lambda b, s: (b * n_s + s, 0))
    mem_spec = pl.BlockSpec((None, n_mem, d), lambda b, s: (b, 0, 0))
    w_mix = w_mix_out.astype(BF16)
    return pl.pallas_call(
        _cross_kernel,
        out_shape=jax.ShapeDtypeStruct((t, d), F32),
        grid=(bsz, n_s),
        in_specs=[tok(d), tok(MIX_W), tok(MIX_W), _const_spec((MIX_W, d)), _const_spec((MIX_W, d)),
                  _const_spec((1, d)), _const_spec((d, d)), _const_spec((1, X_HEAD_DIM)),
                  mem_spec, mem_spec, _const_spec((d, d))],
        out_specs=tok(d),
        compiler_params=_params(2),
        name="cross_attention",
    )(x, left, right, w_mix[:MIX_W], w_mix[MIX_W:], gain.reshape(1, d), w_q.astype(BF16),
      q_gain.reshape(1, X_HEAD_DIM), k, v, w_o.astype(BF16))


def kernel(x, mem, positions, ln_gains, ffn1_w_in, ffn1_w_out, ffn2_w_in, ffn2_w_out, even_w_in, even_f_bias, even_qk_gains, even_lambda, even_subln_gain, even_w_out, odd_w_in, odd_conv_w, odd_qk_gains, odd_rel_bias, odd_w_out, x_w_q, x_w_kv, x_qk_gains, x_w_o):
    bsz, seq, d = x.shape
    depth = ln_gains.shape[0]
    rope = _rope_tables(positions)
    x = x.reshape(bsz * seq, d)
    for layer in range(depth):
        g = ln_gains[layer]
        x = _ffn(x, g[0], ffn1_w_in[layer], ffn1_w_out[layer])
        if layer % 2 == 0:
            e = layer // 2
            lambda_init = 0.8 - 0.6 * math.exp(-0.3 * layer)
            aq, ak, avt, bq, bk, bvt = _even_prep(
                x, g[1], even_w_in[e], even_f_bias[e], even_qk_gains[e], rope, bsz, seq)
            left = _fox_attention(aq, ak, avt, bsz, seq)
            right = _diff_attention(bq, bk, bvt, even_lambda[e], even_subln_gain[e], lambda_init,
                                    bsz, seq)
            w_mix_out = even_w_out[e]
        else:
            o = layer // 2
            left, dq, dk, dvt = _odd_prep(x, g[1], odd_w_in[o], odd_conv_w[o], odd_qk_gains[o],
                                          bsz, seq)
            right = _band_attention(dq, dk, dvt, odd_rel_bias[o], bsz, seq)
            w_mix_out = odd_w_out[o]
        mk, mv = _mem_kv(mem, g[3], x_w_kv[layer], x_qk_gains[layer, 1])
        x = _mix_out_cross_attention(x, left, right, w_mix_out, g[2], x_w_q[layer],
                                     x_qk_gains[layer, 0], mk, mv, x_w_o[layer], bsz, seq)
        x = _ffn(x, g[4], ffn2_w_in[layer], ffn2_w_out[layer])
    return x.reshape(bsz, seq, d)
```

```python
import functools
import math

import jax
import jax.numpy as jnp
from jax import lax
from jax.experimental import pallas as pl
from jax.experimental.pallas import tpu as pltpu

D_MODEL = 1024
CHUNK = 64
HEAD_DIM = 64
ROPE_DIM = HEAD_DIM // 4
ROPE_THETA = 500000.0
RMS_EPS = 1e-6
D_FF = 2816
A_HEADS = 8
MIX_W = 512
LANES = 128
D_LEFT = 8 * CHUNK
REL_CLIP = 128
X_HEADS = 4
X_HEAD_DIM = D_MODEL // X_HEADS
NEG = -1e30
LOG2E = math.log2(math.e)
Q_SCALE = HEAD_DIM ** -0.5 * LOG2E

VMEM_LIMIT = 56 * 1024 * 1024
FFN_CHUNKS = (768, 768, 768, 512)
TOKEN_TILE = 512
ATT_TILE = 256

BF16 = jnp.bfloat16
F32 = jnp.float32


def _params(n_axes):
    return pltpu.CompilerParams(dimension_semantics=("arbitrary",) * n_axes,
                                vmem_limit_bytes=VMEM_LIMIT)


def _const_spec(shape):
    nd = len(shape)
    return pl.BlockSpec(shape, lambda *_: (0,) * nd, pipeline_mode=pl.Buffered(1))


def _rms(x):
    return x * lax.rsqrt(jnp.mean(x * x, axis=-1, keepdims=True) + RMS_EPS)


def _dot(a, b):
    return jnp.dot(a, b, preferred_element_type=F32)


def _dot_nt(a, b):
    return lax.dot_general(a, b, (((1,), (1,)), ((), ())), preferred_element_type=F32)


def _ffn_kernel(x_ref, g_ref, wg_ref, wu_ref, wo_ref, o_ref):
    x = x_ref[...]
    xn = (_rms(x) * g_ref[...]).astype(BF16)
    y = jnp.zeros_like(x)
    start = 0
    for width in FFN_CHUNKS:
        gate = _dot(xn, wg_ref[:, start:start + width])
        up = _dot(xn, wu_ref[:, start:start + width])
        act = (gate * (1.0 / (1.0 + jnp.exp(-gate))) * up).astype(BF16)
        y = y + _dot(act, wo_ref[start:start + width, :])
        start += width
    o_ref[...] = x + 0.5 * y


def _ffn(x, gain, w_in, w_out):
    t, d = x.shape
    tm = TOKEN_TILE
    wg = w_in[:, :D_FF].astype(BF16)
    wu = w_in[:, D_FF:].astype(BF16)
    wo = w_out.astype(BF16)
    return pl.pallas_call(
        _ffn_kernel,
        out_shape=jax.ShapeDtypeStruct((t, d), F32),
        grid=(t // tm,),
        in_specs=[pl.BlockSpec((tm, d), lambda i: (i, 0)),
                  _const_spec((1, d)),
                  _const_spec((d, D_FF)), _const_spec((d, D_FF)), _const_spec((D_FF, d))],
        out_specs=pl.BlockSpec((tm, d), lambda i: (i, 0)),
        compiler_params=_params(1),
        name="ffn",
    )(x, gain.reshape(1, d), wg, wu, wo)


BD_WIDTH = 256


def _head_norm(y, ones_bd, gain):
    sq = (y * y).astype(BF16)
    msq = jnp.concatenate([_dot(sq[:, c:c + BD_WIDTH], ones_bd) for c in range(0, MIX_W, BD_WIDTH)],
                          axis=1) * (1.0 / HEAD_DIM)
    return y * lax.rsqrt(msq + RMS_EPS) * gain


def _block_diag_ones():
    r = jnp.arange(BD_WIDTH) // HEAD_DIM
    return (r[:, None] == r[None, :]).astype(BF16)


def _tile_heads(v):
    return jnp.tile(v, (1, MIX_W // HEAD_DIM))


def _store_transposed(vt_ref, v):
    for i in range(v.shape[0] // ATT_TILE):
        vt_ref[i] = v[i * ATT_TILE:(i + 1) * ATT_TILE, :].T.astype(BF16)


def _split3(f):
    hi = f.astype(BF16).astype(F32)
    rest = f - hi
    mid = rest.astype(BF16).astype(F32)
    return hi, mid, rest - mid


def _even_prep_kernel(x_ref, g_ref, wa_ref, wf_ref, fb_ref, wb_ref, gains_ref, bd_ref,
                      ra_ref, rm_ref, rp_ref,
                      aq_ref, ak_ref, avt_ref, bq_ref, bk_ref, bvt_ref, carry_ref):
    tm = x_ref.shape[0]

    @pl.when(pl.program_id(1) == 0)
    def _():
        carry_ref[...] = jnp.zeros_like(carry_ref)

    h = (_rms(x_ref[...]) * g_ref[...]).astype(BF16)
    bd = bd_ref[...]
    gains = gains_ref[...]

    def project(w_ref, part):
        return _dot(h, w_ref[:, part * MIX_W:(part + 1) * MIX_W])

    qn = _head_norm(project(wa_ref, 0), bd, gains[0:1]) * Q_SCALE
    kn = _head_norm(project(wa_ref, 1), bd, gains[1:2])
    _store_transposed(avt_ref, project(wa_ref, 2))

    z = _dot(h, wf_ref[...]) + fb_ref[...]
    logf = jnp.minimum(z, 0.0) - jnp.log1p(jnp.exp(-jnp.abs(z)))
    row = lax.broadcasted_iota(jnp.int32, logf.shape, 0)
    step = 1
    while step < tm:
        logf = logf + jnp.where(row >= step, pltpu.roll(logf, step, axis=0), 0.0)
        step *= 2
    cum = logf + carry_ref[...]
    carry_ref[...] = cum[tm - 1:tm, :]

    hi, mid, lo = _split3(cum * LOG2E)
    lane = lax.broadcasted_iota(jnp.int32, (tm, LANES), 1)
    is_hi = (lane == HEAD_DIM) | (lane == HEAD_DIM + 3)
    is_mid = (lane == HEAD_DIM + 1) | (lane == HEAD_DIM + 4)
    ones_hi = jnp.where(lane < HEAD_DIM + 6, 1.0, 0.0)
    for hd in range(A_HEADS):
        pieces = jnp.where(is_hi, hi[:, hd:hd + 1],
                           jnp.where(is_mid, mid[:, hd:hd + 1], lo[:, hd:hd + 1]))
        blk = slice((hd // 2) * LANES, (hd // 2 + 1) * LANES)
        sq, sk = qn[:, blk], kn[:, blk]
        if hd % 2:
            sq, sk = pltpu.roll(sq, HEAD_DIM, axis=1), pltpu.roll(sk, HEAD_DIM, axis=1)
        qa = jnp.where(lane < HEAD_DIM, sq, jnp.where(lane < HEAD_DIM + 3, pieces, ones_hi))
        ka = jnp.where(lane < HEAD_DIM, sk, jnp.where(lane < HEAD_DIM + 3, 1.0,
                                                      jnp.where(lane < HEAD_DIM + 6, -pieces, 0.0)))
        aq_ref[:, hd * LANES:(hd + 1) * LANES] = qa.astype(BF16)
        ak_ref[:, hd * LANES:(hd + 1) * LANES] = ka.astype(BF16)

    rep = MIX_W // LANES
    ra = jnp.tile(ra_ref[...], (1, rep))
    rm = jnp.tile(rm_ref[...], (1, rep))
    rp = jnp.tile(rp_ref[...], (1, rep))

    def rope(v):
        half = ROPE_DIM // 2
        return v * ra + pltpu.roll(v, MIX_W - half, axis=1) * rm + pltpu.roll(v, half, axis=1) * rp

    bq = rope(_head_norm(project(wb_ref, 0), bd, gains[2:3]))
    bk = rope(_head_norm(project(wb_ref, 1), bd, gains[3:4]))
    bq_ref[...] = (bq * Q_SCALE).astype(BF16)
    bk_ref[...] = bk.astype(BF16)
    _store_transposed(bvt_ref, project(wb_ref, 2))


def _rope_tables(positions):
    half = ROPE_DIM // 2
    inv = ROPE_THETA ** (-jnp.arange(0, ROPE_DIM, 2, dtype=F32) / ROPE_DIM)
    ang = positions.astype(F32).reshape(-1, 1) * inv
    cos = jnp.tile(jnp.cos(ang), (1, LANES // half))
    sin = jnp.tile(jnp.sin(ang), (1, LANES // half))
    in_head = jnp.arange(LANES) % HEAD_DIM
    ra = jnp.where(in_head < ROPE_DIM, cos, 1.0)
    rm = jnp.where(in_head < half, -sin, 0.0)
    rp = jnp.where((in_head >= half) & (in_head < ROPE_DIM), sin, 0.0)
    return ra, rm, rp


def _even_prep(x, gain, w_in, f_bias, qk_gains, rope, bsz, seq):
    t, d = x.shape
    tm = TOKEN_TILE
    a_w = 3 * MIX_W
    wa = w_in[:, :a_w].astype(BF16)
    wf = jnp.pad(w_in[:, a_w:a_w + A_HEADS], ((0, 0), (0, LANES - A_HEADS))).astype(BF16)
    wb = w_in[:, a_w + A_HEADS:].astype(BF16)
    fb = jnp.pad(f_bias, (0, LANES - A_HEADS)).reshape(1, LANES)
    n_s = seq // tm
    tok = lambda w: pl.BlockSpec((tm, w), lambda b, s: (b * n_s + s, 0))
    vt_spec = pl.BlockSpec((None, tm // ATT_TILE, MIX_W, ATT_TILE), lambda b, s: (b, s, 0, 0))
    stream = lambda w: jax.ShapeDtypeStruct((t, w), BF16)
    vt_shape = jax.ShapeDtypeStruct((bsz, seq // ATT_TILE, MIX_W, ATT_TILE), BF16)
    return pl.pallas_call(
        _even_prep_kernel,
        out_shape=[stream(A_HEADS * LANES), stream(A_HEADS * LANES), vt_shape,
                   stream(MIX_W), stream(MIX_W), vt_shape],
        grid=(bsz, n_s),
        in_specs=[tok(d), _const_spec((1, d)), _const_spec((d, a_w)), _const_spec((d, LANES)),
                  _const_spec((1, LANES)), _const_spec((d, a_w)), _const_spec((4, MIX_W)),
                  _const_spec((BD_WIDTH, BD_WIDTH)), tok(LANES), tok(LANES), tok(LANES)],
        out_specs=[tok(A_HEADS * LANES), tok(A_HEADS * LANES), vt_spec,
                   tok(MIX_W), tok(MIX_W), vt_spec],
        scratch_shapes=[pltpu.VMEM((1, LANES), F32)],
        compiler_params=_params(2),
        name="even_prep",
    )(x, gain.reshape(1, d), wa, wf, fb, wb, _tile_heads(qk_gains), _block_diag_ones(), *rope)


def _lane_lo(shape):
    return lax.broadcasted_iota(jnp.int32, shape, len(shape) - 1) < HEAD_DIM


def _key_rows(g):
    if isinstance(g, int):
        return slice(g * ATT_TILE, (g + 1) * ATT_TILE)
    return pl.ds(pl.multiple_of(g * ATT_TILE, ATT_TILE), ATT_TILE)


SUM_ROWS = 16
MXU_LOOKAHEAD = 5


def _with_sum_rows(value_t):
    return jnp.concatenate([value_t, jnp.ones((SUM_ROWS, value_t.shape[1]), BF16)], axis=0)


def _normalise(acc, v_rows):
    return acc[:v_rows] / acc[v_rows:v_rows + 1]


def _paired_causal_attention(a, n_tiles, n_sub, queries, key_block, value_t, diag_visible, v_rows):
    tq = ATT_TILE
    tiles = (a, n_tiles - 1 - a)
    items = [(t, g, sub) for t, tile in enumerate(tiles) for g in range(tile + 1)
             for sub in range(n_sub)]
    maxes = [[jnp.full((1, tq), NEG, F32)] * n_sub for _ in tiles]
    accs = [[jnp.zeros((v_rows + SUM_ROWS, tq), F32)] * n_sub for _ in tiles]

    def scores(t, g, sub):
        s = _dot_nt(key_block(g, sub), queries[t][sub])
        return jnp.where(diag_visible, s, NEG) if g == tiles[t] else s

    def exponentials(t, g, sub, s):
        m_old = maxes[t][sub]
        m_new = jnp.maximum(m_old, jnp.max(s, axis=0, keepdims=True))
        maxes[t][sub] = m_new
        return jnp.exp2(s - m_new).astype(BF16), jnp.exp2(m_old - m_new)

    def accumulate(t, g, sub, p, alpha):
        accs[t][sub] = alpha * accs[t][sub] + _dot(_with_sum_rows(value_t(g, sub)), p)

    in_flight = []
    for item in items:
        in_flight.append((item, exponentials(*item, scores(*item))))
        if len(in_flight) > MXU_LOOKAHEAD:
            done, (p, alpha) = in_flight.pop(0)
            accumulate(*done, p, alpha)
    for done, (p, alpha) in in_flight:
        accumulate(*done, p, alpha)
    return accs


def _for_each_tile_pair(n_tiles, body):
    for a in range(n_tiles // 2):
        pl.when(pl.program_id(2) == a)(functools.partial(body, a))


HEADS_PER_STEP = 4


def _paired_specs(n_tiles, q_width, out_width):
    tq = ATT_TILE
    half = n_tiles // 2
    q_a = pl.BlockSpec((None, tq, q_width), lambda b, p, a: (b, a, p))
    q_b = pl.BlockSpec((None, tq, q_width), lambda b, p, a: (b, n_tiles - 1 - a, p))
    o_a = pl.BlockSpec((None, tq, out_width), lambda b, p, a: (b, a, p))
    o_b = pl.BlockSpec((None, tq, out_width), lambda b, p, a: (b, half - 1 - a, p))
    return q_a, q_b, o_a, o_b


def _fox_kernel(qa_ref, qb_ref, k_ref, vt_ref, oa_ref, ob_ref, *, n_tiles):
    tq = ATT_TILE

    def key_block(g, sub):
        return k_ref[_key_rows(g), sub * LANES:(sub + 1) * LANES]

    def value_t(g, sub):
        return vt_ref[g, sub * HEAD_DIM:(sub + 1) * HEAD_DIM, :]

    def body(a):
        queries = [[q_ref[:, sub * LANES:(sub + 1) * LANES] for sub in range(HEADS_PER_STEP)]
                   for q_ref in (qa_ref, qb_ref)]
        causal = (lax.broadcasted_iota(jnp.int32, (tq, tq), 0)
                  <= lax.broadcasted_iota(jnp.int32, (tq, tq), 1))
        res = _paired_causal_attention(a, n_tiles, HEADS_PER_STEP, queries, key_block, value_t,
                                       causal, HEAD_DIM)
        for accs, o_ref in zip(res, (oa_ref, ob_ref)):
            o_t = jnp.concatenate([_normalise(acc, HEAD_DIM) for acc in accs], axis=0)
            o_ref[...] = o_t.T.astype(o_ref.dtype)

    _for_each_tile_pair(n_tiles, body)


def _fox_attention(q, k, vt, bsz, seq):
    tq = ATT_TILE
    n_tiles = seq // tq
    q3 = q.reshape(bsz, seq, A_HEADS * LANES)
    k3 = k.reshape(bsz, seq, A_HEADS * LANES)
    out_width = HEADS_PER_STEP * HEAD_DIM
    q_a, q_b, o_a, o_b = _paired_specs(n_tiles, HEADS_PER_STEP * LANES, out_width)
    half_out = jax.ShapeDtypeStruct((bsz, seq // 2, MIX_W), BF16)
    out_a, out_b = pl.pallas_call(
        functools.partial(_fox_kernel, n_tiles=n_tiles),
        out_shape=[half_out, half_out],
        grid=(bsz, A_HEADS // HEADS_PER_STEP, n_tiles // 2),
        in_specs=[q_a, q_b,
                  pl.BlockSpec((None, seq, HEADS_PER_STEP * LANES), lambda b, p, a: (b, 0, p)),
                  pl.BlockSpec((None, n_tiles, out_width, tq), lambda b, p, a: (b, 0, p, 0))],
        out_specs=[o_a, o_b],
        compiler_params=_params(3),
        name="fox_attention",
    )(q3, q3, k3, vt)
    return jnp.concatenate([out_a, out_b], axis=1).reshape(bsz * seq, MIX_W)


def _diff_kernel(qa_ref, qb_ref, k_ref, vt_ref, lam_ref, sg_ref, oa_ref, ob_ref, *,
                 lambda_init, n_tiles):
    tq = ATT_TILE
    lo = _lane_lo((tq, LANES))
    lp = lam_ref[...]
    lam = (jnp.exp(jnp.sum(lp[0:1] * lp[1:2], axis=1, keepdims=True))
           - jnp.exp(jnp.sum(lp[2:3] * lp[3:4], axis=1, keepdims=True)) + lambda_init)

    def half_of(q_ref, sub):
        q = q_ref[:, sub // 2 * LANES:(sub // 2 + 1) * LANES]
        zero = jnp.zeros_like(q)
        return jnp.where(lo, q, zero) if sub % 2 == 0 else jnp.where(lo, zero, q)

    def key_block(g, sub):
        return k_ref[_key_rows(g), sub // 2 * LANES:(sub // 2 + 1) * LANES]

    def value_t(g, sub):
        return vt_ref[g, sub // 2 * LANES:(sub // 2 + 1) * LANES, :]

    def body(a):
        queries = [[half_of(q_ref, sub) for sub in range(HEADS_PER_STEP)]
                   for q_ref in (qa_ref, qb_ref)]
        chunk_causal = (lax.broadcasted_iota(jnp.int32, (tq, tq), 0) // CHUNK
                        <= lax.broadcasted_iota(jnp.int32, (tq, tq), 1) // CHUNK)
        res = _paired_causal_attention(a, n_tiles, HEADS_PER_STEP, queries, key_block, value_t,
                                       chunk_causal, LANES)
        for accs, o_ref in zip(res, (oa_ref, ob_ref)):
            for hd in range(HEADS_PER_STEP // 2):
                o1, o2 = (_normalise(acc, LANES) for acc in accs[2 * hd:2 * hd + 2])
                o = (o1 - lam * o2).T
                o_ref[:, hd * LANES:(hd + 1) * LANES] = (
                    _rms(o) * sg_ref[...] * (1.0 - lambda_init)).astype(o_ref.dtype)

    _for_each_tile_pair(n_tiles, body)


def _diff_attention(q, k, vt, lam_params, subln_gain, lambda_init, bsz, seq):
    tq = ATT_TILE
    n_tiles = seq // tq
    q3, k3 = (a.reshape(bsz, seq, MIX_W) for a in (q, k))
    width = HEADS_PER_STEP // 2 * LANES
    q_a, q_b, o_a, o_b = _paired_specs(n_tiles, width, width)
    half_out = jax.ShapeDtypeStruct((bsz, seq // 2, MIX_W), BF16)
    out_a, out_b = pl.pallas_call(
        functools.partial(_diff_kernel, lambda_init=lambda_init, n_tiles=n_tiles),
        out_shape=[half_out, half_out],
        grid=(bsz, MIX_W // width, n_tiles // 2),
        in_specs=[q_a, q_b,
                  pl.BlockSpec((None, seq, width), lambda b, h, a: (b, 0, h)),
                  pl.BlockSpec((None, n_tiles, width, tq), lambda b, h, a: (b, 0, h, 0)),
                  _const_spec((4, HEAD_DIM)), _const_spec((1, LANES))],
        out_specs=[o_a, o_b],
        compiler_params=_params(3),
        name="diff_attention",
    )(q3, q3, k3, vt, lam_params, subln_gain.reshape(1, LANES))
    return jnp.concatenate([out_a, out_b], axis=1).reshape(bsz * seq, MIX_W)


def _odd_prep_kernel(x_ref, g_ref, wc_ref, wd_ref, cw_ref, gains_ref, bd_ref,
                     c_ref, dq_ref, dk_ref, dvt_ref, carry_ref):
    tm = x_ref.shape[0]

    @pl.when(pl.program_id(1) == 0)
    def _():
        carry_ref[...] = jnp.zeros_like(carry_ref)

    h = (_rms(x_ref[...]) * g_ref[...]).astype(BF16)
    def project(w_ref, part):
        return _dot(h, w_ref[:, part * MIX_W:(part + 1) * MIX_W])

    u = project(wc_ref, 1) * project(wc_ref, 2)
    prev = carry_ref[...]
    carry_ref[...] = u[tm - 8:, :]
    row = lax.broadcasted_iota(jnp.int32, u.shape, 0)
    u1 = jnp.where(row == 0, prev[7:8], pltpu.roll(u, 1, axis=0))
    u2 = jnp.where(row == 0, prev[6:7], jnp.where(row == 1, prev[7:8], pltpu.roll(u, 2, axis=0)))
    cw = cw_ref[...]
    conv = cw[0:1] * u2 + cw[1:2] * u1 + cw[2:3] * u
    c_ref[...] = (project(wc_ref, 0) * conv).astype(BF16)

    bd = bd_ref[...]
    gains = gains_ref[...]
    dq_ref[...] = (_head_norm(project(wd_ref, 0), bd, gains[0:1]) * Q_SCALE).astype(BF16)
    dk_ref[...] = _head_norm(project(wd_ref, 1), bd, gains[1:2]).astype(BF16)
    _store_transposed(dvt_ref, project(wd_ref, 2))


def _odd_prep(x, gain, w_in, conv_w, qk_gains, bsz, seq):
    t, d = x.shape
    tm = TOKEN_TILE
    w3 = 3 * MIX_W
    n_s = seq // tm
    tok = lambda w: pl.BlockSpec((tm, w), lambda b, s: (b * n_s + s, 0))
    stream = jax.ShapeDtypeStruct((t, MIX_W), BF16)
    return pl.pallas_call(
        _odd_prep_kernel,
        out_shape=[stream] * 3 + [jax.ShapeDtypeStruct((bsz, seq // ATT_TILE, MIX_W, ATT_TILE), BF16)],
        grid=(bsz, n_s),
        in_specs=[tok(d), _const_spec((1, d)), _const_spec((d, w3)), _const_spec((d, w3)),
                  _const_spec((3, MIX_W)), _const_spec((2, MIX_W)), _const_spec((BD_WIDTH, BD_WIDTH))],
        out_specs=[tok(MIX_W)] * 3 + [pl.BlockSpec((None, tm // ATT_TILE, MIX_W, ATT_TILE),
                                                   lambda b, s: (b, s, 0, 0))],
        scratch_shapes=[pltpu.VMEM((8, MIX_W), F32)],
        compiler_params=_params(2),
        name="odd_prep",
    )(x, gain.reshape(1, d), w_in[:, :w3].astype(BF16), w_in[:, w3:].astype(BF16), conv_w,
      _tile_heads(qk_gains), _block_diag_ones())


BAND_GROUPS = 1 + D_LEFT // ATT_TILE


BAND_TILES = 4
BAND_WIDTH = BAND_GROUPS * ATT_TILE
BIAS_ROW = BAND_WIDTH + ATT_TILE


def _band_kernel(q_ref, k_ref, vt_ref, w_ref, o_ref, bias_ref):
    tq = ATT_TILE
    j = pl.program_id(2)

    @pl.when((pl.program_id(1) == 0) & (j == 0))
    def _():
        key = lax.broadcasted_iota(jnp.int32, (BAND_WIDTH, tq), 0)
        chunk_start = lax.broadcasted_iota(jnp.int32, (BAND_WIDTH, tq), 1) // CHUNK * CHUNK
        in_band = (key >= chunk_start) & (key < chunk_start + D_LEFT + CHUNK)
        for sub in range(2):
            rows = jnp.broadcast_to(w_ref[sub], (BAND_WIDTH, BIAS_ROW))
            skew = pltpu.roll(rows, tq + 1, axis=1, stride=1, stride_axis=0)
            table = jnp.where(in_band, skew[:, :tq] * LOG2E, NEG)
            for grp in range(BAND_GROUPS):
                bias_ref[sub, grp] = table[grp * tq:(grp + 1) * tq, :]

    lo = _lane_lo((tq, LANES))
    before_start = jnp.where(j == 0, NEG, 0.0)
    q_subs, items = {}, []
    for tile in range(BAND_TILES):
        q = q_ref[tile * tq:(tile + 1) * tq, :]
        zero = jnp.zeros_like(q)
        q_subs[tile] = (jnp.where(lo, q, zero), jnp.where(lo, zero, q))
        items += [(tile, grp, sub) for grp in range(BAND_GROUPS) for sub in range(2)]

    def may_precede(tile, grp):
        return tile + grp < BAND_GROUPS - 1

    def group_index(tile, grp):
        g = BAND_TILES * j + tile + grp - (BAND_GROUPS - 1)
        return jnp.maximum(g, 0) if may_precede(tile, grp) else g

    def scores(tile, grp, sub):
        s = _dot_nt(k_ref[_key_rows(group_index(tile, grp)), :], q_subs[tile][sub]) + bias_ref[sub, grp]
        return s + before_start if may_precede(tile, grp) else s

    maxes, accs = {}, {}

    def exponentials(tile, grp, sub, s):
        mx = jnp.max(s, axis=0, keepdims=True)
        m_old = maxes.get((tile, sub))
        m_new = mx if m_old is None else jnp.maximum(m_old, mx)
        maxes[tile, sub] = m_new
        return jnp.exp2(s - m_new).astype(BF16), None if m_old is None else jnp.exp2(m_old - m_new)

    def accumulate(tile, grp, sub, p, alpha):
        value_t = _with_sum_rows(vt_ref[group_index(tile, grp), sub * HEAD_DIM:(sub + 1) * HEAD_DIM, :])
        acc_n = _dot(value_t, p)
        accs[tile, sub] = acc_n if alpha is None else alpha * accs[tile, sub] + acc_n

    in_flight = []
    for item in items:
        in_flight.append((item, exponentials(*item, scores(*item))))
        if len(in_flight) > MXU_LOOKAHEAD:
            done, (p, alpha) = in_flight.pop(0)
            accumulate(*done, p, alpha)
    for done, (p, alpha) in in_flight:
        accumulate(*done, p, alpha)
    for tile in range(BAND_TILES):
        o_t = jnp.concatenate([_normalise(accs[tile, sub], HEAD_DIM) for sub in range(2)], axis=0)
        o_ref[tile * tq:(tile + 1) * tq, :] = o_t.T.astype(o_ref.dtype)


def _band_bias_rows(rel_table):
    n_lo = ATT_TILE - 1 - REL_CLIP
    n_hi = BIAS_ROW - n_lo - (2 * REL_CLIP + 1)
    w = jnp.concatenate([jnp.repeat(rel_table[:, :1], n_lo, axis=1), rel_table,
                         jnp.repeat(rel_table[:, -1:], n_hi, axis=1)], axis=1)
    return w.astype(F32).reshape(rel_table.shape[0], 1, BIAS_ROW)


def _band_attention(q, k, vt, rel_table, bsz, seq):
    tq = ATT_TILE
    n_q = seq // tq
    q3, k3 = (a.reshape(bsz, seq, MIX_W) for a in (q, k))
    q_spec = pl.BlockSpec((None, BAND_TILES * tq, LANES), lambda p, b, j: (b, j, p))
    out = pl.pallas_call(
        _band_kernel,
        out_shape=jax.ShapeDtypeStruct((bsz, seq, MIX_W), BF16),
        grid=(MIX_W // LANES, bsz, n_q // BAND_TILES),
        in_specs=[q_spec,
                  pl.BlockSpec((None, seq, LANES), lambda p, b, j: (b, 0, p)),
                  pl.BlockSpec((None, n_q, LANES, tq), lambda p, b, j: (b, 0, p, 0)),
                  pl.BlockSpec((2, 1, BIAS_ROW), lambda p, b, j: (p, 0, 0))],
        out_specs=q_spec,
        scratch_shapes=[pltpu.VMEM((2, BAND_GROUPS, tq, tq), F32)],
        compiler_params=_params(3),
        name="band_attention",
    )(q3, k3, vt, _band_bias_rows(rel_table))
    return out.reshape(bsz * seq, MIX_W)


def _mem_kv_kernel(mem_ref, g_ref, w_ref, kg_ref, k_ref, v_ref):
    mem_n = (_rms(mem_ref[...]) * g_ref[...]).astype(BF16)
    kv = _dot(mem_n, w_ref[...])
    kg = kg_ref[...]
    for hd in range(X_HEADS):
        sl = slice(hd * X_HEAD_DIM, (hd + 1) * X_HEAD_DIM)
        k_ref[:, sl] = (_rms(kv[:, sl]) * kg).astype(BF16)
    v_ref[...] = kv[:, D_MODEL:].astype(BF16)


def _mem_kv(mem, gain, w_kv, k_gain):
    bsz, n_mem, d = mem.shape
    blk = pl.BlockSpec((None, n_mem, d), lambda b: (b, 0, 0))
    out = jax.ShapeDtypeStruct((bsz, n_mem, d), BF16)
    return pl.pallas_call(
        _mem_kv_kernel,
        out_shape=[out, out],
        grid=(bsz,),
        in_specs=[blk, _const_spec((1, d)), _const_spec((d, 2 * d)), _const_spec((1, X_HEAD_DIM))],
        out_specs=[blk, blk],
        compiler_params=_params(1),
        name="mem_kv",
    )(mem, gain.reshape(1, d), w_kv.astype(BF16), k_gain.reshape(1, X_HEAD_DIM))


def _cross_kernel(x_ref, left_ref, right_ref, wl_ref, wr_ref, g_ref, wq_ref, qg_ref, k_ref, v_ref,
                  wo_ref, o_ref):
    x = x_ref[...] + _dot(left_ref[...], wl_ref[...]) + _dot(right_ref[...], wr_ref[...])
    h = (_rms(x) * g_ref[...]).astype(BF16)
    q = _dot(h, wq_ref[...])
    qg = qg_ref[...]
    heads = []
    for hd in range(X_HEADS):
        sl = slice(hd * X_HEAD_DIM, (hd + 1) * X_HEAD_DIM)
        qh = (_rms(q[:, sl]) * qg * X_HEAD_DIM ** -0.5).astype(BF16)
        s = _dot_nt(qh, k_ref[:, sl])
        p = jnp.exp(s - jnp.max(s, axis=-1, keepdims=True))
        l = jnp.sum(p, axis=-1, keepdims=True)
        heads.append((_dot(p.astype(BF16), v_ref[:, sl]) / l).astype(BF16))
    o_ref[...] = x + _dot(jnp.concatenate(heads, axis=1), wo_ref[...])


def _mix_out_cross_attention(x, left, right, w_mix_out, gain, w_q, q_gain, k, v, w_o, bsz, seq):
    t, d = x.shape
    tq = TOKEN_TILE
    n_s = seq // tq
    n_mem = k.shape[1]
    tok = lambda width: pl.BlockSpec((tq, width), lambda b, s: (b * n_s + s, 0))
    mem_spec = pl.BlockSpec((None, n_mem, d), lambda b, s: (b, 0, 0))
    w_mix = w_mix_out.astype(BF16)
    return pl.pallas_call(
        _cross_kernel,
        out_shape=jax.ShapeDtypeStruct((t, d), F32),
        grid=(bsz, n_s),
        in_specs=[tok(d), tok(MIX_W), tok(MIX_W), _const_spec((MIX_W, d)), _const_spec((MIX_W, d)),
                  _const_spec((1, d)), _const_spec((d, d)), _const_spec((1, X_HEAD_DIM)),
                  mem_spec, mem_spec, _const_spec((d, d))],
        out_specs=tok(d),
        compiler_params=_params(2),
        name="cross_attention",
    )(x, left, right, w_mix[:MIX_W], w_mix[MIX_W:], gain.reshape(1, d), w_q.astype(BF16),
      q_gain.reshape(1, X_HEAD_DIM), k, v, w_o.astype(BF16))


def kernel(x, mem, positions, ln_gains, ffn1_w_in, ffn1_w_out, ffn2_w_in, ffn2_w_out, even_w_in, even_f_bias, even_qk_gains, even_lambda, even_subln_gain, even_w_out, odd_w_in, odd_conv_w, odd_qk_gains, odd_rel_bias, odd_w_out, x_w_q, x_w_kv, x_qk_gains, x_w_o):
    bsz, seq, d = x.shape
    depth = ln_gains.shape[0]
    rope = _rope_tables(positions)
    x = x.reshape(bsz * seq, d)
    for layer in range(depth):
        g = ln_gains[layer]
        x = _ffn(x, g[0], ffn1_w_in[layer], ffn1_w_out[layer])
        if layer % 2 == 0:
            e = layer // 2
            lambda_init = 0.8 - 0.6 * math.exp(-0.3 * layer)
            aq, ak, avt, bq, bk, bvt = _even_prep(
                x, g[1], even_w_in[e], even_f_bias[e], even_qk_gains[e], rope, bsz, seq)
            left = _fox_attention(aq, ak, avt, bsz, seq)
            right = _diff_attention(bq, bk, bvt, even_lambda[e], even_subln_gain[e], lambda_init,
                                    bsz, seq)
            w_mix_out = even_w_out[e]
        else:
            o = layer // 2
            left, dq, dk, dvt = _odd_prep(x, g[1], odd_w_in[o], odd_conv_w[o], odd_qk_gains[o],
                                          bsz, seq)
            right = _band_attention(dq, dk, dvt, odd_rel_bias[o], bsz, seq)
            w_mix_out = odd_w_out[o]
        mk, mv = _mem_kv(mem, g[3], x_w_kv[layer], x_qk_gains[layer, 1])
        x = _mix_out_cross_attention(x, left, right, w_mix_out, g[2], x_w_q[layer],
                                     x_qk_gains[layer, 0], mk, mv, x_w_o[layer], bsz, seq)
        x = _ffn(x, g[4], ffn2_w_in[layer], ffn2_w_out[layer])
    return x.reshape(bsz, seq, d)
```

```python
import functools
import math

import jax
import jax.numpy as jnp
from jax import lax
from jax.experimental import pallas as pl
from jax.experimental.pallas import tpu as pltpu

D_MODEL = 1024
CHUNK = 64
HEAD_DIM = 64
ROPE_DIM = HEAD_DIM // 4
ROPE_THETA = 500000.0
RMS_EPS = 1e-6
D_FF = 2816
A_HEADS = 8
MIX_W = 512
LANES = 128
D_LEFT = 8 * CHUNK
REL_CLIP = 128
X_HEADS = 4
X_HEAD_DIM = D_MODEL // X_HEADS
NEG = -1e30
LOG2E = math.log2(math.e)
Q_SCALE = HEAD_DIM ** -0.5 * LOG2E

VMEM_LIMIT = 56 * 1024 * 1024
FFN_CHUNKS = (768, 768, 768, 512)
TOKEN_TILE = 512
ATT_TILE = 256

BF16 = jnp.bfloat16
F32 = jnp.float32


def _params(n_axes):
    return pltpu.CompilerParams(dimension_semantics=("arbitrary",) * n_axes,
                                vmem_limit_bytes=VMEM_LIMIT)


def _const_spec(shape):
    nd = len(shape)
    return pl.BlockSpec(shape, lambda *_: (0,) * nd, pipeline_mode=pl.Buffered(1))


def _rms(x):
    return x * lax.rsqrt(jnp.mean(x * x, axis=-1, keepdims=True) + RMS_EPS)


def _dot(a, b):
    return jnp.dot(a, b, preferred_element_type=F32)


def _dot_nt(a, b):
    return lax.dot_general(a, b, (((1,), (1,)), ((), ())), preferred_element_type=F32)


def _ffn_kernel(x_ref, g_ref, wg_ref, wu_ref, wo_ref, o_ref):
    x = x_ref[...]
    xn = (_rms(x) * g_ref[...]).astype(BF16)
    y = jnp.zeros_like(x)
    start = 0
    for width in FFN_CHUNKS:
        gate = _dot(xn, wg_ref[:, start:start + width])
        up = _dot(xn, wu_ref[:, start:start + width])
        act = (gate * (1.0 / (1.0 + jnp.exp(-gate))) * up).astype(BF16)
        y = y + _dot(act, wo_ref[start:start + width, :])
        start += width
    o_ref[...] = x + 0.5 * y


def _ffn(x, gain, w_in, w_out):
    t, d = x.shape
    tm = TOKEN_TILE
    wg = w_in[:, :D_FF].astype(BF16)
    wu = w_in[:, D_FF:].astype(BF16)
    wo = w_out.astype(BF16)
    return pl.pallas_call(
        _ffn_kernel,
        out_shape=jax.ShapeDtypeStruct((t, d), F32),
        grid=(t // tm,),
        in_specs=[pl.BlockSpec((tm, d), lambda i: (i, 0)),
                  _const_spec((1, d)),
                  _const_spec((d, D_FF)), _const_spec((d, D_FF)), _const_spec((D_FF, d))],
        out_specs=pl.BlockSpec((tm, d), lambda i: (i, 0)),
        compiler_params=_params(1),
        name="ffn",
    )(x, gain.reshape(1, d), wg, wu, wo)


BD_WIDTH = 256


def _head_norm(y, ones_bd, gain):
    sq = (y * y).astype(BF16)
    msq = jnp.concatenate([_dot(sq[:, c:c + BD_WIDTH], ones_bd) for c in range(0, MIX_W, BD_WIDTH)],
                          axis=1) * (1.0 / HEAD_DIM)
    return y * lax.rsqrt(msq + RMS_EPS) * gain


def _block_diag_ones():
    r = jnp.arange(BD_WIDTH) // HEAD_DIM
    return (r[:, None] == r[None, :]).astype(BF16)


def _tile_heads(v):
    return jnp.tile(v, (1, MIX_W // HEAD_DIM))


def _store_transposed(vt_ref, v):
    for i in range(v.shape[0] // ATT_TILE):
        vt_ref[i] = v[i * ATT_TILE:(i + 1) * ATT_TILE, :].T.astype(BF16)


def _split3(f):
    hi = f.astype(BF16).astype(F32)
    rest = f - hi
    mid = rest.astype(BF16).astype(F32)
    return hi, mid, rest - mid


def _even_prep_kernel(x_ref, g_ref, wa_ref, wf_ref, fb_ref, wb_ref, gains_ref, bd_ref,
                      ra_ref, rm_ref, rp_ref,
                      aq_ref, ak_ref, avt_ref, bq_ref, bk_ref, bvt_ref, carry_ref):
    tm = x_ref.shape[0]

    @pl.when(pl.program_id(1) == 0)
    def _():
        carry_ref[...] = jnp.zeros_like(carry_ref)

    h = (_rms(x_ref[...]) * g_ref[...]).astype(BF16)
    bd = bd_ref[...]
    gains = gains_ref[...]

    def project(w_ref, part):
        return _dot(h, w_ref[:, part * MIX_W:(part + 1) * MIX_W])

    qn = _head_norm(project(wa_ref, 0), bd, gains[0:1]) * Q_SCALE
    kn = _head_norm(project(wa_ref, 1), bd, gains[1:2])
    _store_transposed(avt_ref, project(wa_ref, 2))

    z = _dot(h, wf_ref[...]) + fb_ref[...]
    logf = jnp.minimum(z, 0.0) - jnp.log1p(jnp.exp(-jnp.abs(z)))
    row = lax.broadcasted_iota(jnp.int32, logf.shape, 0)
    step = 1
    while step < tm:
        logf = logf + jnp.where(row >= step, pltpu.roll(logf, step, axis=0), 0.0)
        step *= 2
    cum = logf + carry_ref[...]
    carry_ref[...] = cum[tm - 1:tm, :]

    hi, mid, lo = _split3(cum * LOG2E)
    lane = lax.broadcasted_iota(jnp.int32, (tm, LANES), 1)
    is_hi = (lane == HEAD_DIM) | (lane == HEAD_DIM + 3)
    is_mid = (lane == HEAD_DIM + 1) | (lane == HEAD_DIM + 4)
    ones_hi = jnp.where(lane < HEAD_DIM + 6, 1.0, 0.0)
    for hd in range(A_HEADS):
        pieces = jnp.where(is_hi, hi[:, hd:hd + 1],
                           jnp.where(is_mid, mid[:, hd:hd + 1], lo[:, hd:hd + 1]))
        blk = slice((hd // 2) * LANES, (hd // 2 + 1) * LANES)
        sq, sk = qn[:, blk], kn[:, blk]
        if hd % 2:
            sq, sk = pltpu.roll(sq, HEAD_DIM, axis=1), pltpu.roll(sk, HEAD_DIM, axis=1)
        qa = jnp.where(lane < HEAD_DIM, sq, jnp.where(lane < HEAD_DIM + 3, pieces, ones_hi))
        ka = jnp.where(lane < HEAD_DIM, sk, jnp.where(lane < HEAD_DIM + 3, 1.0,
                                                      jnp.where(lane < HEAD_DIM + 6, -pieces, 0.0)))
        aq_ref[:, hd * LANES:(hd + 1) * LANES] = qa.astype(BF16)
        ak_ref[:, hd * LANES:(hd + 1) * LANES] = ka.astype(BF16)

    rep = MIX_W // LANES
    ra = jnp.tile(ra_ref[...], (1, rep))
    rm = jnp.tile(rm_ref[...], (1, rep))
    rp = jnp.tile(rp_ref[...], (1, rep))

    def rope(v):
        half = ROPE_DIM // 2
        return v * ra + pltpu.roll(v, MIX_W - half, axis=1) * rm + pltpu.roll(v, half, axis=1) * rp

    bq = rope(_head_norm(project(wb_ref, 0), bd, gains[2:3]))
    bk = rope(_head_norm(project(wb_ref, 1), bd, gains[3:4]))
    bq_ref[...] = (bq * Q_SCALE).astype(BF16)
    bk_ref[...] = bk.astype(BF16)
    _store_transposed(bvt_ref, project(wb_ref, 2))


def _rope_tables(positions):
    half = ROPE_DIM // 2
    inv = ROPE_THETA ** (-jnp.arange(0, ROPE_DIM, 2, dtype=F32) / ROPE_DIM)
    ang = positions.astype(F32).reshape(-1, 1) * inv
    cos = jnp.tile(jnp.cos(ang), (1, LANES // half))
    sin = jnp.tile(jnp.sin(ang), (1, LANES // half))
    in_head = jnp.arange(LANES) % HEAD_DIM
    ra = jnp.where(in_head < ROPE_DIM, cos, 1.0)
    rm = jnp.where(in_head < half, -sin, 0.0)
    rp = jnp.where((in_head >= half) & (in_head < ROPE_DIM), sin, 0.0)
    return ra, rm, rp


def _even_prep(x, gain, w_in, f_bias, qk_gains, rope, bsz, seq):
    t, d = x.shape
    tm = TOKEN_TILE
    a_w = 3 * MIX_W
    wa = w_in[:, :a_w].astype(BF16)
    wf = jnp.pad(w_in[:, a_w:a_w + A_HEADS], ((0, 0), (0, LANES - A_HEADS))).astype(BF16)
    wb = w_in[:, a_w + A_HEADS:].astype(BF16)
    fb = jnp.pad(f_bias, (0, LANES - A_HEADS)).reshape(1, LANES)
    n_s = seq // tm
    tok = lambda w: pl.BlockSpec((tm, w), lambda b, s: (b * n_s + s, 0))
    vt_spec = pl.BlockSpec((None, tm // ATT_TILE, MIX_W, ATT_TILE), lambda b, s: (b, s, 0, 0))
    stream = lambda w: jax.ShapeDtypeStruct((t, w), BF16)
    vt_shape = jax.ShapeDtypeStruct((bsz, seq // ATT_TILE, MIX_W, ATT_TILE), BF16)
    return pl.pallas_call(
        _even_prep_kernel,
        out_shape=[stream(A_HEADS * LANES), stream(A_HEADS * LANES), vt_shape,
                   stream(MIX_W), stream(MIX_W), vt_shape],
        grid=(bsz, n_s),
        in_specs=[tok(d), _const_spec((1, d)), _const_spec((d, a_w)), _const_spec((d, LANES)),
                  _const_spec((1, LANES)), _const_spec((d, a_w)), _const_spec((4, MIX_W)),
                  _const_spec((BD_WIDTH, BD_WIDTH)), tok(LANES), tok(LANES), tok(LANES)],
        out_specs=[tok(A_HEADS * LANES), tok(A_HEADS * LANES), vt_spec,
                   tok(MIX_W), tok(MIX_W), vt_spec],
        scratch_shapes=[pltpu.VMEM((1, LANES), F32)],
        compiler_params=_params(2),
        name="even_prep",
    )(x, gain.reshape(1, d), wa, wf, fb, wb, _tile_heads(qk_gains), _block_diag_ones(), *rope)


def _lane_lo(shape):
    return lax.broadcasted_iota(jnp.int32, shape, len(shape) - 1) < HEAD_DIM


def _key_rows(g):
    if isinstance(g, int):
        return slice(g * ATT_TILE, (g + 1) * ATT_TILE)
    return pl.ds(pl.multiple_of(g * ATT_TILE, ATT_TILE), ATT_TILE)


SUM_ROWS = 16
MXU_LOOKAHEAD = 5


def _with_sum_rows(value_t):
    return jnp.concatenate([value_t, jnp.ones((SUM_ROWS, value_t.shape[1]), BF16)], axis=0)


def _normalise(acc, v_rows):
    return acc[:v_rows] / acc[v_rows:v_rows + 1]


def _paired_causal_attention(a, n_tiles, n_sub, queries, key_block, value_t, diag_visible, v_rows):
    tq = ATT_TILE
    tiles = (a, n_tiles - 1 - a)
    items = [(t, g, sub) for t, tile in enumerate(tiles) for g in range(tile + 1)
             for sub in range(n_sub)]
    maxes = [[jnp.full((1, tq), NEG, F32)] * n_sub for _ in tiles]
    accs = [[jnp.zeros((v_rows + SUM_ROWS, tq), F32)] * n_sub for _ in tiles]

    def scores(t, g, sub):
        s = _dot_nt(key_block(g, sub), queries[t][sub])
        return jnp.where(diag_visible, s, NEG) if g == tiles[t] else s

    def exponentials(t, g, sub, s):
        m_old = maxes[t][sub]
        m_new = jnp.maximum(m_old, jnp.max(s, axis=0, keepdims=True))
        maxes[t][sub] = m_new
        return jnp.exp2(s - m_new).astype(BF16), jnp.exp2(m_old - m_new)

    def accumulate(t, g, sub, p, alpha):
        accs[t][sub] = alpha * accs[t][sub] + _dot(_with_sum_rows(value_t(g, sub)), p)

    in_flight = []
    for item in items:
        in_flight.append((item, exponentials(*item, scores(*item))))
        if len(in_flight) > MXU_LOOKAHEAD:
            done, (p, alpha) = in_flight.pop(0)
            accumulate(*done, p, alpha)
    for done, (p, alpha) in in_flight:
        accumulate(*done, p, alpha)
    return accs


def _for_each_tile_pair(n_tiles, body):
    for a in range(n_tiles // 2):
        pl.when(pl.program_id(2) == a)(functools.partial(body, a))


HEADS_PER_STEP = 4


def _paired_specs(n_tiles, q_width, out_width):
    tq = ATT_TILE
    half = n_tiles // 2
    q_a = pl.BlockSpec((None, tq, q_width), lambda b, p, a: (b, a, p))
    q_b = pl.BlockSpec((None, tq, q_width), lambda b, p, a: (b, n_tiles - 1 - a, p))
    o_a = pl.BlockSpec((None, tq, out_width), lambda b, p, a: (b, a, p))
    o_b = pl.BlockSpec((None, tq, out_width), lambda b, p, a: (b, half - 1 - a, p))
    return q_a, q_b, o_a, o_b


def _fox_kernel(qa_ref, qb_ref, k_ref, vt_ref, oa_ref, ob_ref, *, n_tiles):
    tq = ATT_TILE

    def key_block(g, sub):
        return k_ref[_key_rows(g), sub * LANES:(sub + 1) * LANES]

    def value_t(g, sub):
        return vt_ref[g, sub * HEAD_DIM:(sub + 1) * HEAD_DIM, :]

    def body(a):
        queries = [[q_ref[:, sub * LANES:(sub + 1) * LANES] for sub in range(HEADS_PER_STEP)]
                   for q_ref in (qa_ref, qb_ref)]
        causal = (lax.broadcasted_iota(jnp.int32, (tq, tq), 0)
                  <= lax.broadcasted_iota(jnp.int32, (tq, tq), 1))
        res = _paired_causal_attention(a, n_tiles, HEADS_PER_STEP, queries, key_block, value_t,
                                       causal, HEAD_DIM)
        for accs, o_ref in zip(res, (oa_ref, ob_ref)):
            o_t = jnp.concatenate([_normalise(acc, HEAD_DIM) for acc in accs], axis=0)
            o_ref[...] = o_t.T.astype(o_ref.dtype)

    _for_each_tile_pair(n_tiles, body)


def _fox_attention(q, k, vt, bsz, seq):
    tq = ATT_TILE
    n_tiles = seq // tq
    q3 = q.reshape(bsz, seq, A_HEADS * LANES)
    k3 = k.reshape(bsz, seq, A_HEADS * LANES)
    out_width = HEADS_PER_STEP * HEAD_DIM
    q_a, q_b, o_a, o_b = _paired_specs(n_tiles, HEADS_PER_STEP * LANES, out_width)
    half_out = jax.ShapeDtypeStruct((bsz, seq // 2, MIX_W), BF16)
    out_a, out_b = pl.pallas_call(
        functools.partial(_fox_kernel, n_tiles=n_tiles),
        out_shape=[half_out, half_out],
        grid=(bsz, A_HEADS // HEADS_PER_STEP, n_tiles // 2),
        in_specs=[q_a, q_b,
                  pl.BlockSpec((None, seq, HEADS_PER_STEP * LANES), lambda b, p, a: (b, 0, p)),
                  pl.BlockSpec((None, n_tiles, out_width, tq), lambda b, p, a: (b, 0, p, 0))],
        out_specs=[o_a, o_b],
        compiler_params=_params(3),
        name="fox_attention",
    )(q3, q3, k3, vt)
    return out_a, out_b


def _diff_kernel(qa_ref, qb_ref, k_ref, vt_ref, lam_ref, sg_ref, oa_ref, ob_ref, *,
                 lambda_init, n_tiles):
    tq = ATT_TILE
    lo = _lane_lo((tq, LANES))
    lp = lam_ref[...]
    lam = (jnp.exp(jnp.sum(lp[0:1] * lp[1:2], axis=1, keepdims=True))
           - jnp.exp(jnp.sum(lp[2:3] * lp[3:4], axis=1, keepdims=True)) + lambda_init)

    def half_of(q_ref, sub):
        q = q_ref[:, sub // 2 * LANES:(sub // 2 + 1) * LANES]
        zero = jnp.zeros_like(q)
        return jnp.where(lo, q, zero) if sub % 2 == 0 else jnp.where(lo, zero, q)

    def key_block(g, sub):
        return k_ref[_key_rows(g), sub // 2 * LANES:(sub // 2 + 1) * LANES]

    def value_t(g, sub):
        return vt_ref[g, sub // 2 * LANES:(sub // 2 + 1) * LANES, :]

    def body(a):
        queries = [[half_of(q_ref, sub) for sub in range(HEADS_PER_STEP)]
                   for q_ref in (qa_ref, qb_ref)]
        chunk_causal = (lax.broadcasted_iota(jnp.int32, (tq, tq), 0) // CHUNK
                        <= lax.broadcasted_iota(jnp.int32, (tq, tq), 1) // CHUNK)
        res = _paired_causal_attention(a, n_tiles, HEADS_PER_STEP, queries, key_block, value_t,
                                       chunk_causal, LANES)
        for accs, o_ref in zip(res, (oa_ref, ob_ref)):
            for hd in range(HEADS_PER_STEP // 2):
                o1, o2 = (_normalise(acc, LANES) for acc in accs[2 * hd:2 * hd + 2])
                o = (o1 - lam * o2).T
                o_ref[:, hd * LANES:(hd + 1) * LANES] = (
                    _rms(o) * sg_ref[...] * (1.0 - lambda_init)).astype(o_ref.dtype)

    _for_each_tile_pair(n_tiles, body)


def _diff_attention(q, k, vt, lam_params, subln_gain, lambda_init, bsz, seq):
    tq = ATT_TILE
    n_tiles = seq // tq
    q3, k3 = (a.reshape(bsz, seq, MIX_W) for a in (q, k))
    width = HEADS_PER_STEP // 2 * LANES
    q_a, q_b, o_a, o_b = _paired_specs(n_tiles, width, width)
    half_out = jax.ShapeDtypeStruct((bsz, seq // 2, MIX_W), BF16)
    out_a, out_b = pl.pallas_call(
        functools.partial(_diff_kernel, lambda_init=lambda_init, n_tiles=n_tiles),
        out_shape=[half_out, half_out],
        grid=(bsz, MIX_W // width, n_tiles // 2),
        in_specs=[q_a, q_b,
                  pl.BlockSpec((None, seq, width), lambda b, h, a: (b, 0, h)),
                  pl.BlockSpec((None, n_tiles, width, tq), lambda b, h, a: (b, 0, h, 0)),
                  _const_spec((4, HEAD_DIM)), _const_spec((1, LANES))],
        out_specs=[o_a, o_b],
        compiler_params=_params(3),
        name="diff_attention",
    )(q3, q3, k3, vt, lam_params, subln_gain.reshape(1, LANES))
    return out_a, out_b


def _odd_prep_kernel(x_ref, g_ref, wc_ref, wd_ref, cw_ref, gains_ref, bd_ref,
                     c_ref, dq_ref, dk_ref, dvt_ref, carry_ref):
    tm = x_ref.shape[0]

    @pl.when(pl.program_id(1) == 0)
    def _():
        carry_ref[...] = jnp.zeros_like(carry_ref)

    h = (_rms(x_ref[...]) * g_ref[...]).astype(BF16)
    def project(w_ref, part):
        return _dot(h, w_ref[:, part * MIX_W:(part + 1) * MIX_W])

    u = project(wc_ref, 1) * project(wc_ref, 2)
    prev = carry_ref[...]
    carry_ref[...] = u[tm - 8:, :]
    row = lax.broadcasted_iota(jnp.int32, u.shape, 0)
    u1 = jnp.where(row == 0, prev[7:8], pltpu.roll(u, 1, axis=0))
    u2 = jnp.where(row == 0, prev[6:7], jnp.where(row == 1, prev[7:8], pltpu.roll(u, 2, axis=0)))
    cw = cw_ref[...]
    conv = cw[0:1] * u2 + cw[1:2] * u1 + cw[2:3] * u
    c_ref[...] = (project(wc_ref, 0) * conv).astype(BF16)

    bd = bd_ref[...]
    gains = gains_ref[...]
    dq_ref[...] = (_head_norm(project(wd_ref, 0), bd, gains[0:1]) * Q_SCALE).astype(BF16)
    dk_ref[...] = _head_norm(project(wd_ref, 1), bd, gains[1:2]).astype(BF16)
    _store_transposed(dvt_ref, project(wd_ref, 2))


def _odd_prep(x, gain, w_in, conv_w, qk_gains, bsz, seq):
    t, d = x.shape
    tm = TOKEN_TILE
    w3 = 3 * MIX_W
    n_s = seq // tm
    tok = lambda w: pl.BlockSpec((tm, w), lambda b, s: (b * n_s + s, 0))
    stream = jax.ShapeDtypeStruct((t, MIX_W), BF16)
    return pl.pallas_call(
        _odd_prep_kernel,
        out_shape=[stream] * 3 + [jax.ShapeDtypeStruct((bsz, seq // ATT_TILE, MIX_W, ATT_TILE), BF16)],
        grid=(bsz, n_s),
        in_specs=[tok(d), _const_spec((1, d)), _const_spec((d, w3)), _const_spec((d, w3)),
                  _const_spec((3, MIX_W)), _const_spec((2, MIX_W)), _const_spec((BD_WIDTH, BD_WIDTH))],
        out_specs=[tok(MIX_W)] * 3 + [pl.BlockSpec((None, tm // ATT_TILE, MIX_W, ATT_TILE),
                                                   lambda b, s: (b, s, 0, 0))],
        scratch_shapes=[pltpu.VMEM((8, MIX_W), F32)],
        compiler_params=_params(2),
        name="odd_prep",
    )(x, gain.reshape(1, d), w_in[:, :w3].astype(BF16), w_in[:, w3:].astype(BF16), conv_w,
      _tile_heads(qk_gains), _block_diag_ones())


BAND_GROUPS = 1 + D_LEFT // ATT_TILE


MAX_BAND_TILES = 8
BAND_WIDTH = BAND_GROUPS * ATT_TILE
BIAS_ROW = BAND_WIDTH + ATT_TILE


def _band_kernel(q_ref, k_ref, vt_ref, w_ref, o_ref, bias_ref):
    tq = ATT_TILE
    BAND_TILES = q_ref.shape[0] // tq
    j = pl.program_id(2)

    @pl.when((pl.program_id(1) == 0) & (j == 0))
    def _():
        key = lax.broadcasted_iota(jnp.int32, (BAND_WIDTH, tq), 0)
        chunk_start = lax.broadcasted_iota(jnp.int32, (BAND_WIDTH, tq), 1) // CHUNK * CHUNK
        in_band = (key >= chunk_start) & (key < chunk_start + D_LEFT + CHUNK)
        for sub in range(2):
            rows = jnp.broadcast_to(w_ref[sub], (BAND_WIDTH, BIAS_ROW))
            skew = pltpu.roll(rows, tq + 1, axis=1, stride=1, stride_axis=0)
            table = jnp.where(in_band, skew[:, :tq] * LOG2E, NEG)
            for grp in range(BAND_GROUPS):
                bias_ref[sub, grp] = table[grp * tq:(grp + 1) * tq, :]

    lo = _lane_lo((tq, LANES))
    before_start = jnp.where(j == 0, NEG, 0.0)
    q_subs, items = {}, []
    for tile in range(BAND_TILES):
        q = q_ref[tile * tq:(tile + 1) * tq, :]
        zero = jnp.zeros_like(q)
        q_subs[tile] = (jnp.where(lo, q, zero), jnp.where(lo, zero, q))
        items += [(tile, grp, sub) for grp in range(BAND_GROUPS) for sub in range(2)]

    def may_precede(tile, grp):
        return tile + grp < BAND_GROUPS - 1

    def group_index(tile, grp):
        g = BAND_TILES * j + tile + grp - (BAND_GROUPS - 1)
        return jnp.maximum(g, 0) if may_precede(tile, grp) else g

    def scores(tile, grp, sub):
        s = _dot_nt(k_ref[_key_rows(group_index(tile, grp)), :], q_subs[tile][sub]) + bias_ref[sub, grp]
        return s + before_start if may_precede(tile, grp) else s

    maxes, accs = {}, {}

    def exponentials(tile, grp, sub, s):
        mx = jnp.max(s, axis=0, keepdims=True)
        m_old = maxes.get((tile, sub))
        m_new = mx if m_old is None else jnp.maximum(m_old, mx)
        maxes[tile, sub] = m_new
        return jnp.exp2(s - m_new).astype(BF16), None if m_old is None else jnp.exp2(m_old - m_new)

    def accumulate(tile, grp, sub, p, alpha):
        value_t = _with_sum_rows(vt_ref[group_index(tile, grp), sub * HEAD_DIM:(sub + 1) * HEAD_DIM, :])
        acc_n = _dot(value_t, p)
        accs[tile, sub] = acc_n if alpha is None else alpha * accs[tile, sub] + acc_n

    in_flight = []
    for item in items:
        in_flight.append((item, exponentials(*item, scores(*item))))
        if len(in_flight) > MXU_LOOKAHEAD:
            done, (p, alpha) = in_flight.pop(0)
            accumulate(*done, p, alpha)
    for done, (p, alpha) in in_flight:
        accumulate(*done, p, alpha)
    for tile in range(BAND_TILES):
        o_t = jnp.concatenate([_normalise(accs[tile, sub], HEAD_DIM) for sub in range(2)], axis=0)
        o_ref[tile * tq:(tile + 1) * tq, :] = o_t.T.astype(o_ref.dtype)


def _band_bias_rows(rel_table):
    n_lo = ATT_TILE - 1 - REL_CLIP
    n_hi = BIAS_ROW - n_lo - (2 * REL_CLIP + 1)
    w = jnp.concatenate([jnp.repeat(rel_table[:, :1], n_lo, axis=1), rel_table,
                         jnp.repeat(rel_table[:, -1:], n_hi, axis=1)], axis=1)
    return w.astype(F32).reshape(rel_table.shape[0], 1, BIAS_ROW)


def _band_attention(q, k, vt, rel_table, bsz, seq):
    tq = ATT_TILE
    n_q = seq // tq
    BAND_TILES = min(MAX_BAND_TILES, n_q)
    q3, k3 = (a.reshape(bsz, seq, MIX_W) for a in (q, k))
    q_spec = pl.BlockSpec((None, BAND_TILES * tq, LANES), lambda p, b, j: (b, j, p))
    out = pl.pallas_call(
        _band_kernel,
        out_shape=jax.ShapeDtypeStruct((bsz, seq, MIX_W), BF16),
        grid=(MIX_W // LANES, bsz, n_q // BAND_TILES),
        in_specs=[q_spec,
                  pl.BlockSpec((None, seq, LANES), lambda p, b, j: (b, 0, p)),
                  pl.BlockSpec((None, n_q, LANES, tq), lambda p, b, j: (b, 0, p, 0)),
                  pl.BlockSpec((2, 1, BIAS_ROW), lambda p, b, j: (p, 0, 0))],
        out_specs=q_spec,
        scratch_shapes=[pltpu.VMEM((2, BAND_GROUPS, tq, tq), F32)],
        compiler_params=_params(3),
        name="band_attention",
    )(q3, k3, vt, _band_bias_rows(rel_table))
    return out.reshape(bsz * seq, MIX_W)


def _mem_kv_kernel(mem_ref, g_ref, w_ref, kg_ref, k_ref, v_ref):
    mem_n = (_rms(mem_ref[...]) * g_ref[...]).astype(BF16)
    kv = _dot(mem_n, w_ref[...])
    kg = kg_ref[...]
    for hd in range(X_HEADS):
        sl = slice(hd * X_HEAD_DIM, (hd + 1) * X_HEAD_DIM)
        k_ref[:, sl] = (_rms(kv[:, sl]) * kg).astype(BF16)
    v_ref[...] = kv[:, D_MODEL:].astype(BF16)


def _mem_kv(mem, gain, w_kv, k_gain):
    bsz, n_mem, d = mem.shape
    blk = pl.BlockSpec((None, n_mem, d), lambda b: (b, 0, 0))
    out = jax.ShapeDtypeStruct((bsz, n_mem, d), BF16)
    return pl.pallas_call(
        _mem_kv_kernel,
        out_shape=[out, out],
        grid=(bsz,),
        in_specs=[blk, _const_spec((1, d)), _const_spec((d, 2 * d)), _const_spec((1, X_HEAD_DIM))],
        out_specs=[blk, blk],
        compiler_params=_params(1),
        name="mem_kv",
    )(mem, gain.reshape(1, d), w_kv.astype(BF16), k_gain.reshape(1, X_HEAD_DIM))


def _cross_kernel(x_ref, left_a_ref, left_b_ref, right_a_ref, right_b_ref, wl_ref, wr_ref, g_ref,
                  wq_ref, qg_ref, k_ref, v_ref, wo_ref, o_ref):
    first_half = pl.program_id(1) < pl.num_programs(1) // 2
    left = jnp.where(first_half, left_a_ref[...], left_b_ref[...])
    right = jnp.where(first_half, right_a_ref[...], right_b_ref[...])
    x = x_ref[...] + _dot(left, wl_ref[...]) + _dot(right, wr_ref[...])
    h = (_rms(x) * g_ref[...]).astype(BF16)
    q = _dot(h, wq_ref[...])
    qg = qg_ref[...]
    heads = []
    for hd in range(X_HEADS):
        sl = slice(hd * X_HEAD_DIM, (hd + 1) * X_HEAD_DIM)
        qh = (_rms(q[:, sl]) * qg * X_HEAD_DIM ** -0.5).astype(BF16)
        s = _dot_nt(qh, k_ref[:, sl])
        p = jnp.exp(s - jnp.max(s, axis=-1, keepdims=True))
        l = jnp.sum(p, axis=-1, keepdims=True)
        heads.append((_dot(p.astype(BF16), v_ref[:, sl]) / l).astype(BF16))
    o_ref[...] = x + _dot(jnp.concatenate(heads, axis=1), wo_ref[...])


def _halves(stream, bsz, seq):
    if isinstance(stream, tuple):
        return stream[0], stream[1], 0
    full = stream.reshape(bsz, seq, MIX_W)
    return full, full, seq // TOKEN_TILE // 2


def _mix_out_cross_attention(x, left, right, w_mix_out, gain, w_q, q_gain, k, v, w_o, bsz, seq):
    t, d = x.shape
    tq = TOKEN_TILE
    n_s = seq // tq
    half = n_s // 2
    n_mem = k.shape[1]
    tok = pl.BlockSpec((tq, d), lambda b, s: (b * n_s + s, 0))
    mem_spec = pl.BlockSpec((None, n_mem, d), lambda b, s: (b, 0, 0))
    w_mix = w_mix_out.astype(BF16)
    stream_args, stream_specs = [], []
    for first, second, offset in (_halves(left, bsz, seq), _halves(right, bsz, seq)):
        stream_args += [first, second]
        stream_specs += [
            pl.BlockSpec((None, tq, MIX_W), lambda b, s: (b, jnp.minimum(s, half - 1), 0)),
            pl.BlockSpec((None, tq, MIX_W),
                         lambda b, s, offset=offset: (b, offset + jnp.maximum(s - half, 0), 0))]
    return pl.pallas_call(
        _cross_kernel,
        out_shape=jax.ShapeDtypeStruct((t, d), F32),
        grid=(bsz, n_s),
        in_specs=[tok, *stream_specs, _const_spec((MIX_W, d)), _const_spec((MIX_W, d)),
                  _const_spec((1, d)), _const_spec((d, d)), _const_spec((1, X_HEAD_DIM)),
                  mem_spec, mem_spec, _const_spec((d, d))],
        out_specs=tok,
        compiler_params=_params(2),
        name="cross_attention",
    )(x, *stream_args, w_mix[:MIX_W], w_mix[MIX_W:], gain.reshape(1, d), w_q.astype(BF16),
      q_gain.reshape(1, X_HEAD_DIM), k, v, w_o.astype(BF16))


def kernel(x, mem, positions, ln_gains, ffn1_w_in, ffn1_w_out, ffn2_w_in, ffn2_w_out, even_w_in, even_f_bias, even_qk_gains, even_lambda, even_subln_gain, even_w_out, odd_w_in, odd_conv_w, odd_qk_gains, odd_rel_bias, odd_w_out, x_w_q, x_w_kv, x_qk_gains, x_w_o):
    bsz, seq, d = x.shape
    depth = ln_gains.shape[0]
    rope = _rope_tables(positions)
    x = x.reshape(bsz * seq, d)
    for layer in range(depth):
        g = ln_gains[layer]
        x = _ffn(x, g[0], ffn1_w_in[layer], ffn1_w_out[layer])
        if layer % 2 == 0:
            e = layer // 2
            lambda_init = 0.8 - 0.6 * math.exp(-0.3 * layer)
            aq, ak, avt, bq, bk, bvt = _even_prep(
                x, g[1], even_w_in[e], even_f_bias[e], even_qk_gains[e], rope, bsz, seq)
            left = _fox_attention(aq, ak, avt, bsz, seq)
            right = _diff_attention(bq, bk, bvt, even_lambda[e], even_subln_gain[e], lambda_init,
                                    bsz, seq)
            w_mix_out = even_w_out[e]
        else:
            o = layer // 2
            left, dq, dk, dvt = _odd_prep(x, g[1], odd_w_in[o], odd_conv_w[o], odd_qk_gains[o],
                                          bsz, seq)
            right = _band_attention(dq, dk, dvt, odd_rel_bias[o], bsz, seq)
            w_mix_out = odd_w_out[o]
        mk, mv = _mem_kv(mem, g[3], x_w_kv[layer], x_qk_gains[layer, 1])
        x = _mix_out_cross_attention(x, left, right, w_mix_out, g[2], x_w_q[layer],
                                     x_qk_gains[layer, 0], mk, mv, x_w_o[layer], bsz, seq)
        x = _ffn(x, g[4], ffn2_w_in[layer], ffn2_w_out[layer])
    return x.reshape(bsz, seq, d)
```

```python
import functools
import math

import jax
import jax.numpy as jnp
from jax import lax
from jax.experimental import pallas as pl
from jax.experimental.pallas import tpu as pltpu

D_MODEL = 1024
CHUNK = 64
HEAD_DIM = 64
ROPE_DIM = HEAD_DIM // 4
ROPE_THETA = 500000.0
RMS_EPS = 1e-6
D_FF = 2816
A_HEADS = 8
MIX_W = 512
LANES = 128
D_LEFT = 8 * CHUNK
REL_CLIP = 128
X_HEADS = 4
X_HEAD_DIM = D_MODEL // X_HEADS
NEG = -1e30
LOG2E = math.log2(math.e)
Q_SCALE = HEAD_DIM ** -0.5 * LOG2E

VMEM_LIMIT = 56 * 1024 * 1024
FFN_CHUNKS = (768, 768, 768, 512)
TOKEN_TILE = 512
ATT_TILE = 256

BF16 = jnp.bfloat16
F32 = jnp.float32


def _params(n_axes):
    return pltpu.CompilerParams(dimension_semantics=("arbitrary",) * n_axes,
                                vmem_limit_bytes=VMEM_LIMIT)


def _const_spec(shape):
    nd = len(shape)
    return pl.BlockSpec(shape, lambda *_: (0,) * nd, pipeline_mode=pl.Buffered(1))


def _rms(x):
    return x * lax.rsqrt(jnp.mean(x * x, axis=-1, keepdims=True) + RMS_EPS)


def _dot(a, b):
    return jnp.dot(a, b, preferred_element_type=F32)


def _dot_nt(a, b):
    return lax.dot_general(a, b, (((1,), (1,)), ((), ())), preferred_element_type=F32)


def _ffn_kernel(x_ref, g_ref, wg_ref, wu_ref, wo_ref, o_ref):
    x = x_ref[...]
    xn = (_rms(x) * g_ref[...]).astype(BF16)

    y = jnp.zeros_like(x)
    start = 0
    for width in FFN_CHUNKS:
        gate = _dot(xn, wg_ref[:, start:start + width])
        up = _dot(xn, wu_ref[:, start:start + width])
        act = (gate * (1.0 / (1.0 + jnp.exp(-gate))) * up).astype(BF16)
        y = y + _dot(act, wo_ref[start:start + width, :])
        start += width
    o_ref[...] = x + 0.5 * y


def _ffn(x, gain, w_in, w_out):
    t, d = x.shape
    tm = TOKEN_TILE
    wg = w_in[:, :D_FF].astype(BF16)
    wu = w_in[:, D_FF:].astype(BF16)
    wo = w_out.astype(BF16)
    return pl.pallas_call(
        _ffn_kernel,
        out_shape=jax.ShapeDtypeStruct((t, d), F32),
        grid=(t // tm,),
        in_specs=[pl.BlockSpec((tm, d), lambda i: (i, 0)),
                  _const_spec((1, d)),
                  _const_spec((d, D_FF)), _const_spec((d, D_FF)), _const_spec((D_FF, d))],
        out_specs=pl.BlockSpec((tm, d), lambda i: (i, 0)),
        compiler_params=_params(1),
        name="ffn",
    )(x, gain.reshape(1, d), wg, wu, wo)


BD_WIDTH = 256


def _head_sumsq(y, ones_bd):
    sq = (y * y).astype(BF16)
    return jnp.concatenate([_dot(sq[:, c:c + BD_WIDTH], ones_bd) for c in range(0, MIX_W, BD_WIDTH)],
                           axis=1)


def _head_norm(y, sumsq, gain):
    return y * lax.rsqrt(sumsq * (1.0 / HEAD_DIM) + RMS_EPS) * gain


def _block_diag_ones():
    r = jnp.arange(BD_WIDTH) // HEAD_DIM
    return (r[:, None] == r[None, :]).astype(BF16)


def _tile_heads(v):
    return jnp.tile(v, (1, MIX_W // HEAD_DIM))


def _store_transposed(vt_ref, v):
    for i in range(v.shape[0] // ATT_TILE):
        vt_ref[i] = v[i * ATT_TILE:(i + 1) * ATT_TILE, :].T.astype(BF16)


def _split3(f):
    hi = f.astype(BF16).astype(F32)
    rest = f - hi
    mid = rest.astype(BF16).astype(F32)
    return hi, mid, rest - mid


def _even_prep_kernel(x_ref, g_ref, wa_ref, wf_ref, fb_ref, wb_ref, gains_ref, bd_ref,
                      ra_ref, rm_ref, rp_ref,
                      aq_ref, ak_ref, avt_ref, bq_ref, bk_ref, bvt_ref, carry_ref):
    tm = x_ref.shape[0]

    @pl.when(pl.program_id(1) == 0)
    def _():
        carry_ref[...] = jnp.zeros_like(carry_ref)

    h = (_rms(x_ref[...]) * g_ref[...]).astype(BF16)
    bd = bd_ref[...]
    gains = gains_ref[...]

    def project(w_ref, part):
        return _dot(h, w_ref[:, part * MIX_W:(part + 1) * MIX_W])

    z = _dot(h, wf_ref[...]) + fb_ref[...]
    a_q = project(wa_ref, 0)

    logf = jnp.minimum(z, 0.0) - jnp.log1p(jnp.exp(-jnp.abs(z)))
    row = lax.broadcasted_iota(jnp.int32, logf.shape, 0)
    step = 1
    while step < tm:
        logf = logf + jnp.where(row >= step, pltpu.roll(logf, step, axis=0), 0.0)
        step *= 2
    cum = logf + carry_ref[...]
    carry_ref[...] = cum[tm - 1:tm, :]
    hi, mid, lo = _split3(cum * LOG2E)

    a_k = project(wa_ref, 1)
    a_q_ss = _head_sumsq(a_q, bd)
    b_q = project(wb_ref, 0)
    qn = _head_norm(a_q, a_q_ss, gains[0:1]) * Q_SCALE
    a_k_ss = _head_sumsq(a_k, bd)
    b_k = project(wb_ref, 1)
    kn = _head_norm(a_k, a_k_ss, gains[1:2])

    lane = lax.broadcasted_iota(jnp.int32, (tm, LANES), 1)
    is_hi = (lane == HEAD_DIM) | (lane == HEAD_DIM + 3)
    is_mid = (lane == HEAD_DIM + 1) | (lane == HEAD_DIM + 4)
    ones_hi = jnp.where(lane < HEAD_DIM + 6, 1.0, 0.0)
    for hd in range(A_HEADS):
        pieces = jnp.where(is_hi, hi[:, hd:hd + 1],
                           jnp.where(is_mid, mid[:, hd:hd + 1], lo[:, hd:hd + 1]))
        blk = slice((hd // 2) * LANES, (hd // 2 + 1) * LANES)
        sq, sk = qn[:, blk], kn[:, blk]
        if hd % 2:
            sq, sk = pltpu.roll(sq, HEAD_DIM, axis=1), pltpu.roll(sk, HEAD_DIM, axis=1)
        qa = jnp.where(lane < HEAD_DIM, sq, jnp.where(lane < HEAD_DIM + 3, pieces, ones_hi))
        ka = jnp.where(lane < HEAD_DIM, sk, jnp.where(lane < HEAD_DIM + 3, 1.0,
                                                      jnp.where(lane < HEAD_DIM + 6, -pieces, 0.0)))
        aq_ref[:, hd * LANES:(hd + 1) * LANES] = qa.astype(BF16)
        ak_ref[:, hd * LANES:(hd + 1) * LANES] = ka.astype(BF16)

    rep = MIX_W // LANES
    ra = jnp.tile(ra_ref[...], (1, rep))
    rm = jnp.tile(rm_ref[...], (1, rep))
    rp = jnp.tile(rp_ref[...], (1, rep))

    def rope(v):
        half = ROPE_DIM // 2
        return v * ra + pltpu.roll(v, MIX_W - half, axis=1) * rm + pltpu.roll(v, half, axis=1) * rp

    b_q_ss = _head_sumsq(b_q, bd)
    a_v = project(wa_ref, 2)
    bq_ref[...] = (rope(_head_norm(b_q, b_q_ss, gains[2:3])) * Q_SCALE).astype(BF16)
    b_k_ss = _head_sumsq(b_k, bd)
    _store_transposed(avt_ref, a_v)
    b_v = project(wb_ref, 2)
    bk_ref[...] = rope(_head_norm(b_k, b_k_ss, gains[3:4])).astype(BF16)
    _store_transposed(bvt_ref, b_v)


def _rope_tables(positions):
    half = ROPE_DIM // 2
    inv = ROPE_THETA ** (-jnp.arange(0, ROPE_DIM, 2, dtype=F32) / ROPE_DIM)
    ang = positions.astype(F32).reshape(-1, 1) * inv
    cos = jnp.tile(jnp.cos(ang), (1, LANES // half))
    sin = jnp.tile(jnp.sin(ang), (1, LANES // half))
    in_head = jnp.arange(LANES) % HEAD_DIM
    ra = jnp.where(in_head < ROPE_DIM, cos, 1.0)
    rm = jnp.where(in_head < half, -sin, 0.0)
    rp = jnp.where((in_head >= half) & (in_head < ROPE_DIM), sin, 0.0)
    return ra, rm, rp


def _even_prep(x, gain, w_in, f_bias, qk_gains, rope, bsz, seq):
    t, d = x.shape
    tm = TOKEN_TILE
    a_w = 3 * MIX_W
    wa = w_in[:, :a_w].astype(BF16)
    wf = jnp.pad(w_in[:, a_w:a_w + A_HEADS], ((0, 0), (0, LANES - A_HEADS))).astype(BF16)
    wb = w_in[:, a_w + A_HEADS:].astype(BF16)
    fb = jnp.pad(f_bias, (0, LANES - A_HEADS)).reshape(1, LANES)
    n_s = seq // tm
    tok = lambda w: pl.BlockSpec((tm, w), lambda b, s: (b * n_s + s, 0))
    vt_spec = pl.BlockSpec((None, tm // ATT_TILE, MIX_W, ATT_TILE), lambda b, s: (b, s, 0, 0))
    stream = lambda w: jax.ShapeDtypeStruct((t, w), BF16)
    vt_shape = jax.ShapeDtypeStruct((bsz, seq // ATT_TILE, MIX_W, ATT_TILE), BF16)
    return pl.pallas_call(
        _even_prep_kernel,
        out_shape=[stream(A_HEADS * LANES), stream(A_HEADS * LANES), vt_shape,
                   stream(MIX_W), stream(MIX_W), vt_shape],
        grid=(bsz, n_s),
        in_specs=[tok(d), _const_spec((1, d)), _const_spec((d, a_w)), _const_spec((d, LANES)),
                  _const_spec((1, LANES)), _const_spec((d, a_w)), _const_spec((4, MIX_W)),
                  _const_spec((BD_WIDTH, BD_WIDTH)), tok(LANES), tok(LANES), tok(LANES)],
        out_specs=[tok(A_HEADS * LANES), tok(A_HEADS * LANES), vt_spec,
                   tok(MIX_W), tok(MIX_W), vt_spec],
        scratch_shapes=[pltpu.VMEM((1, LANES), F32)],
        compiler_params=_params(2),
        name="even_prep",
    )(x, gain.reshape(1, d), wa, wf, fb, wb, _tile_heads(qk_gains), _block_diag_ones(), *rope)


def _lane_lo(shape):
    return lax.broadcasted_iota(jnp.int32, shape, len(shape) - 1) < HEAD_DIM


def _key_rows(g):
    if isinstance(g, int):
        return slice(g * ATT_TILE, (g + 1) * ATT_TILE)
    return pl.ds(pl.multiple_of(g * ATT_TILE, ATT_TILE), ATT_TILE)


SUM_ROWS = 16
MXU_LOOKAHEAD = 5


def _with_sum_rows(value_t):
    return jnp.concatenate([value_t, jnp.ones((SUM_ROWS, value_t.shape[1]), BF16)], axis=0)


def _normalise(acc, v_rows):
    return acc[:v_rows] / acc[v_rows:v_rows + 1]


def _paired_causal_attention(a, n_tiles, n_sub, queries, key_block, value_t, diag_visible, v_rows):
    tq = ATT_TILE
    tiles = (a, n_tiles - 1 - a)
    items = [(t, g, sub) for t, tile in enumerate(tiles) for g in range(tile + 1)
             for sub in range(n_sub)]
    maxes = [[jnp.full((1, tq), NEG, F32)] * n_sub for _ in tiles]
    accs = [[jnp.zeros((v_rows + SUM_ROWS, tq), F32)] * n_sub for _ in tiles]

    def scores(t, g, sub):
        s = _dot_nt(key_block(g, sub), queries[t][sub])
        return jnp.where(diag_visible, s, NEG) if g == tiles[t] else s

    def exponentials(t, g, sub, s):
        m_old = maxes[t][sub]
        m_new = jnp.maximum(m_old, jnp.max(s, axis=0, keepdims=True))
        maxes[t][sub] = m_new
        return jnp.exp2(s - m_new).astype(BF16), jnp.exp2(m_old - m_new)

    def accumulate(t, g, sub, p, alpha):
        accs[t][sub] = alpha * accs[t][sub] + _dot(_with_sum_rows(value_t(g, sub)), p)

    in_flight = []
    for item in items:
        in_flight.append((item, exponentials(*item, scores(*item))))
        if len(in_flight) > MXU_LOOKAHEAD:
            done, (p, alpha) = in_flight.pop(0)
            accumulate(*done, p, alpha)
    for done, (p, alpha) in in_flight:
        accumulate(*done, p, alpha)
    return accs


def _for_each_tile_pair(n_tiles, body):
    for a in range(n_tiles // 2):
        pl.when(pl.program_id(2) == a)(functools.partial(body, a))


HEADS_PER_STEP = 4


def _paired_specs(n_tiles, q_width, out_width):
    tq = ATT_TILE
    half = n_tiles // 2
    q_a = pl.BlockSpec((None, tq, q_width), lambda b, p, a: (b, a, p))
    q_b = pl.BlockSpec((None, tq, q_width), lambda b, p, a: (b, n_tiles - 1 - a, p))
    o_a = pl.BlockSpec((None, tq, out_width), lambda b, p, a: (b, a, p))
    o_b = pl.BlockSpec((None, tq, out_width), lambda b, p, a: (b, half - 1 - a, p))
    return q_a, q_b, o_a, o_b


def _fox_kernel(qa_ref, qb_ref, k_ref, vt_ref, oa_ref, ob_ref, *, n_tiles):
    tq = ATT_TILE

    def key_block(g, sub):
        return k_ref[_key_rows(g), sub * LANES:(sub + 1) * LANES]

    def value_t(g, sub):
        return vt_ref[g, sub * HEAD_DIM:(sub + 1) * HEAD_DIM, :]

    def body(a):
        queries = [[q_ref[:, sub * LANES:(sub + 1) * LANES] for sub in range(HEADS_PER_STEP)]
                   for q_ref in (qa_ref, qb_ref)]
        causal = (lax.broadcasted_iota(jnp.int32, (tq, tq), 0)
                  <= lax.broadcasted_iota(jnp.int32, (tq, tq), 1))
        res = _paired_causal_attention(a, n_tiles, HEADS_PER_STEP, queries, key_block, value_t,
                                       causal, HEAD_DIM)
        for accs, o_ref in zip(res, (oa_ref, ob_ref)):
            o_t = jnp.concatenate([_normalise(acc, HEAD_DIM) for acc in accs], axis=0)
            o_ref[...] = o_t.T.astype(o_ref.dtype)

    _for_each_tile_pair(n_tiles, body)


def _fox_attention(q, k, vt, bsz, seq):
    tq = ATT_TILE
    n_tiles = seq // tq
    q3 = q.reshape(bsz, seq, A_HEADS * LANES)
    k3 = k.reshape(bsz, seq, A_HEADS * LANES)
    out_width = HEADS_PER_STEP * HEAD_DIM
    q_a, q_b, o_a, o_b = _paired_specs(n_tiles, HEADS_PER_STEP * LANES, out_width)
    half_out = jax.ShapeDtypeStruct((bsz, seq // 2, MIX_W), BF16)
    out_a, out_b = pl.pallas_call(
        functools.partial(_fox_kernel, n_tiles=n_tiles),
        out_shape=[half_out, half_out],
        grid=(bsz, A_HEADS // HEADS_PER_STEP, n_tiles // 2),
        in_specs=[q_a, q_b,
                  pl.BlockSpec((None, seq, HEADS_PER_STEP * LANES), lambda b, p, a: (b, 0, p)),
                  pl.BlockSpec((None, n_tiles, out_width, tq), lambda b, p, a: (b, 0, p, 0))],
        out_specs=[o_a, o_b],
        compiler_params=_params(3),
        name="fox_attention",
    )(q3, q3, k3, vt)
    return out_a, out_b


def _diff_kernel(qa_ref, qb_ref, k_ref, vt_ref, lam_ref, sg_ref, oa_ref, ob_ref, *,
                 lambda_init, n_tiles):
    tq = ATT_TILE
    lo = _lane_lo((tq, LANES))
    lp = lam_ref[...]
    lam = (jnp.exp(jnp.sum(lp[0:1] * lp[1:2], axis=1, keepdims=True))
           - jnp.exp(jnp.sum(lp[2:3] * lp[3:4], axis=1, keepdims=True)) + lambda_init)

    def half_of(q_ref, sub):
        q = q_ref[:, sub // 2 * LANES:(sub // 2 + 1) * LANES]
        zero = jnp.zeros_like(q)
        return jnp.where(lo, q, zero) if sub % 2 == 0 else jnp.where(lo, zero, q)

    def key_block(g, sub):
        return k_ref[_key_rows(g), sub // 2 * LANES:(sub // 2 + 1) * LANES]

    def value_t(g, sub):
        return vt_ref[g, sub // 2 * LANES:(sub // 2 + 1) * LANES, :]

    def body(a):
        queries = [[half_of(q_ref, sub) for sub in range(HEADS_PER_STEP)]
                   for q_ref in (qa_ref, qb_ref)]
        chunk_causal = (lax.broadcasted_iota(jnp.int32, (tq, tq), 0) // CHUNK
                        <= lax.broadcasted_iota(jnp.int32, (tq, tq), 1) // CHUNK)
        res = _paired_causal_attention(a, n_tiles, HEADS_PER_STEP, queries, key_block, value_t,
                                       chunk_causal, LANES)
        for accs, o_ref in zip(res, (oa_ref, ob_ref)):
            for hd in range(HEADS_PER_STEP // 2):
                o1, o2 = (_normalise(acc, LANES) for acc in accs[2 * hd:2 * hd + 2])
                o = (o1 - lam * o2).T
                o_ref[:, hd * LANES:(hd + 1) * LANES] = (
                    _rms(o) * sg_ref[...] * (1.0 - lambda_init)).astype(o_ref.dtype)

    _for_each_tile_pair(n_tiles, body)


def _diff_attention(q, k, vt, lam_params, subln_gain, lambda_init, bsz, seq):
    tq = ATT_TILE
    n_tiles = seq // tq
    q3, k3 = (a.reshape(bsz, seq, MIX_W) for a in (q, k))
    width = HEADS_PER_STEP // 2 * LANES
    q_a, q_b, o_a, o_b = _paired_specs(n_tiles, width, width)
    half_out = jax.ShapeDtypeStruct((bsz, seq // 2, MIX_W), BF16)
    out_a, out_b = pl.pallas_call(
        functools.partial(_diff_kernel, lambda_init=lambda_init, n_tiles=n_tiles),
        out_shape=[half_out, half_out],
        grid=(bsz, MIX_W // width, n_tiles // 2),
        in_specs=[q_a, q_b,
                  pl.BlockSpec((None, seq, width), lambda b, h, a: (b, 0, h)),
                  pl.BlockSpec((None, n_tiles, width, tq), lambda b, h, a: (b, 0, h, 0)),
                  _const_spec((4, HEAD_DIM)), _const_spec((1, LANES))],
        out_specs=[o_a, o_b],
        compiler_params=_params(3),
        name="diff_attention",
    )(q3, q3, k3, vt, lam_params, subln_gain.reshape(1, LANES))
    return out_a, out_b


def _odd_prep_kernel(x_ref, g_ref, wc_ref, wd_ref, cw_ref, gains_ref, bd_ref,
                     c_ref, dq_ref, dk_ref, dvt_ref, carry_ref):
    tm = x_ref.shape[0]

    @pl.when(pl.program_id(1) == 0)
    def _():
        carry_ref[...] = jnp.zeros_like(carry_ref)

    h = (_rms(x_ref[...]) * g_ref[...]).astype(BF16)
    bd = bd_ref[...]
    gains = gains_ref[...]

    def project(w_ref, part):
        return _dot(h, w_ref[:, part * MIX_W:(part + 1) * MIX_W])

    d_q = project(wd_ref, 0)
    d_k = project(wd_ref, 1)
    d_q_ss = _head_sumsq(d_q, bd)
    c_c = project(wc_ref, 1)
    d_k_ss = _head_sumsq(d_k, bd)
    c_h = project(wc_ref, 2)
    c_b = project(wc_ref, 0)
    d_v = project(wd_ref, 2)

    u = c_c * c_h
    prev = carry_ref[...]
    carry_ref[...] = u[tm - 8:, :]
    row = lax.broadcasted_iota(jnp.int32, u.shape, 0)
    u1 = jnp.where(row == 0, prev[7:8], pltpu.roll(u, 1, axis=0))
    u2 = jnp.where(row == 0, prev[6:7], jnp.where(row == 1, prev[7:8], pltpu.roll(u, 2, axis=0)))
    cw = cw_ref[...]
    conv = cw[0:1] * u2 + cw[1:2] * u1 + cw[2:3] * u
    c_ref[...] = (c_b * conv).astype(BF16)

    dq_ref[...] = (_head_norm(d_q, d_q_ss, gains[0:1]) * Q_SCALE).astype(BF16)
    dk_ref[...] = _head_norm(d_k, d_k_ss, gains[1:2]).astype(BF16)
    _store_transposed(dvt_ref, d_v)


def _odd_prep(x, gain, w_in, conv_w, qk_gains, bsz, seq):
    t, d = x.shape
    tm = TOKEN_TILE
    w3 = 3 * MIX_W
    n_s = seq // tm
    tok = lambda w: pl.BlockSpec((tm, w), lambda b, s: (b * n_s + s, 0))
    stream = jax.ShapeDtypeStruct((t, MIX_W), BF16)
    return pl.pallas_call(
        _odd_prep_kernel,
        out_shape=[stream] * 3 + [jax.ShapeDtypeStruct((bsz, seq // ATT_TILE, MIX_W, ATT_TILE), BF16)],
        grid=(bsz, n_s),
        in_specs=[tok(d), _const_spec((1, d)), _const_spec((d, w3)), _const_spec((d, w3)),
                  _const_spec((3, MIX_W)), _const_spec((2, MIX_W)), _const_spec((BD_WIDTH, BD_WIDTH))],
        out_specs=[tok(MIX_W)] * 3 + [pl.BlockSpec((None, tm // ATT_TILE, MIX_W, ATT_TILE),
                                                   lambda b, s: (b, s, 0, 0))],
        scratch_shapes=[pltpu.VMEM((8, MIX_W), F32)],
        compiler_params=_params(2),
        name="odd_prep",
    )(x, gain.reshape(1, d), w_in[:, :w3].astype(BF16), w_in[:, w3:].astype(BF16), conv_w,
      _tile_heads(qk_gains), _block_diag_ones())


BAND_GROUPS = 1 + D_LEFT // ATT_TILE


MAX_BAND_TILES = 8
BAND_WIDTH = BAND_GROUPS * ATT_TILE
BIAS_ROW = BAND_WIDTH + ATT_TILE


def _band_kernel(q_ref, k_ref, vt_ref, w_ref, o_ref, bias_ref):
    tq = ATT_TILE
    BAND_TILES = q_ref.shape[0] // tq
    j = pl.program_id(2)

    @pl.when((pl.program_id(1) == 0) & (j == 0))
    def _():
        key = lax.broadcasted_iota(jnp.int32, (BAND_WIDTH, tq), 0)
        chunk_start = lax.broadcasted_iota(jnp.int32, (BAND_WIDTH, tq), 1) // CHUNK * CHUNK
        in_band = (key >= chunk_start) & (key < chunk_start + D_LEFT + CHUNK)
        for sub in range(2):
            rows = jnp.broadcast_to(w_ref[sub], (BAND_WIDTH, BIAS_ROW))
            skew = pltpu.roll(rows, tq + 1, axis=1, stride=1, stride_axis=0)
            table = jnp.where(in_band, skew[:, :tq] * LOG2E, NEG)
            for grp in range(BAND_GROUPS):
                bias_ref[sub, grp] = table[grp * tq:(grp + 1) * tq, :]

    lo = _lane_lo((tq, LANES))
    before_start = jnp.where(j == 0, NEG, 0.0)
    q_subs, items = {}, []
    for tile in range(BAND_TILES):
        q = q_ref[tile * tq:(tile + 1) * tq, :]
        zero = jnp.zeros_like(q)
        q_subs[tile] = (jnp.where(lo, q, zero), jnp.where(lo, zero, q))
        items += [(tile, grp, sub) for grp in range(BAND_GROUPS) for sub in range(2)]

    def may_precede(tile, grp):
        return tile + grp < BAND_GROUPS - 1

    def group_index(tile, grp):
        g = BAND_TILES * j + tile + grp - (BAND_GROUPS - 1)
        return jnp.maximum(g, 0) if may_precede(tile, grp) else g

    def scores(tile, grp, sub):
        s = _dot_nt(k_ref[_key_rows(group_index(tile, grp)), :], q_subs[tile][sub]) + bias_ref[sub, grp]
        return s + before_start if may_precede(tile, grp) else s

    maxes, accs = {}, {}

    def exponentials(tile, grp, sub, s):
        mx = jnp.max(s, axis=0, keepdims=True)
        m_old = maxes.get((tile, sub))
        m_new = mx if m_old is None else jnp.maximum(m_old, mx)
        maxes[tile, sub] = m_new
        return jnp.exp2(s - m_new).astype(BF16), None if m_old is None else jnp.exp2(m_old - m_new)

    def accumulate(tile, grp, sub, p, alpha):
        value_t = _with_sum_rows(vt_ref[group_index(tile, grp), sub * HEAD_DIM:(sub + 1) * HEAD_DIM, :])
        acc_n = _dot(value_t, p)
        accs[tile, sub] = acc_n if alpha is None else alpha * accs[tile, sub] + acc_n

    in_flight = []
    for item in items:
        in_flight.append((item, exponentials(*item, scores(*item))))
        if len(in_flight) > MXU_LOOKAHEAD:
            done, (p, alpha) = in_flight.pop(0)
            accumulate(*done, p, alpha)
    for done, (p, alpha) in in_flight:
        accumulate(*done, p, alpha)
    for tile in range(BAND_TILES):
        o_t = jnp.concatenate([_normalise(accs[tile, sub], HEAD_DIM) for sub in range(2)], axis=0)
        o_ref[tile * tq:(tile + 1) * tq, :] = o_t.T.astype(o_ref.dtype)


def _band_bias_rows(rel_table):
    n_lo = ATT_TILE - 1 - REL_CLIP
    n_hi = BIAS_ROW - n_lo - (2 * REL_CLIP + 1)
    w = jnp.concatenate([jnp.repeat(rel_table[:, :1], n_lo, axis=1), rel_table,
                         jnp.repeat(rel_table[:, -1:], n_hi, axis=1)], axis=1)
    return w.astype(F32).reshape(rel_table.shape[0], 1, BIAS_ROW)


def _band_attention(q, k, vt, rel_table, bsz, seq):
    tq = ATT_TILE
    n_q = seq // tq
    BAND_TILES = min(MAX_BAND_TILES, n_q)
    q3, k3 = (a.reshape(bsz, seq, MIX_W) for a in (q, k))
    q_spec = pl.BlockSpec((None, BAND_TILES * tq, LANES), lambda p, b, j: (b, j, p))
    out = pl.pallas_call(
        _band_kernel,
        out_shape=jax.ShapeDtypeStruct((bsz, seq, MIX_W), BF16),
        grid=(MIX_W // LANES, bsz, n_q // BAND_TILES),
        in_specs=[q_spec,
                  pl.BlockSpec((None, seq, LANES), lambda p, b, j: (b, 0, p)),
                  pl.BlockSpec((None, n_q, LANES, tq), lambda p, b, j: (b, 0, p, 0)),
                  pl.BlockSpec((2, 1, BIAS_ROW), lambda p, b, j: (p, 0, 0))],
        out_specs=q_spec,
        scratch_shapes=[pltpu.VMEM((2, BAND_GROUPS, tq, tq), F32)],
        compiler_params=_params(3),
        name="band_attention",
    )(q3, k3, vt, _band_bias_rows(rel_table))
    return out.reshape(bsz * seq, MIX_W)


def _mem_kv_kernel(mem_ref, g_ref, w_ref, kg_ref, k_ref, v_ref):
    mem_n = (_rms(mem_ref[...]) * g_ref[...]).astype(BF16)
    kv = _dot(mem_n, w_ref[...])
    kg = kg_ref[...]
    for hd in range(X_HEADS):
        sl = slice(hd * X_HEAD_DIM, (hd + 1) * X_HEAD_DIM)
        k_ref[:, sl] = (_rms(kv[:, sl]) * kg).astype(BF16)
    v_ref[...] = kv[:, D_MODEL:].astype(BF16)


def _mem_kv(mem, gain, w_kv, k_gain):
    bsz, n_mem, d = mem.shape
    blk = pl.BlockSpec((None, n_mem, d), lambda b: (b, 0, 0))
    out = jax.ShapeDtypeStruct((bsz, n_mem, d), BF16)
    return pl.pallas_call(
        _mem_kv_kernel,
        out_shape=[out, out],
        grid=(bsz,),
        in_specs=[blk, _const_spec((1, d)), _const_spec((d, 2 * d)), _const_spec((1, X_HEAD_DIM))],
        out_specs=[blk, blk],
        compiler_params=_params(1),
        name="mem_kv",
    )(mem, gain.reshape(1, d), w_kv.astype(BF16), k_gain.reshape(1, X_HEAD_DIM))


def _cross_kernel(x_ref, left_a_ref, left_b_ref, right_a_ref, right_b_ref, wl_ref, wr_ref, g_ref,
                  wq_ref, qg_ref, k_ref, v_ref, wo_ref, o_ref):
    first_half = pl.program_id(1) < pl.num_programs(1) // 2
    left = jnp.where(first_half, left_a_ref[...], left_b_ref[...])
    right = jnp.where(first_half, right_a_ref[...], right_b_ref[...])
    x = x_ref[...] + _dot(left, wl_ref[...]) + _dot(right, wr_ref[...])
    h = (_rms(x) * g_ref[...]).astype(BF16)
    q = _dot(h, wq_ref[...])
    qg = qg_ref[...]
    head_cols = [slice(hd * X_HEAD_DIM, (hd + 1) * X_HEAD_DIM) for hd in range(X_HEADS)]

    def probabilities(sl):
        qh = (_rms(q[:, sl]) * qg * X_HEAD_DIM ** -0.5).astype(BF16)
        s = _dot_nt(qh, k_ref[:, sl])
        p = jnp.exp(s - jnp.max(s, axis=-1, keepdims=True))
        return p.astype(BF16), jnp.sum(p, axis=-1, keepdims=True)

    heads = []
    ahead = probabilities(head_cols[0])
    for hd, sl in enumerate(head_cols):
        p, l = ahead
        if hd + 1 < X_HEADS:
            ahead = probabilities(head_cols[hd + 1])
        heads.append((_dot(p, v_ref[:, sl]) / l).astype(BF16))
    o_ref[...] = x + _dot(jnp.concatenate(heads, axis=1), wo_ref[...])


def _halves(stream, bsz, seq):
    if isinstance(stream, tuple):
        return stream[0], stream[1], 0
    full = stream.reshape(bsz, seq, MIX_W)
    return full, full, seq // TOKEN_TILE // 2


def _mix_out_cross_attention(x, left, right, w_mix_out, gain, w_q, q_gain, k, v, w_o, bsz, seq):
    t, d = x.shape
    tq = TOKEN_TILE
    n_s = seq // tq
    half = n_s // 2
    n_mem = k.shape[1]
    tok = pl.BlockSpec((tq, d), lambda b, s: (b * n_s + s, 0))
    mem_spec = pl.BlockSpec((None, n_mem, d), lambda b, s: (b, 0, 0))
    w_mix = w_mix_out.astype(BF16)
    stream_args, stream_specs = [], []
    for first, second, offset in (_halves(left, bsz, seq), _halves(right, bsz, seq)):
        stream_args += [first, second]
        stream_specs += [
            pl.BlockSpec((None, tq, MIX_W), lambda b, s: (b, jnp.minimum(s, half - 1), 0)),
            pl.BlockSpec((None, tq, MIX_W),
                         lambda b, s, offset=offset: (b, offset + jnp.maximum(s - half, 0), 0))]
    return pl.pallas_call(
        _cross_kernel,
        out_shape=jax.ShapeDtypeStruct((t, d), F32),
        grid=(bsz, n_s),
        in_specs=[tok, *stream_specs, _const_spec((MIX_W, d)), _const_spec((MIX_W, d)),
                  _const_spec((1, d)), _const_spec((d, d)), _const_spec((1, X_HEAD_DIM)),
                  mem_spec, mem_spec, _const_spec((d, d))],
        out_specs=tok,
        compiler_params=_params(2),
        name="cross_attention",
    )(x, *stream_args, w_mix[:MIX_W], w_mix[MIX_W:], gain.reshape(1, d), w_q.astype(BF16),
      q_gain.reshape(1, X_HEAD_DIM), k, v, w_o.astype(BF16))


def kernel(x, mem, positions, ln_gains, ffn1_w_in, ffn1_w_out, ffn2_w_in, ffn2_w_out, even_w_in, even_f_bias, even_qk_gains, even_lambda, even_subln_gain, even_w_out, odd_w_in, odd_conv_w, odd_qk_gains, odd_rel_bias, odd_w_out, x_w_q, x_w_kv, x_qk_gains, x_w_o):
    bsz, seq, d = x.shape
    depth = ln_gains.shape[0]
    rope = _rope_tables(positions)
    x = x.reshape(bsz * seq, d)
    for layer in range(depth):
        g = ln_gains[layer]
        x = _ffn(x, g[0], ffn1_w_in[layer], ffn1_w_out[layer])
        if layer % 2 == 0:
            e = layer // 2
            lambda_init = 0.8 - 0.6 * math.exp(-0.3 * layer)
            aq, ak, avt, bq, bk, bvt = _even_prep(
                x, g[1], even_w_in[e], even_f_bias[e], even_qk_gains[e], rope, bsz, seq)
            left = _fox_attention(aq, ak, avt, bsz, seq)
            right = _diff_attention(bq, bk, bvt, even_lambda[e], even_subln_gain[e], lambda_init,
                                    bsz, seq)
            w_mix_out = even_w_out[e]
        else:
            o = layer // 2
            left, dq, dk, dvt = _odd_prep(x, g[1], odd_w_in[o], odd_conv_w[o], odd_qk_gains[o],
                                          bsz, seq)
            right = _band_attention(dq, dk, dvt, odd_rel_bias[o], bsz, seq)
            w_mix_out = odd_w_out[o]
        mk, mv = _mem_kv(mem, g[3], x_w_kv[layer], x_qk_gains[layer, 1])
        x = _mix_out_cross_attention(x, left, right, w_mix_out, g[2], x_w_q[layer],
                                     x_qk_gains[layer, 0], mk, mv, x_w_o[layer], bsz, seq)
        x = _ffn(x, g[4], ffn2_w_in[layer], ffn2_w_out[layer])
    return x.reshape(bsz, seq, d)
```

```python
import functools
import math

import jax
import jax.numpy as jnp
from jax import lax
from jax.experimental import pallas as pl
from jax.experimental.pallas import tpu as pltpu

D_MODEL = 1024
CHUNK = 64
HEAD_DIM = 64
ROPE_DIM = HEAD_DIM // 4
ROPE_THETA = 500000.0
RMS_EPS = 1e-6
D_FF = 2816
A_HEADS = 8
MIX_W = 512
LANES = 128
D_LEFT = 8 * CHUNK
REL_CLIP = 128
X_HEADS = 4
X_HEAD_DIM = D_MODEL // X_HEADS
NEG = -1e30
LOG2E = math.log2(math.e)
Q_SCALE = HEAD_DIM ** -0.5 * LOG2E

VMEM_LIMIT = 56 * 1024 * 1024
FFN_CHUNKS = (768, 768, 768, 512)
TOKEN_TILE = 512
ATT_TILE = 256

BF16 = jnp.bfloat16
F32 = jnp.float32


def _params(n_axes):
    return pltpu.CompilerParams(dimension_semantics=("arbitrary",) * n_axes,
                                vmem_limit_bytes=VMEM_LIMIT)


def _const_spec(shape):
    nd = len(shape)
    return pl.BlockSpec(shape, lambda *_: (0,) * nd, pipeline_mode=pl.Buffered(1))


def _rms(x):
    return x * lax.rsqrt(jnp.mean(x * x, axis=-1, keepdims=True) + RMS_EPS)


def _dot(a, b):
    return jnp.dot(a, b, preferred_element_type=F32)


def _dot_nt(a, b):
    return lax.dot_general(a, b, (((1,), (1,)), ((), ())), preferred_element_type=F32)


def _ffn_kernel(x_ref, g_ref, wg_ref, wu_ref, wo_ref, o_ref):
    x = x_ref[...]
    xn = (_rms(x) * g_ref[...]).astype(BF16)

    y = jnp.zeros_like(x)
    start = 0
    for width in FFN_CHUNKS:
        gate = _dot(xn, wg_ref[:, start:start + width])
        up = _dot(xn, wu_ref[:, start:start + width])
        act = (gate * (1.0 / (1.0 + jnp.exp(-gate))) * up).astype(BF16)
        y = y + _dot(act, wo_ref[start:start + width, :])
        start += width
    o_ref[...] = x + 0.5 * y


def _ffn(x, gain, w_in, w_out):
    t, d = x.shape
    tm = TOKEN_TILE
    wg = w_in[:, :D_FF].astype(BF16)
    wu = w_in[:, D_FF:].astype(BF16)
    wo = w_out.astype(BF16)
    return pl.pallas_call(
        _ffn_kernel,
        out_shape=jax.ShapeDtypeStruct((t, d), F32),
        grid=(t // tm,),
        in_specs=[pl.BlockSpec((tm, d), lambda i: (i, 0)),
                  _const_spec((1, d)),
                  _const_spec((d, D_FF)), _const_spec((d, D_FF)), _const_spec((D_FF, d))],
        out_specs=pl.BlockSpec((tm, d), lambda i: (i, 0)),
        compiler_params=_params(1),
        name="ffn",
    )(x, gain.reshape(1, d), wg, wu, wo)


BD_WIDTH = 256


def _head_sumsq(y, ones_bd):
    sq = (y * y).astype(BF16)
    return jnp.concatenate([_dot(sq[:, c:c + BD_WIDTH], ones_bd) for c in range(0, MIX_W, BD_WIDTH)],
                           axis=1)


def _head_norm(y, sumsq, gain):
    return y * lax.rsqrt(sumsq * (1.0 / HEAD_DIM) + RMS_EPS) * gain


def _block_diag_ones():
    r = jnp.arange(BD_WIDTH) // HEAD_DIM
    return (r[:, None] == r[None, :]).astype(BF16)


def _tile_heads(v):
    return jnp.tile(v, (1, MIX_W // HEAD_DIM))


def _store_transposed(vt_ref, v):
    for i in range(v.shape[0] // ATT_TILE):
        vt_ref[i] = v[i * ATT_TILE:(i + 1) * ATT_TILE, :].T.astype(BF16)


def _split3(f):
    hi = f.astype(BF16).astype(F32)
    rest = f - hi
    mid = rest.astype(BF16).astype(F32)
    return hi, mid, rest - mid


def _even_prep_kernel(x_ref, g_ref, wa_ref, wf_ref, fb_ref, wb_ref, gains_ref, bd_ref,
                      ra_ref, rm_ref, rp_ref,
                      aq_ref, ak_ref, avt_ref, bq_ref, bk_ref, bvt_ref, carry_ref):
    tm = x_ref.shape[0]

    @pl.when(pl.program_id(1) == 0)
    def _():
        carry_ref[...] = jnp.zeros_like(carry_ref)

    h = (_rms(x_ref[...]) * g_ref[...]).astype(BF16)
    bd = bd_ref[...]
    gains = gains_ref[...]

    def project(w_ref, part):
        return _dot(h, w_ref[:, part * MIX_W:(part + 1) * MIX_W])

    z = _dot(h, wf_ref[...]) + fb_ref[...]
    a_q = project(wa_ref, 0)

    logf = jnp.minimum(z, 0.0) - jnp.log1p(jnp.exp(-jnp.abs(z)))
    row = lax.broadcasted_iota(jnp.int32, logf.shape, 0)
    step = 1
    while step < tm:
        logf = logf + jnp.where(row >= step, pltpu.roll(logf, step, axis=0), 0.0)
        step *= 2
    cum = logf + carry_ref[...]
    carry_ref[...] = cum[tm - 1:tm, :]
    hi, mid, lo = _split3(cum * LOG2E)

    a_k = project(wa_ref, 1)
    a_q_ss = _head_sumsq(a_q, bd)
    b_q = project(wb_ref, 0)
    qn = _head_norm(a_q, a_q_ss, gains[0:1]) * Q_SCALE
    a_k_ss = _head_sumsq(a_k, bd)
    b_k = project(wb_ref, 1)
    kn = _head_norm(a_k, a_k_ss, gains[1:2])

    lane = lax.broadcasted_iota(jnp.int32, (tm, LANES), 1)
    is_hi = (lane == HEAD_DIM) | (lane == HEAD_DIM + 3)
    is_mid = (lane == HEAD_DIM + 1) | (lane == HEAD_DIM + 4)
    ones_hi = jnp.where(lane < HEAD_DIM + 6, 1.0, 0.0)
    for hd in range(A_HEADS):
        pieces = jnp.where(is_hi, hi[:, hd:hd + 1],
                           jnp.where(is_mid, mid[:, hd:hd + 1], lo[:, hd:hd + 1]))
        blk = slice((hd // 2) * LANES, (hd // 2 + 1) * LANES)
        sq, sk = qn[:, blk], kn[:, blk]
        if hd % 2:
            sq, sk = pltpu.roll(sq, HEAD_DIM, axis=1), pltpu.roll(sk, HEAD_DIM, axis=1)
        qa = jnp.where(lane < HEAD_DIM, sq, jnp.where(lane < HEAD_DIM + 3, pieces, ones_hi))
        ka = jnp.where(lane < HEAD_DIM, sk, jnp.where(lane < HEAD_DIM + 3, 1.0,
                                                      jnp.where(lane < HEAD_DIM + 6, -pieces, 0.0)))
        aq_ref[:, hd * LANES:(hd + 1) * LANES] = qa.astype(BF16)
        ak_ref[:, hd * LANES:(hd + 1) * LANES] = ka.astype(BF16)

    rep = MIX_W // LANES
    ra = jnp.tile(ra_ref[...], (1, rep))
    rm = jnp.tile(rm_ref[...], (1, rep))
    rp = jnp.tile(rp_ref[...], (1, rep))

    def rope(v):
        half = ROPE_DIM // 2
        return v * ra + pltpu.roll(v, MIX_W - half, axis=1) * rm + pltpu.roll(v, half, axis=1) * rp

    b_q_ss = _head_sumsq(b_q, bd)
    a_v = project(wa_ref, 2)
    bq_ref[...] = (rope(_head_norm(b_q, b_q_ss, gains[2:3])) * Q_SCALE).astype(BF16)
    b_k_ss = _head_sumsq(b_k, bd)
    _store_transposed(avt_ref, a_v)
    b_v = project(wb_ref, 2)
    bk_ref[...] = rope(_head_norm(b_k, b_k_ss, gains[3:4])).astype(BF16)
    _store_transposed(bvt_ref, b_v)


def _rope_tables(positions):
    half = ROPE_DIM // 2
    inv = ROPE_THETA ** (-jnp.arange(0, ROPE_DIM, 2, dtype=F32) / ROPE_DIM)
    ang = positions.astype(F32).reshape(-1, 1) * inv
    cos = jnp.tile(jnp.cos(ang), (1, LANES // half))
    sin = jnp.tile(jnp.sin(ang), (1, LANES // half))
    in_head = jnp.arange(LANES) % HEAD_DIM
    ra = jnp.where(in_head < ROPE_DIM, cos, 1.0)
    rm = jnp.where(in_head < half, -sin, 0.0)
    rp = jnp.where((in_head >= half) & (in_head < ROPE_DIM), sin, 0.0)
    return ra, rm, rp


def _even_prep(x, gain, w_in, f_bias, qk_gains, rope, bsz, seq):
    t, d = x.shape
    tm = TOKEN_TILE
    a_w = 3 * MIX_W
    wa = w_in[:, :a_w].astype(BF16)
    wf = jnp.pad(w_in[:, a_w:a_w + A_HEADS], ((0, 0), (0, LANES - A_HEADS))).astype(BF16)
    wb = w_in[:, a_w + A_HEADS:].astype(BF16)
    fb = jnp.pad(f_bias, (0, LANES - A_HEADS)).reshape(1, LANES)
    n_s = seq // tm
    tok = lambda w: pl.BlockSpec((tm, w), lambda b, s: (b * n_s + s, 0))
    vt_spec = pl.BlockSpec((None, tm // ATT_TILE, MIX_W, ATT_TILE), lambda b, s: (b, s, 0, 0))
    stream = lambda w: jax.ShapeDtypeStruct((t, w), BF16)
    vt_shape = jax.ShapeDtypeStruct((bsz, seq // ATT_TILE, MIX_W, ATT_TILE), BF16)
    return pl.pallas_call(
        _even_prep_kernel,
        out_shape=[stream(A_HEADS * LANES), stream(A_HEADS * LANES), vt_shape,
                   stream(MIX_W), stream(MIX_W), vt_shape],
        grid=(bsz, n_s),
        in_specs=[tok(d), _const_spec((1, d)), _const_spec((d, a_w)), _const_spec((d, LANES)),
                  _const_spec((1, LANES)), _const_spec((d, a_w)), _const_spec((4, MIX_W)),
                  _const_spec((BD_WIDTH, BD_WIDTH)), tok(LANES), tok(LANES), tok(LANES)],
        out_specs=[tok(A_HEADS * LANES), tok(A_HEADS * LANES), vt_spec,
                   tok(MIX_W), tok(MIX_W), vt_spec],
        scratch_shapes=[pltpu.VMEM((1, LANES), F32)],
        compiler_params=_params(2),
        name="even_prep",
    )(x, gain.reshape(1, d), wa, wf, fb, wb, _tile_heads(qk_gains), _block_diag_ones(), *rope)


def _lane_lo(shape):
    return lax.broadcasted_iota(jnp.int32, shape, len(shape) - 1) < HEAD_DIM


def _key_rows(g):
    if isinstance(g, int):
        return slice(g * ATT_TILE, (g + 1) * ATT_TILE)
    return pl.ds(pl.multiple_of(g * ATT_TILE, ATT_TILE), ATT_TILE)


SUM_ROWS = 16
MXU_LOOKAHEAD = 5


def _with_sum_rows(value_t):
    return jnp.concatenate([value_t, jnp.ones((SUM_ROWS, value_t.shape[1]), BF16)], axis=0)


def _normalise(acc, v_rows):
    return acc[:v_rows] / acc[v_rows:v_rows + 1]


def _causal_attention(tiles, n_sub, queries, key_block, value_t, diag_visible, v_rows, finish):
    tq = ATT_TILE
    items = [(i, g, sub) for i, tile in enumerate(tiles) for g in range(tile + 1)
             for sub in range(n_sub)]
    maxes = [[jnp.full((1, tq), NEG, F32)] * n_sub for _ in tiles]
    accs = [[jnp.zeros((v_rows + SUM_ROWS, tq), F32)] * n_sub for _ in tiles]

    def scores(i, g, sub):
        s = _dot_nt(key_block(g, sub), queries[i][sub])
        return jnp.where(diag_visible, s, NEG) if g == tiles[i] else s

    def exponentials(i, g, sub, s):
        m_old = maxes[i][sub]
        m_new = jnp.maximum(m_old, jnp.max(s, axis=0, keepdims=True))
        maxes[i][sub] = m_new
        return jnp.exp2(s - m_new).astype(BF16), jnp.exp2(m_old - m_new)

    def accumulate(i, g, sub, p, alpha):
        accs[i][sub] = alpha * accs[i][sub] + _dot(_with_sum_rows(value_t(g, sub)), p)
        if g == tiles[i] and sub == n_sub - 1:
            finish(i, accs[i])

    in_flight = []
    for item in items:
        in_flight.append((item, exponentials(*item, scores(*item))))
        if len(in_flight) > MXU_LOOKAHEAD:
            done, (p, alpha) = in_flight.pop(0)
            accumulate(*done, p, alpha)
    for done, (p, alpha) in in_flight:
        accumulate(*done, p, alpha)


HEADS_PER_STEP = 4
TILES_PER_HALF = 2


def _step_tiles(n_tiles, j):
    first = [TILES_PER_HALF * j + i for i in range(TILES_PER_HALF)]
    return first + [n_tiles - 1 - t for t in reversed(first)]


def _for_each_step(n_tiles, body):
    for j in range(n_tiles // (2 * TILES_PER_HALF)):
        pl.when(pl.program_id(2) == j)(functools.partial(body, _step_tiles(n_tiles, j)))


def _tile_views(front_ref, back_ref):
    rows = [slice(i * ATT_TILE, (i + 1) * ATT_TILE) for i in range(TILES_PER_HALF)]
    return [(front_ref, r) for r in rows] + [(back_ref, r) for r in rows]


def _mirrored_specs(n_tiles, q_width, out_width):
    rows = TILES_PER_HALF * ATT_TILE
    n_steps = n_tiles // (2 * TILES_PER_HALF)
    q_front = pl.BlockSpec((None, rows, q_width), lambda b, p, j: (b, j, p))
    q_back = pl.BlockSpec((None, rows, q_width), lambda b, p, j: (b, 2 * n_steps - 1 - j, p))
    o_front = pl.BlockSpec((None, rows, out_width), lambda b, p, j: (b, j, p))
    o_back = pl.BlockSpec((None, rows, out_width), lambda b, p, j: (b, n_steps - 1 - j, p))
    return q_front, q_back, o_front, o_back


def _fox_kernel(qa_ref, qb_ref, k_ref, vt_ref, oa_ref, ob_ref, *, n_tiles):
    tq = ATT_TILE

    def key_block(g, sub):
        return k_ref[_key_rows(g), sub * LANES:(sub + 1) * LANES]

    def value_t(g, sub):
        return vt_ref[g, sub * HEAD_DIM:(sub + 1) * HEAD_DIM, :]

    def body(tiles):
        queries = [[q_ref[rows, sub * LANES:(sub + 1) * LANES] for sub in range(HEADS_PER_STEP)]
                   for q_ref, rows in _tile_views(qa_ref, qb_ref)]
        outputs = _tile_views(oa_ref, ob_ref)
        causal = (lax.broadcasted_iota(jnp.int32, (tq, tq), 0)
                  <= lax.broadcasted_iota(jnp.int32, (tq, tq), 1))

        def finish(i, accs):
            o_ref, rows = outputs[i]
            o_t = jnp.concatenate([_normalise(acc, HEAD_DIM) for acc in accs], axis=0)
            o_ref[rows, :] = o_t.T.astype(o_ref.dtype)

        _causal_attention(tiles, HEADS_PER_STEP, queries, key_block, value_t, causal, HEAD_DIM,
                          finish)

    _for_each_step(n_tiles, body)


def _fox_attention(q, k, vt, bsz, seq):
    tq = ATT_TILE
    n_tiles = seq // tq
    q3 = q.reshape(bsz, seq, A_HEADS * LANES)
    k3 = k.reshape(bsz, seq, A_HEADS * LANES)
    out_width = HEADS_PER_STEP * HEAD_DIM
    q_a, q_b, o_a, o_b = _mirrored_specs(n_tiles, HEADS_PER_STEP * LANES, out_width)
    half_out = jax.ShapeDtypeStruct((bsz, seq // 2, MIX_W), BF16)
    out_a, out_b = pl.pallas_call(
        functools.partial(_fox_kernel, n_tiles=n_tiles),
        out_shape=[half_out, half_out],
        grid=(bsz, A_HEADS // HEADS_PER_STEP, n_tiles // (2 * TILES_PER_HALF)),
        in_specs=[q_a, q_b,
                  pl.BlockSpec((None, seq, HEADS_PER_STEP * LANES), lambda b, p, a: (b, 0, p)),
                  pl.BlockSpec((None, n_tiles, out_width, tq), lambda b, p, a: (b, 0, p, 0))],
        out_specs=[o_a, o_b],
        compiler_params=_params(3),
        name="fox_attention",
    )(q3, q3, k3, vt)
    return out_a, out_b


def _diff_kernel(qa_ref, qb_ref, k_ref, vt_ref, lam_ref, sg_ref, oa_ref, ob_ref, *,
                 lambda_init, n_tiles):
    tq = ATT_TILE
    lo = _lane_lo((tq, LANES))
    lp = lam_ref[...]
    lam = (jnp.exp(jnp.sum(lp[0:1] * lp[1:2], axis=1, keepdims=True))
           - jnp.exp(jnp.sum(lp[2:3] * lp[3:4], axis=1, keepdims=True)) + lambda_init)

    def half_of(q_ref, rows, sub):
        q = q_ref[rows, sub // 2 * LANES:(sub // 2 + 1) * LANES]
        zero = jnp.zeros_like(q)
        return jnp.where(lo, q, zero) if sub % 2 == 0 else jnp.where(lo, zero, q)

    def key_block(g, sub):
        return k_ref[_key_rows(g), sub // 2 * LANES:(sub // 2 + 1) * LANES]

    def value_t(g, sub):
        return vt_ref[g, sub // 2 * LANES:(sub // 2 + 1) * LANES, :]

    def body(tiles):
        queries = [[half_of(q_ref, rows, sub) for sub in range(HEADS_PER_STEP)]
                   for q_ref, rows in _tile_views(qa_ref, qb_ref)]
        outputs = _tile_views(oa_ref, ob_ref)
        chunk_causal = (lax.broadcasted_iota(jnp.int32, (tq, tq), 0) // CHUNK
                        <= lax.broadcasted_iota(jnp.int32, (tq, tq), 1) // CHUNK)

        def finish(i, accs):
            o_ref, rows = outputs[i]
            for hd in range(HEADS_PER_STEP // 2):
                o1, o2 = (_normalise(acc, LANES) for acc in accs[2 * hd:2 * hd + 2])
                o = (o1 - lam * o2).T
                o_ref[rows, hd * LANES:(hd + 1) * LANES] = (
                    _rms(o) * sg_ref[...] * (1.0 - lambda_init)).astype(o_ref.dtype)

        _causal_attention(tiles, HEADS_PER_STEP, queries, key_block, value_t, chunk_causal, LANES,
                          finish)

    _for_each_step(n_tiles, body)


def _diff_attention(q, k, vt, lam_params, subln_gain, lambda_init, bsz, seq):
    tq = ATT_TILE
    n_tiles = seq // tq
    q3, k3 = (a.reshape(bsz, seq, MIX_W) for a in (q, k))
    width = HEADS_PER_STEP // 2 * LANES
    q_a, q_b, o_a, o_b = _mirrored_specs(n_tiles, width, width)
    half_out = jax.ShapeDtypeStruct((bsz, seq // 2, MIX_W), BF16)
    out_a, out_b = pl.pallas_call(
        functools.partial(_diff_kernel, lambda_init=lambda_init, n_tiles=n_tiles),
        out_shape=[half_out, half_out],
        grid=(bsz, MIX_W // width, n_tiles // (2 * TILES_PER_HALF)),
        in_specs=[q_a, q_b,
                  pl.BlockSpec((None, seq, width), lambda b, h, a: (b, 0, h)),
                  pl.BlockSpec((None, n_tiles, width, tq), lambda b, h, a: (b, 0, h, 0)),
                  _const_spec((4, HEAD_DIM)), _const_spec((1, LANES))],
        out_specs=[o_a, o_b],
        compiler_params=_params(3),
        name="diff_attention",
    )(q3, q3, k3, vt, lam_params, subln_gain.reshape(1, LANES))
    return out_a, out_b


def _odd_prep_kernel(x_ref, g_ref, wc_ref, wd_ref, cw_ref, gains_ref, bd_ref,
                     c_ref, dq_ref, dk_ref, dvt_ref, carry_ref):
    tm = x_ref.shape[0]

    @pl.when(pl.program_id(1) == 0)
    def _():
        carry_ref[...] = jnp.zeros_like(carry_ref)

    h = (_rms(x_ref[...]) * g_ref[...]).astype(BF16)
    bd = bd_ref[...]
    gains = gains_ref[...]

    def project(w_ref, part):
        return _dot(h, w_ref[:, part * MIX_W:(part + 1) * MIX_W])

    d_q = project(wd_ref, 0)
    d_k = project(wd_ref, 1)
    d_q_ss = _head_sumsq(d_q, bd)
    c_c = project(wc_ref, 1)
    d_k_ss = _head_sumsq(d_k, bd)
    c_h = project(wc_ref, 2)
    c_b = project(wc_ref, 0)
    d_v = project(wd_ref, 2)

    u = c_c * c_h
    prev = carry_ref[...]
    carry_ref[...] = u[tm - 8:, :]
    row = lax.broadcasted_iota(jnp.int32, u.shape, 0)
    u1 = jnp.where(row == 0, prev[7:8], pltpu.roll(u, 1, axis=0))
    u2 = jnp.where(row == 0, prev[6:7], jnp.where(row == 1, prev[7:8], pltpu.roll(u, 2, axis=0)))
    cw = cw_ref[...]
    conv = cw[0:1] * u2 + cw[1:2] * u1 + cw[2:3] * u
    c_ref[...] = (c_b * conv).astype(BF16)

    dq_ref[...] = (_head_norm(d_q, d_q_ss, gains[0:1]) * Q_SCALE).astype(BF16)
    dk_ref[...] = _head_norm(d_k, d_k_ss, gains[1:2]).astype(BF16)
    _store_transposed(dvt_ref, d_v)


def _odd_prep(x, gain, w_in, conv_w, qk_gains, bsz, seq):
    t, d = x.shape
    tm = TOKEN_TILE
    w3 = 3 * MIX_W
    n_s = seq // tm
    tok = lambda w: pl.BlockSpec((tm, w), lambda b, s: (b * n_s + s, 0))
    stream = jax.ShapeDtypeStruct((t, MIX_W), BF16)
    return pl.pallas_call(
        _odd_prep_kernel,
        out_shape=[stream] * 3 + [jax.ShapeDtypeStruct((bsz, seq // ATT_TILE, MIX_W, ATT_TILE), BF16)],
        grid=(bsz, n_s),
        in_specs=[tok(d), _const_spec((1, d)), _const_spec((d, w3)), _const_spec((d, w3)),
                  _const_spec((3, MIX_W)), _const_spec((2, MIX_W)), _const_spec((BD_WIDTH, BD_WIDTH))],
        out_specs=[tok(MIX_W)] * 3 + [pl.BlockSpec((None, tm // ATT_TILE, MIX_W, ATT_TILE),
                                                   lambda b, s: (b, s, 0, 0))],
        scratch_shapes=[pltpu.VMEM((8, MIX_W), F32)],
        compiler_params=_params(2),
        name="odd_prep",
    )(x, gain.reshape(1, d), w_in[:, :w3].astype(BF16), w_in[:, w3:].astype(BF16), conv_w,
      _tile_heads(qk_gains), _block_diag_ones())


BAND_GROUPS = 1 + D_LEFT // ATT_TILE


MAX_BAND_TILES = 8
BAND_WIDTH = BAND_GROUPS * ATT_TILE
BIAS_ROW = BAND_WIDTH + ATT_TILE


def _band_kernel(q_ref, k_ref, vt_ref, w_ref, o_ref, bias_ref):
    tq = ATT_TILE
    BAND_TILES = q_ref.shape[0] // tq
    j = pl.program_id(2)

    @pl.when((pl.program_id(1) == 0) & (j == 0))
    def _():
        key = lax.broadcasted_iota(jnp.int32, (BAND_WIDTH, tq), 0)
        chunk_start = lax.broadcasted_iota(jnp.int32, (BAND_WIDTH, tq), 1) // CHUNK * CHUNK
        in_band = (key >= chunk_start) & (key < chunk_start + D_LEFT + CHUNK)
        for sub in range(2):
            rows = jnp.broadcast_to(w_ref[sub], (BAND_WIDTH, BIAS_ROW))
            skew = pltpu.roll(rows, tq + 1, axis=1, stride=1, stride_axis=0)
            table = jnp.where(in_band, skew[:, :tq] * LOG2E, NEG)
            for grp in range(BAND_GROUPS):
                bias_ref[sub, grp] = table[grp * tq:(grp + 1) * tq, :]

    lo = _lane_lo((tq, LANES))
    before_start = jnp.where(j == 0, NEG, 0.0)
    q_subs, items = {}, []
    for tile in range(BAND_TILES):
        q = q_ref[tile * tq:(tile + 1) * tq, :]
        zero = jnp.zeros_like(q)
        q_subs[tile] = (jnp.where(lo, q, zero), jnp.where(lo, zero, q))
        items += [(tile, grp, sub) for grp in range(BAND_GROUPS) for sub in range(2)]

    def may_precede(tile, grp):
        return tile + grp < BAND_GROUPS - 1

    def group_index(tile, grp):
        g = BAND_TILES * j + tile + grp - (BAND_GROUPS - 1)
        return jnp.maximum(g, 0) if may_precede(tile, grp) else g

    def scores(tile, grp, sub):
        s = _dot_nt(k_ref[_key_rows(group_index(tile, grp)), :], q_subs[tile][sub]) + bias_ref[sub, grp]
        return s + before_start if may_precede(tile, grp) else s

    maxes, accs = {}, {}

    def exponentials(tile, grp, sub, s):
        mx = jnp.max(s, axis=0, keepdims=True)
        m_old = maxes.get((tile, sub))
        m_new = mx if m_old is None else jnp.maximum(m_old, mx)
        maxes[tile, sub] = m_new
        return jnp.exp2(s - m_new).astype(BF16), None if m_old is None else jnp.exp2(m_old - m_new)

    def accumulate(tile, grp, sub, p, alpha):
        value_t = _with_sum_rows(vt_ref[group_index(tile, grp), sub * HEAD_DIM:(sub + 1) * HEAD_DIM, :])
        acc_n = _dot(value_t, p)
        accs[tile, sub] = acc_n if alpha is None else alpha * accs[tile, sub] + acc_n
        if grp == BAND_GROUPS - 1 and sub == 1:
            o_t = jnp.concatenate([_normalise(accs[tile, s], HEAD_DIM) for s in range(2)], axis=0)
            o_ref[tile * tq:(tile + 1) * tq, :] = o_t.T.astype(o_ref.dtype)

    in_flight = []
    for item in items:
        in_flight.append((item, exponentials(*item, scores(*item))))
        if len(in_flight) > MXU_LOOKAHEAD:
            done, (p, alpha) = in_flight.pop(0)
            accumulate(*done, p, alpha)
    for done, (p, alpha) in in_flight:
        accumulate(*done, p, alpha)


def _band_bias_rows(rel_table):
    n_lo = ATT_TILE - 1 - REL_CLIP
    n_hi = BIAS_ROW - n_lo - (2 * REL_CLIP + 1)
    w = jnp.concatenate([jnp.repeat(rel_table[:, :1], n_lo, axis=1), rel_table,
                         jnp.repeat(rel_table[:, -1:], n_hi, axis=1)], axis=1)
    return w.astype(F32).reshape(rel_table.shape[0], 1, BIAS_ROW)


def _band_attention(q, k, vt, rel_table, bsz, seq):
    tq = ATT_TILE
    n_q = seq // tq
    BAND_TILES = min(MAX_BAND_TILES, n_q)
    q3, k3 = (a.reshape(bsz, seq, MIX_W) for a in (q, k))
    q_spec = pl.BlockSpec((None, BAND_TILES * tq, LANES), lambda p, b, j: (b, j, p))
    out = pl.pallas_call(
        _band_kernel,
        out_shape=jax.ShapeDtypeStruct((bsz, seq, MIX_W), BF16),
        grid=(MIX_W // LANES, bsz, n_q // BAND_TILES),
        in_specs=[q_spec,
                  pl.BlockSpec((None, seq, LANES), lambda p, b, j: (b, 0, p)),
                  pl.BlockSpec((None, n_q, LANES, tq), lambda p, b, j: (b, 0, p, 0)),
                  pl.BlockSpec((2, 1, BIAS_ROW), lambda p, b, j: (p, 0, 0))],
        out_specs=q_spec,
        scratch_shapes=[pltpu.VMEM((2, BAND_GROUPS, tq, tq), F32)],
        compiler_params=_params(3),
        name="band_attention",
    )(q3, k3, vt, _band_bias_rows(rel_table))
    return out.reshape(bsz * seq, MIX_W)


def _mem_kv_kernel(mem_ref, g_ref, w_ref, kg_ref, k_ref, v_ref):
    mem_n = (_rms(mem_ref[...]) * g_ref[...]).astype(BF16)
    kv = _dot(mem_n, w_ref[...])
    kg = kg_ref[...]
    for hd in range(X_HEADS):
        sl = slice(hd * X_HEAD_DIM, (hd + 1) * X_HEAD_DIM)
        k_ref[:, sl] = (_rms(kv[:, sl]) * kg).astype(BF16)
    v_ref[...] = kv[:, D_MODEL:].astype(BF16)


def _mem_kv(mem, gain, w_kv, k_gain):
    bsz, n_mem, d = mem.shape
    blk = pl.BlockSpec((None, n_mem, d), lambda b: (b, 0, 0))
    out = jax.ShapeDtypeStruct((bsz, n_mem, d), BF16)
    return pl.pallas_call(
        _mem_kv_kernel,
        out_shape=[out, out],
        grid=(bsz,),
        in_specs=[blk, _const_spec((1, d)), _const_spec((d, 2 * d)), _const_spec((1, X_HEAD_DIM))],
        out_specs=[blk, blk],
        compiler_params=_params(1),
        name="mem_kv",
    )(mem, gain.reshape(1, d), w_kv.astype(BF16), k_gain.reshape(1, X_HEAD_DIM))


def _cross_kernel(x_ref, left_a_ref, left_b_ref, right_a_ref, right_b_ref, wl_ref, wr_ref, g_ref,
                  wq_ref, qg_ref, k_ref, v_ref, wo_ref, o_ref):
    first_half = pl.program_id(1) < pl.num_programs(1) // 2
    left = jnp.where(first_half, left_a_ref[...], left_b_ref[...])
    right = jnp.where(first_half, right_a_ref[...], right_b_ref[...])
    x = x_ref[...] + _dot(left, wl_ref[...]) + _dot(right, wr_ref[...])
    h = (_rms(x) * g_ref[...]).astype(BF16)
    q = _dot(h, wq_ref[...])
    qg = qg_ref[...]
    head_cols = [slice(hd * X_HEAD_DIM, (hd + 1) * X_HEAD_DIM) for hd in range(X_HEADS)]

    def probabilities(sl):
        qh = (_rms(q[:, sl]) * qg * X_HEAD_DIM ** -0.5).astype(BF16)
        s = _dot_nt(qh, k_ref[:, sl])
        p = jnp.exp(s - jnp.max(s, axis=-1, keepdims=True))
        return p.astype(BF16), jnp.sum(p, axis=-1, keepdims=True)

    heads = []
    ahead = probabilities(head_cols[0])
    for hd, sl in enumerate(head_cols):
        p, l = ahead
        if hd + 1 < X_HEADS:
            ahead = probabilities(head_cols[hd + 1])
        heads.append((_dot(p, v_ref[:, sl]) / l).astype(BF16))
    o_ref[...] = x + _dot(jnp.concatenate(heads, axis=1), wo_ref[...])


def _halves(stream, bsz, seq):
    if isinstance(stream, tuple):
        return stream[0], stream[1], 0
    full = stream.reshape(bsz, seq, MIX_W)
    return full, full, seq // TOKEN_TILE // 2


def _mix_out_cross_attention(x, left, right, w_mix_out, gain, w_q, q_gain, k, v, w_o, bsz, seq):
    t, d = x.shape
    tq = TOKEN_TILE
    n_s = seq // tq
    half = n_s // 2
    n_mem = k.shape[1]
    tok = pl.BlockSpec((tq, d), lambda b, s: (b * n_s + s, 0))
    mem_spec = pl.BlockSpec((None, n_mem, d), lambda b, s: (b, 0, 0))
    w_mix = w_mix_out.astype(BF16)
    stream_args, stream_specs = [], []
    for first, second, offset in (_halves(left, bsz, seq), _halves(right, bsz, seq)):
        stream_args += [first, second]
        stream_specs += [
            pl.BlockSpec((None, tq, MIX_W), lambda b, s: (b, jnp.minimum(s, half - 1), 0)),
            pl.BlockSpec((None, tq, MIX_W),
                         lambda b, s, offset=offset: (b, offset + jnp.maximum(s - half, 0), 0))]
    return pl.pallas_call(
        _cross_kernel,
        out_shape=jax.ShapeDtypeStruct((t, d), F32),
        grid=(bsz, n_s),
        in_specs=[tok, *stream_specs, _const_spec((MIX_W, d)), _const_spec((MIX_W, d)),
                  _const_spec((1, d)), _const_spec((d, d)), _const_spec((1, X_HEAD_DIM)),
                  mem_spec, mem_spec, _const_spec((d, d))],
        out_specs=tok,
        compiler_params=_params(2),
        name="cross_attention",
    )(x, *stream_args, w_mix[:MIX_W], w_mix[MIX_W:], gain.reshape(1, d), w_q.astype(BF16),
      q_gain.reshape(1, X_HEAD_DIM), k, v, w_o.astype(BF16))


def kernel(x, mem, positions, ln_gains, ffn1_w_in, ffn1_w_out, ffn2_w_in, ffn2_w_out, even_w_in, even_f_bias, even_qk_gains, even_lambda, even_subln_gain, even_w_out, odd_w_in, odd_conv_w, odd_qk_gains, odd_rel_bias, odd_w_out, x_w_q, x_w_kv, x_qk_gains, x_w_o):
    bsz, seq, d = x.shape
    depth = ln_gains.shape[0]
    rope = _rope_tables(positions)
    x = x.reshape(bsz * seq, d)
    for layer in range(depth):
        g = ln_gains[layer]
        x = _ffn(x, g[0], ffn1_w_in[layer], ffn1_w_out[layer])
        if layer % 2 == 0:
            e = layer // 2
            lambda_init = 0.8 - 0.6 * math.exp(-0.3 * layer)
            aq, ak, avt, bq, bk, bvt = _even_prep(
                x, g[1], even_w_in[e], even_f_bias[e], even_qk_gains[e], rope, bsz, seq)
            left = _fox_attention(aq, ak, avt, bsz, seq)
            right = _diff_attention(bq, bk, bvt, even_lambda[e], even_subln_gain[e], lambda_init,
                                    bsz, seq)
            w_mix_out = even_w_out[e]
        else:
            o = layer // 2
            left, dq, dk, dvt = _odd_prep(x, g[1], odd_w_in[o], odd_conv_w[o], odd_qk_gains[o],
                                          bsz, seq)
            right = _band_attention(dq, dk, dvt, odd_rel_bias[o], bsz, seq)
            w_mix_out = odd_w_out[o]
        mk, mv = _mem_kv(mem, g[3], x_w_kv[layer], x_qk_gains[layer, 1])
        x = _mix_out_cross_attention(x, left, right, w_mix_out, g[2], x_w_q[layer],
                                     x_qk_gains[layer, 0], mk, mv, x_w_o[layer], bsz, seq)
        x = _ffn(x, g[4], ffn2_w_in[layer], ffn2_w_out[layer])
    return x.reshape(bsz, seq, d)
```

```python
import functools
import math

import jax
import jax.numpy as jnp
from jax import lax
from jax.experimental import pallas as pl
from jax.experimental.pallas import tpu as pltpu

D_MODEL = 1024
CHUNK = 64
HEAD_DIM = 64
ROPE_DIM = HEAD_DIM // 4
ROPE_THETA = 500000.0
RMS_EPS = 1e-6
D_FF = 2816
A_HEADS = 8
MIX_W = 512
LANES = 128
D_LEFT = 8 * CHUNK
REL_CLIP = 128
X_HEADS = 4
X_HEAD_DIM = D_MODEL // X_HEADS
NEG = -1e30
LOG2E = math.log2(math.e)
Q_SCALE = HEAD_DIM ** -0.5 * LOG2E

VMEM_LIMIT = 56 * 1024 * 1024
FFN_CHUNKS = (768, 768, 768, 512)
TOKEN_TILE = 512
ATT_TILE = 256

BF16 = jnp.bfloat16
F32 = jnp.float32


def _params(n_axes):
    return pltpu.CompilerParams(dimension_semantics=("arbitrary",) * n_axes,
                                vmem_limit_bytes=VMEM_LIMIT)


def _const_spec(shape):
    nd = len(shape)
    return pl.BlockSpec(shape, lambda *_: (0,) * nd, pipeline_mode=pl.Buffered(1))


def _rms(x):
    return x * lax.rsqrt(jnp.mean(x * x, axis=-1, keepdims=True) + RMS_EPS)


def _dot(a, b):
    return jnp.dot(a, b, preferred_element_type=F32)


def _dot_nt(a, b):
    return lax.dot_general(a, b, (((1,), (1,)), ((), ())), preferred_element_type=F32)


def _ffn_kernel(x_ref, g_ref, wi_ref, wo_ref, o_ref):
    n_tiles = x_ref.shape[0] // TOKEN_TILE
    bounds = [sum(FFN_CHUNKS[:c]) for c in range(len(FFN_CHUNKS) + 1)]
    chunks = list(zip(bounds[:-1], bounds[1:]))

    def rows(i):
        return slice(i * TOKEN_TILE, (i + 1) * TOKEN_TILE)

    def normed(i):
        x = x_ref[rows(i), :]
        return x, (_rms(x) * g_ref[...]).astype(BF16)

    def chunk_out(xn, cols):
        lo, hi = cols
        gate = _dot(xn, wi_ref[:, lo:hi])
        up = _dot(xn, wi_ref[:, D_FF + lo:D_FF + hi])
        act = (gate * (1.0 / (1.0 + jnp.exp(-gate))) * up).astype(BF16)
        return _dot(act, wo_ref[lo:hi, :])

    x, xn = normed(0)
    previous = None
    for i in range(n_tiles):
        y = chunk_out(xn, chunks[0])
        if previous is not None:
            o_ref[rows(i - 1), :] = previous[0] + 0.5 * previous[1]
        for c, cols in enumerate(chunks[1:], start=1):
            y = y + chunk_out(xn, cols)
            if c == 1 and i + 1 < n_tiles:
                following = normed(i + 1)
        previous = (x, y)
        if i + 1 < n_tiles:
            x, xn = following
    o_ref[rows(n_tiles - 1), :] = previous[0] + 0.5 * previous[1]


FFN_TILES_PER_STEP = 2


def _ffn(x, gain, w_in, w_out):
    t, d = x.shape
    tm = TOKEN_TILE * FFN_TILES_PER_STEP
    return pl.pallas_call(
        _ffn_kernel,
        out_shape=jax.ShapeDtypeStruct((t, d), F32),
        grid=(t // tm,),
        in_specs=[pl.BlockSpec((tm, d), lambda i: (i, 0)),
                  _const_spec((1, d)), _const_spec((d, 2 * D_FF)), _const_spec((D_FF, d))],
        out_specs=pl.BlockSpec((tm, d), lambda i: (i, 0)),
        compiler_params=_params(1),
        name="ffn",
    )(x, gain.reshape(1, d), w_in.astype(BF16), w_out.astype(BF16))


BD_WIDTH = 256


def _head_sumsq(y, ones_bd):
    sq = (y * y).astype(BF16)
    return jnp.concatenate([_dot(sq[:, c:c + BD_WIDTH], ones_bd) for c in range(0, MIX_W, BD_WIDTH)],
                           axis=1)


def _head_norm(y, sumsq, gain):
    return y * lax.rsqrt(sumsq * (1.0 / HEAD_DIM) + RMS_EPS) * gain


def _block_diag_ones():
    r = jnp.arange(BD_WIDTH) // HEAD_DIM
    return (r[:, None] == r[None, :]).astype(BF16)


def _tile_heads(v):
    return jnp.tile(v, (1, MIX_W // HEAD_DIM))


def _store_transposed(vt_ref, v):
    for i in range(v.shape[0] // ATT_TILE):
        vt_ref[i] = v[i * ATT_TILE:(i + 1) * ATT_TILE, :].T.astype(BF16)


def _split3(f):
    hi = f.astype(BF16).astype(F32)
    rest = f - hi
    mid = rest.astype(BF16).astype(F32)
    return hi, mid, rest - mid


def _even_prep_kernel(x_ref, g_ref, wa_ref, wf_ref, fb_ref, wb_ref, gains_ref, bd_ref,
                      ra_ref, rm_ref, rp_ref,
                      aq_ref, ak_ref, avt_ref, bq_ref, bk_ref, bvt_ref, carry_ref):
    tm = x_ref.shape[0]

    @pl.when(pl.program_id(1) == 0)
    def _():
        carry_ref[...] = jnp.zeros_like(carry_ref)

    h = (_rms(x_ref[...]) * g_ref[...]).astype(BF16)
    bd = bd_ref[...]
    gains = gains_ref[...]

    def project(w_ref, part):
        return _dot(h, w_ref[:, part * MIX_W:(part + 1) * MIX_W])

    z = _dot(h, wf_ref[...]) + fb_ref[...]
    a_q = project(wa_ref, 0)

    logf = jnp.minimum(z, 0.0) - jnp.log1p(jnp.exp(-jnp.abs(z)))
    row = lax.broadcasted_iota(jnp.int32, logf.shape, 0)
    step = 1
    while step < tm:
        logf = logf + jnp.where(row >= step, pltpu.roll(logf, step, axis=0), 0.0)
        step *= 2
    cum = logf + carry_ref[...]
    carry_ref[...] = cum[tm - 1:tm, :]
    hi, mid, lo = _split3(cum * LOG2E)

    a_k = project(wa_ref, 1)
    a_q_ss = _head_sumsq(a_q, bd)
    b_q = project(wb_ref, 0)
    qn = _head_norm(a_q, a_q_ss, gains[0:1]) * Q_SCALE
    a_k_ss = _head_sumsq(a_k, bd)
    b_k = project(wb_ref, 1)
    kn = _head_norm(a_k, a_k_ss, gains[1:2])

    lane = lax.broadcasted_iota(jnp.int32, (tm, LANES), 1)
    is_hi = (lane == HEAD_DIM) | (lane == HEAD_DIM + 3)
    is_mid = (lane == HEAD_DIM + 1) | (lane == HEAD_DIM + 4)
    ones_hi = jnp.where(lane < HEAD_DIM + 6, 1.0, 0.0)
    for hd in range(A_HEADS):
        if hd == A_HEADS // 2:
            b_q_ss = _head_sumsq(b_q, bd)
            a_v = project(wa_ref, 2)
        pieces = jnp.where(is_hi, hi[:, hd:hd + 1],
                           jnp.where(is_mid, mid[:, hd:hd + 1], lo[:, hd:hd + 1]))
        blk = slice((hd // 2) * LANES, (hd // 2 + 1) * LANES)
        sq, sk = qn[:, blk], kn[:, blk]
        if hd % 2:
            sq, sk = pltpu.roll(sq, HEAD_DIM, axis=1), pltpu.roll(sk, HEAD_DIM, axis=1)
        qa = jnp.where(lane < HEAD_DIM, sq, jnp.where(lane < HEAD_DIM + 3, pieces, ones_hi))
        ka = jnp.where(lane < HEAD_DIM, sk, jnp.where(lane < HEAD_DIM + 3, 1.0,
                                                      jnp.where(lane < HEAD_DIM + 6, -pieces, 0.0)))
        aq_ref[:, hd * LANES:(hd + 1) * LANES] = qa.astype(BF16)
        ak_ref[:, hd * LANES:(hd + 1) * LANES] = ka.astype(BF16)

    rep = MIX_W // LANES
    ra = jnp.tile(ra_ref[...], (1, rep))
    rm = jnp.tile(rm_ref[...], (1, rep))
    rp = jnp.tile(rp_ref[...], (1, rep))

    def rope(v):
        half = ROPE_DIM // 2
        return v * ra + pltpu.roll(v, MIX_W - half, axis=1) * rm + pltpu.roll(v, half, axis=1) * rp

    bq_ref[...] = (rope(_head_norm(b_q, b_q_ss, gains[2:3])) * Q_SCALE).astype(BF16)
    b_k_ss = _head_sumsq(b_k, bd)
    _store_transposed(avt_ref, a_v)
    b_v = project(wb_ref, 2)
    bk_ref[...] = rope(_head_norm(b_k, b_k_ss, gains[3:4])).astype(BF16)
    _store_transposed(bvt_ref, b_v)


def _rope_tables(positions):
    half = ROPE_DIM // 2
    inv = ROPE_THETA ** (-jnp.arange(0, ROPE_DIM, 2, dtype=F32) / ROPE_DIM)
    ang = positions.astype(F32).reshape(-1, 1) * inv
    cos = jnp.tile(jnp.cos(ang), (1, LANES // half))
    sin = jnp.tile(jnp.sin(ang), (1, LANES // half))
    in_head = jnp.arange(LANES) % HEAD_DIM
    ra = jnp.where(in_head < ROPE_DIM, cos, 1.0)
    rm = jnp.where(in_head < half, -sin, 0.0)
    rp = jnp.where((in_head >= half) & (in_head < ROPE_DIM), sin, 0.0)
    return ra, rm, rp


def _even_prep(x, gain, w_in, f_bias, qk_gains, rope, bsz, seq):
    t, d = x.shape
    tm = TOKEN_TILE
    a_w = 3 * MIX_W
    wa = w_in[:, :a_w].astype(BF16)
    wf = jnp.pad(w_in[:, a_w:a_w + A_HEADS], ((0, 0), (0, LANES - A_HEADS))).astype(BF16)
    wb = w_in[:, a_w + A_HEADS:].astype(BF16)
    fb = jnp.pad(f_bias, (0, LANES - A_HEADS)).reshape(1, LANES)
    n_s = seq // tm
    tok = lambda w: pl.BlockSpec((tm, w), lambda b, s: (b * n_s + s, 0))
    vt_spec = pl.BlockSpec((None, tm // ATT_TILE, MIX_W, ATT_TILE), lambda b, s: (b, s, 0, 0))
    stream = lambda w: jax.ShapeDtypeStruct((t, w), BF16)
    vt_shape = jax.ShapeDtypeStruct((bsz, seq // ATT_TILE, MIX_W, ATT_TILE), BF16)
    return pl.pallas_call(
        _even_prep_kernel,
        out_shape=[stream(A_HEADS * LANES), stream(A_HEADS * LANES), vt_shape,
                   stream(MIX_W), stream(MIX_W), vt_shape],
        grid=(bsz, n_s),
        in_specs=[tok(d), _const_spec((1, d)), _const_spec((d, a_w)), _const_spec((d, LANES)),
                  _const_spec((1, LANES)), _const_spec((d, a_w)), _const_spec((4, MIX_W)),
                  _const_spec((BD_WIDTH, BD_WIDTH)), tok(LANES), tok(LANES), tok(LANES)],
        out_specs=[tok(A_HEADS * LANES), tok(A_HEADS * LANES), vt_spec,
                   tok(MIX_W), tok(MIX_W), vt_spec],
        scratch_shapes=[pltpu.VMEM((1, LANES), F32)],
        compiler_params=_params(2),
        name="even_prep",
    )(x, gain.reshape(1, d), wa, wf, fb, wb, _tile_heads(qk_gains), _block_diag_ones(), *rope)


def _lane_lo(shape):
    return lax.broadcasted_iota(jnp.int32, shape, len(shape) - 1) < HEAD_DIM


def _key_rows(g):
    if isinstance(g, int):
        return slice(g * ATT_TILE, (g + 1) * ATT_TILE)
    return pl.ds(pl.multiple_of(g * ATT_TILE, ATT_TILE), ATT_TILE)


SUM_ROWS = 16
MXU_LOOKAHEAD = 5


def _with_sum_rows(value_t):
    return jnp.concatenate([value_t, jnp.ones((SUM_ROWS, value_t.shape[1]), BF16)], axis=0)


def _normalise(acc, v_rows):
    return acc[:v_rows] / acc[v_rows:v_rows + 1]


def _causal_attention(tiles, n_sub, queries, key_block, value_t, diag_visible, v_rows, finish):
    tq = ATT_TILE
    items = [(i, g, sub) for i, tile in enumerate(tiles) for g in range(tile + 1)
             for sub in range(n_sub)]
    maxes = [[jnp.full((1, tq), NEG, F32)] * n_sub for _ in tiles]
    accs = [[jnp.zeros((v_rows + SUM_ROWS, tq), F32)] * n_sub for _ in tiles]

    def scores(i, g, sub):
        s = _dot_nt(key_block(g, sub), queries[i][sub])
        return jnp.where(diag_visible, s, NEG) if g == tiles[i] else s

    def exponentials(i, g, sub, s):
        m_old = maxes[i][sub]
        m_new = jnp.maximum(m_old, jnp.max(s, axis=0, keepdims=True))
        maxes[i][sub] = m_new
        return jnp.exp2(s - m_new).astype(BF16), jnp.exp2(m_old - m_new)

    def accumulate(i, g, sub, p, alpha):
        accs[i][sub] = alpha * accs[i][sub] + _dot(_with_sum_rows(value_t(g, sub)), p)
        if g == tiles[i] and sub == n_sub - 1:
            finish(i, accs[i])

    in_flight = []
    for item in items:
        in_flight.append((item, exponentials(*item, scores(*item))))
        if len(in_flight) > MXU_LOOKAHEAD:
            done, (p, alpha) = in_flight.pop(0)
            accumulate(*done, p, alpha)
    for done, (p, alpha) in in_flight:
        accumulate(*done, p, alpha)


HEADS_PER_STEP = 4
TILES_PER_HALF = 2


def _step_tiles(n_tiles, j):
    first = [TILES_PER_HALF * j + i for i in range(TILES_PER_HALF)]
    return first + [n_tiles - 1 - t for t in reversed(first)]


def _for_each_step(n_tiles, body):
    for j in range(n_tiles // (2 * TILES_PER_HALF)):
        pl.when(pl.program_id(2) == j)(functools.partial(body, _step_tiles(n_tiles, j)))


def _tile_views(front_ref, back_ref):
    rows = [slice(i * ATT_TILE, (i + 1) * ATT_TILE) for i in range(TILES_PER_HALF)]
    return [(front_ref, r) for r in rows] + [(back_ref, r) for r in rows]


def _mirrored_specs(n_tiles, q_width, out_width):
    rows = TILES_PER_HALF * ATT_TILE
    n_steps = n_tiles // (2 * TILES_PER_HALF)
    q_front = pl.BlockSpec((None, rows, q_width), lambda b, p, j: (b, j, p))
    q_back = pl.BlockSpec((None, rows, q_width), lambda b, p, j: (b, 2 * n_steps - 1 - j, p))
    o_front = pl.BlockSpec((None, rows, out_width), lambda b, p, j: (b, j, p))
    o_back = pl.BlockSpec((None, rows, out_width), lambda b, p, j: (b, n_steps - 1 - j, p))
    return q_front, q_back, o_front, o_back


def _fox_kernel(qa_ref, qb_ref, k_ref, vt_ref, oa_ref, ob_ref, *, n_tiles):
    tq = ATT_TILE

    def key_block(g, sub):
        return k_ref[_key_rows(g), sub * LANES:(sub + 1) * LANES]

    def value_t(g, sub):
        return vt_ref[g, sub * HEAD_DIM:(sub + 1) * HEAD_DIM, :]

    def body(tiles):
        queries = [[q_ref[rows, sub * LANES:(sub + 1) * LANES] for sub in range(HEADS_PER_STEP)]
                   for q_ref, rows in _tile_views(qa_ref, qb_ref)]
        outputs = _tile_views(oa_ref, ob_ref)
        causal = (lax.broadcasted_iota(jnp.int32, (tq, tq), 0)
                  <= lax.broadcasted_iota(jnp.int32, (tq, tq), 1))

        def finish(i, accs):
            o_ref, rows = outputs[i]
            o_t = jnp.concatenate([_normalise(acc, HEAD_DIM) for acc in accs], axis=0)
            o_ref[rows, :] = o_t.T.astype(o_ref.dtype)

        _causal_attention(tiles, HEADS_PER_STEP, queries, key_block, value_t, causal, HEAD_DIM,
                          finish)

    _for_each_step(n_tiles, body)


def _fox_attention(q, k, vt, bsz, seq):
    tq = ATT_TILE
    n_tiles = seq // tq
    q3 = q.reshape(bsz, seq, A_HEADS * LANES)
    k3 = k.reshape(bsz, seq, A_HEADS * LANES)
    out_width = HEADS_PER_STEP * HEAD_DIM
    q_a, q_b, o_a, o_b = _mirrored_specs(n_tiles, HEADS_PER_STEP * LANES, out_width)
    half_out = jax.ShapeDtypeStruct((bsz, seq // 2, MIX_W), BF16)
    out_a, out_b = pl.pallas_call(
        functools.partial(_fox_kernel, n_tiles=n_tiles),
        out_shape=[half_out, half_out],
        grid=(bsz, A_HEADS // HEADS_PER_STEP, n_tiles // (2 * TILES_PER_HALF)),
        in_specs=[q_a, q_b,
                  pl.BlockSpec((None, seq, HEADS_PER_STEP * LANES), lambda b, p, a: (b, 0, p)),
                  pl.BlockSpec((None, n_tiles, out_width, tq), lambda b, p, a: (b, 0, p, 0))],
        out_specs=[o_a, o_b],
        compiler_params=_params(3),
        name="fox_attention",
    )(q3, q3, k3, vt)
    return out_a, out_b


def _diff_kernel(qa_ref, qb_ref, k_ref, vt_ref, lam_ref, sg_ref, oa_ref, ob_ref, *,
                 lambda_init, n_tiles):
    tq = ATT_TILE
    lo = _lane_lo((tq, LANES))
    lp = lam_ref[...]
    lam = (jnp.exp(jnp.sum(lp[0:1] * lp[1:2], axis=1, keepdims=True))
           - jnp.exp(jnp.sum(lp[2:3] * lp[3:4], axis=1, keepdims=True)) + lambda_init)

    def half_of(q_ref, rows, sub):
        q = q_ref[rows, sub // 2 * LANES:(sub // 2 + 1) * LANES]
        zero = jnp.zeros_like(q)
        return jnp.where(lo, q, zero) if sub % 2 == 0 else jnp.where(lo, zero, q)

    def key_block(g, sub):
        return k_ref[_key_rows(g), sub // 2 * LANES:(sub // 2 + 1) * LANES]

    def value_t(g, sub):
        return vt_ref[g, sub // 2 * LANES:(sub // 2 + 1) * LANES, :]

    def body(tiles):
        queries = [[half_of(q_ref, rows, sub) for sub in range(HEADS_PER_STEP)]
                   for q_ref, rows in _tile_views(qa_ref, qb_ref)]
        outputs = _tile_views(oa_ref, ob_ref)
        chunk_causal = (lax.broadcasted_iota(jnp.int32, (tq, tq), 0) // CHUNK
                        <= lax.broadcasted_iota(jnp.int32, (tq, tq), 1) // CHUNK)

        def finish(i, accs):
            o_ref, rows = outputs[i]
            for hd in range(HEADS_PER_STEP // 2):
                o1, o2 = (_normalise(acc, LANES) for acc in accs[2 * hd:2 * hd + 2])
                o = (o1 - lam * o2).T
                o_ref[rows, hd * LANES:(hd + 1) * LANES] = (
                    _rms(o) * sg_ref[...] * (1.0 - lambda_init)).astype(o_ref.dtype)

        _causal_attention(tiles, HEADS_PER_STEP, queries, key_block, value_t, chunk_causal, LANES,
                          finish)

    _for_each_step(n_tiles, body)


def _diff_attention(q, k, vt, lam_params, subln_gain, lambda_init, bsz, seq):
    tq = ATT_TILE
    n_tiles = seq // tq
    q3, k3 = (a.reshape(bsz, seq, MIX_W) for a in (q, k))
    width = HEADS_PER_STEP // 2 * LANES
    q_a, q_b, o_a, o_b = _mirrored_specs(n_tiles, width, width)
    half_out = jax.ShapeDtypeStruct((bsz, seq // 2, MIX_W), BF16)
    out_a, out_b = pl.pallas_call(
        functools.partial(_diff_kernel, lambda_init=lambda_init, n_tiles=n_tiles),
        out_shape=[half_out, half_out],
        grid=(bsz, MIX_W // width, n_tiles // (2 * TILES_PER_HALF)),
        in_specs=[q_a, q_b,
                  pl.BlockSpec((None, seq, width), lambda b, h, a: (b, 0, h)),
                  pl.BlockSpec((None, n_tiles, width, tq), lambda b, h, a: (b, 0, h, 0)),
                  _const_spec((4, HEAD_DIM)), _const_spec((1, LANES))],
        out_specs=[o_a, o_b],
        compiler_params=_params(3),
        name="diff_attention",
    )(q3, q3, k3, vt, lam_params, subln_gain.reshape(1, LANES))
    return out_a, out_b


def _odd_prep_kernel(x_ref, g_ref, wc_ref, wd_ref, cw_ref, gains_ref, bd_ref,
                     c_ref, dq_ref, dk_ref, dvt_ref, carry_ref):
    tm = x_ref.shape[0]

    @pl.when(pl.program_id(1) == 0)
    def _():
        carry_ref[...] = jnp.zeros_like(carry_ref)

    h = (_rms(x_ref[...]) * g_ref[...]).astype(BF16)
    bd = bd_ref[...]
    gains = gains_ref[...]

    def project(w_ref, part):
        return _dot(h, w_ref[:, part * MIX_W:(part + 1) * MIX_W])

    d_q = project(wd_ref, 0)
    d_k = project(wd_ref, 1)
    d_q_ss = _head_sumsq(d_q, bd)
    c_c = project(wc_ref, 1)
    d_k_ss = _head_sumsq(d_k, bd)
    c_h = project(wc_ref, 2)
    c_b = project(wc_ref, 0)
    d_v = project(wd_ref, 2)

    u = c_c * c_h
    prev = carry_ref[...]
    carry_ref[...] = u[tm - 8:, :]
    row = lax.broadcasted_iota(jnp.int32, u.shape, 0)
    u1 = jnp.where(row == 0, prev[7:8], pltpu.roll(u, 1, axis=0))
    u2 = jnp.where(row == 0, prev[6:7], jnp.where(row == 1, prev[7:8], pltpu.roll(u, 2, axis=0)))
    cw = cw_ref[...]
    conv = cw[0:1] * u2 + cw[1:2] * u1 + cw[2:3] * u
    c_ref[...] = (c_b * conv).astype(BF16)

    dq_ref[...] = (_head_norm(d_q, d_q_ss, gains[0:1]) * Q_SCALE).astype(BF16)
    dk_ref[...] = _head_norm(d_k, d_k_ss, gains[1:2]).astype(BF16)
    _store_transposed(dvt_ref, d_v)


def _odd_prep(x, gain, w_in, conv_w, qk_gains, bsz, seq):
    t, d = x.shape
    tm = TOKEN_TILE
    w3 = 3 * MIX_W
    n_s = seq // tm
    tok = lambda w: pl.BlockSpec((tm, w), lambda b, s: (b * n_s + s, 0))
    stream = jax.ShapeDtypeStruct((t, MIX_W), BF16)
    return pl.pallas_call(
        _odd_prep_kernel,
        out_shape=[stream] * 3 + [jax.ShapeDtypeStruct((bsz, seq // ATT_TILE, MIX_W, ATT_TILE), BF16)],
        grid=(bsz, n_s),
        in_specs=[tok(d), _const_spec((1, d)), _const_spec((d, w3)), _const_spec((d, w3)),
                  _const_spec((3, MIX_W)), _const_spec((2, MIX_W)), _const_spec((BD_WIDTH, BD_WIDTH))],
        out_specs=[tok(MIX_W)] * 3 + [pl.BlockSpec((None, tm // ATT_TILE, MIX_W, ATT_TILE),
                                                   lambda b, s: (b, s, 0, 0))],
        scratch_shapes=[pltpu.VMEM((8, MIX_W), F32)],
        compiler_params=_params(2),
        name="odd_prep",
    )(x, gain.reshape(1, d), w_in[:, :w3].astype(BF16), w_in[:, w3:].astype(BF16), conv_w,
      _tile_heads(qk_gains), _block_diag_ones())


BAND_GROUPS = 1 + D_LEFT // ATT_TILE


MAX_BAND_TILES = 8
BAND_WIDTH = BAND_GROUPS * ATT_TILE
BIAS_ROW = BAND_WIDTH + ATT_TILE


def _band_kernel(q_ref, k_ref, vt_ref, w_ref, o_ref, bias_ref):
    tq = ATT_TILE
    BAND_TILES = q_ref.shape[0] // tq
    j = pl.program_id(2)

    @pl.when((pl.program_id(1) == 0) & (j == 0))
    def _():
        key = lax.broadcasted_iota(jnp.int32, (BAND_WIDTH, tq), 0)
        chunk_start = lax.broadcasted_iota(jnp.int32, (BAND_WIDTH, tq), 1) // CHUNK * CHUNK
        in_band = (key >= chunk_start) & (key < chunk_start + D_LEFT + CHUNK)
        for sub in range(2):
            rows = jnp.broadcast_to(w_ref[sub], (BAND_WIDTH, BIAS_ROW))
            skew = pltpu.roll(rows, tq + 1, axis=1, stride=1, stride_axis=0)
            table = jnp.where(in_band, skew[:, :tq] * LOG2E, NEG)
            for grp in range(BAND_GROUPS):
                bias_ref[sub, grp] = table[grp * tq:(grp + 1) * tq, :]

    lo = _lane_lo((tq, LANES))
    before_start = jnp.where(j == 0, NEG, 0.0)
    q_subs, items = {}, []
    for tile in range(BAND_TILES):
        q = q_ref[tile * tq:(tile + 1) * tq, :]
        zero = jnp.zeros_like(q)
        q_subs[tile] = (jnp.where(lo, q, zero), jnp.where(lo, zero, q))
        items += [(tile, grp, sub) for grp in range(BAND_GROUPS) for sub in range(2)]

    def may_precede(tile, grp):
        return tile + grp < BAND_GROUPS - 1

    def group_index(tile, grp):
        g = BAND_TILES * j + tile + grp - (BAND_GROUPS - 1)
        return jnp.maximum(g, 0) if may_precede(tile, grp) else g

    def scores(tile, grp, sub):
        s = _dot_nt(k_ref[_key_rows(group_index(tile, grp)), :], q_subs[tile][sub]) + bias_ref[sub, grp]
        return s + before_start if may_precede(tile, grp) else s

    maxes, accs = {}, {}

    def exponentials(tile, grp, sub, s):
        mx = jnp.max(s, axis=0, keepdims=True)
        m_old = maxes.get((tile, sub))
        m_new = mx if m_old is None else jnp.maximum(m_old, mx)
        maxes[tile, sub] = m_new
        return jnp.exp2(s - m_new).astype(BF16), None if m_old is None else jnp.exp2(m_old - m_new)

    def accumulate(tile, grp, sub, p, alpha):
        value_t = _with_sum_rows(vt_ref[group_index(tile, grp), sub * HEAD_DIM:(sub + 1) * HEAD_DIM, :])
        acc_n = _dot(value_t, p)
        accs[tile, sub] = acc_n if alpha is None else alpha * accs[tile, sub] + acc_n
        if grp == BAND_GROUPS - 1 and sub == 1:
            o_t = jnp.concatenate([_normalise(accs[tile, s], HEAD_DIM) for s in range(2)], axis=0)
            o_ref[tile * tq:(tile + 1) * tq, :] = o_t.T.astype(o_ref.dtype)

    in_flight = []
    for item in items:
        in_flight.append((item, exponentials(*item, scores(*item))))
        if len(in_flight) > MXU_LOOKAHEAD:
            done, (p, alpha) = in_flight.pop(0)
            accumulate(*done, p, alpha)
    for done, (p, alpha) in in_flight:
        accumulate(*done, p, alpha)


def _band_bias_rows(rel_table):
    n_lo = ATT_TILE - 1 - REL_CLIP
    n_hi = BIAS_ROW - n_lo - (2 * REL_CLIP + 1)
    w = jnp.concatenate([jnp.repeat(rel_table[:, :1], n_lo, axis=1), rel_table,
                         jnp.repeat(rel_table[:, -1:], n_hi, axis=1)], axis=1)
    return w.astype(F32).reshape(rel_table.shape[0], 1, BIAS_ROW)


def _band_attention(q, k, vt, rel_table, bsz, seq):
    tq = ATT_TILE
    n_q = seq // tq
    BAND_TILES = min(MAX_BAND_TILES, n_q)
    q3, k3 = (a.reshape(bsz, seq, MIX_W) for a in (q, k))
    q_spec = pl.BlockSpec((None, BAND_TILES * tq, LANES), lambda p, b, j: (b, j, p))
    out = pl.pallas_call(
        _band_kernel,
        out_shape=jax.ShapeDtypeStruct((bsz, seq, MIX_W), BF16),
        grid=(MIX_W // LANES, bsz, n_q // BAND_TILES),
        in_specs=[q_spec,
                  pl.BlockSpec((None, seq, LANES), lambda p, b, j: (b, 0, p)),
                  pl.BlockSpec((None, n_q, LANES, tq), lambda p, b, j: (b, 0, p, 0)),
                  pl.BlockSpec((2, 1, BIAS_ROW), lambda p, b, j: (p, 0, 0))],
        out_specs=q_spec,
        scratch_shapes=[pltpu.VMEM((2, BAND_GROUPS, tq, tq), F32)],
        compiler_params=_params(3),
        name="band_attention",
    )(q3, k3, vt, _band_bias_rows(rel_table))
    return out.reshape(bsz * seq, MIX_W)


def _mem_kv_kernel(mem_ref, g_ref, w_ref, kg_ref, k_ref, v_ref):
    mem_n = (_rms(mem_ref[...]) * g_ref[...]).astype(BF16)
    kv = _dot(mem_n, w_ref[...])
    kg = kg_ref[...]
    for hd in range(X_HEADS):
        sl = slice(hd * X_HEAD_DIM, (hd + 1) * X_HEAD_DIM)
        k_ref[:, sl] = (_rms(kv[:, sl]) * kg).astype(BF16)
    v_ref[...] = kv[:, D_MODEL:].astype(BF16)


def _mem_kv(mem, gain, w_kv, k_gain):
    bsz, n_mem, d = mem.shape
    blk = pl.BlockSpec((None, n_mem, d), lambda b: (b, 0, 0))
    out = jax.ShapeDtypeStruct((bsz, n_mem, d), BF16)
    return pl.pallas_call(
        _mem_kv_kernel,
        out_shape=[out, out],
        grid=(bsz,),
        in_specs=[blk, _const_spec((1, d)), _const_spec((d, 2 * d)), _const_spec((1, X_HEAD_DIM))],
        out_specs=[blk, blk],
        compiler_params=_params(1),
        name="mem_kv",
    )(mem, gain.reshape(1, d), w_kv.astype(BF16), k_gain.reshape(1, X_HEAD_DIM))


def _cross_kernel(x_ref, left_a_ref, left_b_ref, right_a_ref, right_b_ref, wl_ref, wr_ref, g_ref,
                  wq_ref, qg_ref, k_ref, v_ref, wo_ref, o_ref):
    first_half = pl.program_id(1) < pl.num_programs(1) // 2
    left = jnp.where(first_half, left_a_ref[...], left_b_ref[...])
    right = jnp.where(first_half, right_a_ref[...], right_b_ref[...])
    x = x_ref[...] + _dot(left, wl_ref[...]) + _dot(right, wr_ref[...])
    h = (_rms(x) * g_ref[...]).astype(BF16)
    q = _dot(h, wq_ref[...])
    qg = qg_ref[...]
    head_cols = [slice(hd * X_HEAD_DIM, (hd + 1) * X_HEAD_DIM) for hd in range(X_HEADS)]

    def probabilities(sl):
        qh = (_rms(q[:, sl]) * qg * X_HEAD_DIM ** -0.5).astype(BF16)
        s = _dot_nt(qh, k_ref[:, sl])
        p = jnp.exp(s - jnp.max(s, axis=-1, keepdims=True))
        return p.astype(BF16), jnp.sum(p, axis=-1, keepdims=True)

    heads = []
    ahead = probabilities(head_cols[0])
    for hd, sl in enumerate(head_cols):
        p, l = ahead
        if hd + 1 < X_HEADS:
            ahead = probabilities(head_cols[hd + 1])
        heads.append((_dot(p, v_ref[:, sl]) / l).astype(BF16))
    o_ref[...] = x + _dot(jnp.concatenate(heads, axis=1), wo_ref[...])


def _halves(stream, bsz, seq):
    if isinstance(stream, tuple):
        return stream[0], stream[1], 0
    full = stream.reshape(bsz, seq, MIX_W)
    return full, full, seq // TOKEN_TILE // 2


def _mix_out_cross_attention(x, left, right, w_mix_out, gain, w_q, q_gain, k, v, w_o, bsz, seq):
    t, d = x.shape
    tq = TOKEN_TILE
    n_s = seq // tq
    half = n_s // 2
    n_mem = k.shape[1]
    tok = pl.BlockSpec((tq, d), lambda b, s: (b * n_s + s, 0))
    mem_spec = pl.BlockSpec((None, n_mem, d), lambda b, s: (b, 0, 0))
    w_mix = w_mix_out.astype(BF16)
    stream_args, stream_specs = [], []
    for first, second, offset in (_halves(left, bsz, seq), _halves(right, bsz, seq)):
        stream_args += [first, second]
        stream_specs += [
            pl.BlockSpec((None, tq, MIX_W), lambda b, s: (b, jnp.minimum(s, half - 1), 0)),
            pl.BlockSpec((None, tq, MIX_W),
                         lambda b, s, offset=offset: (b, offset + jnp.maximum(s - half, 0), 0))]
    return pl.pallas_call(
        _cross_kernel,
        out_shape=jax.ShapeDtypeStruct((t, d), F32),
        grid=(bsz, n_s),
        in_specs=[tok, *stream_specs, _const_spec((MIX_W, d)), _const_spec((MIX_W, d)),
                  _const_spec((1, d)), _const_spec((d, d)), _const_spec((1, X_HEAD_DIM)),
                  mem_spec, mem_spec, _const_spec((d, d))],
        out_specs=tok,
        compiler_params=_params(2),
        name="cross_attention",
    )(x, *stream_args, w_mix[:MIX_W], w_mix[MIX_W:], gain.reshape(1, d), w_q.astype(BF16),
      q_gain.reshape(1, X_HEAD_DIM), k, v, w_o.astype(BF16))


def kernel(x, mem, positions, ln_gains, ffn1_w_in, ffn1_w_out, ffn2_w_in, ffn2_w_out, even_w_in, even_f_bias, even_qk_gains, even_lambda, even_subln_gain, even_w_out, odd_w_in, odd_conv_w, odd_qk_gains, odd_rel_bias, odd_w_out, x_w_q, x_w_kv, x_qk_gains, x_w_o):
    bsz, seq, d = x.shape
    depth = ln_gains.shape[0]
    rope = _rope_tables(positions)
    x = x.reshape(bsz * seq, d)
    (ffn1_w_in, ffn1_w_out, ffn2_w_in, ffn2_w_out, even_w_in, even_w_out, odd_w_in, odd_w_out,
     x_w_q, x_w_kv, x_w_o) = (
        w.astype(BF16) for w in (ffn1_w_in, ffn1_w_out, ffn2_w_in, ffn2_w_out, even_w_in,
                                 even_w_out, odd_w_in, odd_w_out, x_w_q, x_w_kv, x_w_o))
    for layer in range(depth):
        g = ln_gains[layer]
        x = _ffn(x, g[0], ffn1_w_in[layer], ffn1_w_out[layer])
        if layer % 2 == 0:
            e = layer // 2
            lambda_init = 0.8 - 0.6 * math.exp(-0.3 * layer)
            aq, ak, avt, bq, bk, bvt = _even_prep(
                x, g[1], even_w_in[e], even_f_bias[e], even_qk_gains[e], rope, bsz, seq)
            left = _fox_attention(aq, ak, avt, bsz, seq)
            right = _diff_attention(bq, bk, bvt, even_lambda[e], even_subln_gain[e], lambda_init,
                                    bsz, seq)
            w_mix_out = even_w_out[e]
        else:
            o = layer // 2
            left, dq, dk, dvt = _odd_prep(x, g[1], odd_w_in[o], odd_conv_w[o], odd_qk_gains[o],
                                          bsz, seq)
            right = _band_attention(dq, dk, dvt, odd_rel_bias[o], bsz, seq)
            w_mix_out = odd_w_out[o]
        mk, mv = _mem_kv(mem, g[3], x_w_kv[layer], x_qk_gains[layer, 1])
        x = _mix_out_cross_attention(x, left, right, w_mix_out, g[2], x_w_q[layer],
                                     x_qk_gains[layer, 0], mk, mv, x_w_o[layer], bsz, seq)
        x = _ffn(x, g[4], ffn2_w_in[layer], ffn2_w_out[layer])
    return x.reshape(bsz, seq, d)
```

```python
import functools
import math

import jax
import jax.numpy as jnp
from jax import lax
from jax.experimental import pallas as pl
from jax.experimental.pallas import tpu as pltpu

D_MODEL = 1024
CHUNK = 64
HEAD_DIM = 64
ROPE_DIM = HEAD_DIM // 4
ROPE_THETA = 500000.0
RMS_EPS = 1e-6
D_FF = 2816
A_HEADS = 8
MIX_W = 512
LANES = 128
D_LEFT = 8 * CHUNK
REL_CLIP = 128
X_HEADS = 4
X_HEAD_DIM = D_MODEL // X_HEADS
NEG = -1e30
LOG2E = math.log2(math.e)
Q_SCALE = HEAD_DIM ** -0.5 * LOG2E

VMEM_LIMIT = 56 * 1024 * 1024
FFN_CHUNKS = (768, 768, 768, 512)
TOKEN_TILE = 512
PREP_TILE = 1024
ATT_TILE = 256

BF16 = jnp.bfloat16
F32 = jnp.float32


def _params(n_axes):
    return pltpu.CompilerParams(dimension_semantics=("arbitrary",) * n_axes,
                                vmem_limit_bytes=VMEM_LIMIT)


def _const_spec(shape):
    nd = len(shape)
    return pl.BlockSpec(shape, lambda *_: (0,) * nd, pipeline_mode=pl.Buffered(1))


def _rms(x):
    return x * lax.rsqrt(jnp.mean(x * x, axis=-1, keepdims=True) + RMS_EPS)


def _dot(a, b):
    return jnp.dot(a, b, preferred_element_type=F32)


def _dot_nt(a, b):
    return lax.dot_general(a, b, (((1,), (1,)), ((), ())), preferred_element_type=F32)


def _ffn_kernel(x_ref, g_ref, wi_ref, wo_ref, o_ref):
    n_tiles = x_ref.shape[0] // TOKEN_TILE
    bounds = [sum(FFN_CHUNKS[:c]) for c in range(len(FFN_CHUNKS) + 1)]
    chunks = list(zip(bounds[:-1], bounds[1:]))

    def rows(i):
        return slice(i * TOKEN_TILE, (i + 1) * TOKEN_TILE)

    def normed(i):
        x = x_ref[rows(i), :]
        return x, (_rms(x) * g_ref[...]).astype(BF16)

    def chunk_out(xn, cols):
        lo, hi = cols
        gate = _dot(xn, wi_ref[:, lo:hi])
        up = _dot(xn, wi_ref[:, D_FF + lo:D_FF + hi])
        act = (gate * (1.0 / (1.0 + jnp.exp(-gate))) * up).astype(BF16)
        return _dot(act, wo_ref[lo:hi, :])

    x, xn = normed(0)
    previous = None
    for i in range(n_tiles):
        y = chunk_out(xn, chunks[0])
        if previous is not None:
            o_ref[rows(i - 1), :] = previous[0] + 0.5 * previous[1]
        for c, cols in enumerate(chunks[1:], start=1):
            y = y + chunk_out(xn, cols)
            if c == 1 and i + 1 < n_tiles:
                following = normed(i + 1)
        previous = (x, y)
        if i + 1 < n_tiles:
            x, xn = following
    o_ref[rows(n_tiles - 1), :] = previous[0] + 0.5 * previous[1]


FFN_TILES_PER_STEP = 2


def _ffn(x, gain, w_in, w_out):
    t, d = x.shape
    tm = TOKEN_TILE * FFN_TILES_PER_STEP
    return pl.pallas_call(
        _ffn_kernel,
        out_shape=jax.ShapeDtypeStruct((t, d), F32),
        grid=(t // tm,),
        in_specs=[pl.BlockSpec((tm, d), lambda i: (i, 0)),
                  _const_spec((1, d)), _const_spec((d, 2 * D_FF)), _const_spec((D_FF, d))],
        out_specs=pl.BlockSpec((tm, d), lambda i: (i, 0)),
        compiler_params=_params(1),
        name="ffn",
    )(x, gain.reshape(1, d), w_in.astype(BF16), w_out.astype(BF16))


BD_WIDTH = 256


def _head_meansq(y, mean_bd):
    sq = (y * y).astype(BF16)
    return jnp.concatenate([_dot(sq[:, c:c + BD_WIDTH], mean_bd) for c in range(0, MIX_W, BD_WIDTH)],
                           axis=1)


def _head_norm(y, meansq, gain):
    return y * lax.rsqrt(meansq + RMS_EPS) * gain


def _block_diag_mean():
    r = jnp.arange(BD_WIDTH) // HEAD_DIM
    return jnp.where(r[:, None] == r[None, :], 1.0 / HEAD_DIM, 0.0).astype(BF16)


def _tile_heads(v):
    return jnp.tile(v, (1, MIX_W // HEAD_DIM))


def _qk_gain_rows(qk_gains):
    scale = jnp.where(jnp.arange(qk_gains.shape[0]) % 2 == 0, Q_SCALE, 1.0).astype(F32)
    return _tile_heads(qk_gains * scale[:, None])


def _store_transposed(vt_ref, v):
    for i in range(v.shape[0] // ATT_TILE):
        vt_ref[i] = v[i * ATT_TILE:(i + 1) * ATT_TILE, :].T.astype(BF16)


def _split3(f):
    hi = f.astype(BF16).astype(F32)
    rest = f - hi
    mid = rest.astype(BF16).astype(F32)
    return hi, mid, rest - mid


def _even_prep_kernel(x_ref, g_ref, wa_ref, wf_ref, fb_ref, wb_ref, gains_ref, bd_ref,
                      ra_ref, rm_ref, rp_ref,
                      aq_ref, ak_ref, avt_ref, bq_ref, bk_ref, bvt_ref, carry_ref):
    tm = x_ref.shape[0]

    @pl.when(pl.program_id(1) == 0)
    def _():
        carry_ref[...] = jnp.zeros_like(carry_ref)

    h = (_rms(x_ref[...]) * g_ref[...]).astype(BF16)
    bd = bd_ref[...]
    gains = gains_ref[...]

    def project(w_ref, part):
        return _dot(h, w_ref[:, part * MIX_W:(part + 1) * MIX_W])

    z = _dot(h, wf_ref[...]) + fb_ref[...]
    a_q = project(wa_ref, 0)

    logf = jnp.minimum(z, 0.0) - jnp.log1p(jnp.exp(-jnp.abs(z)))
    row = lax.broadcasted_iota(jnp.int32, logf.shape, 0)
    step = 1
    while step < tm:
        logf = logf + jnp.where(row >= step, pltpu.roll(logf, step, axis=0), 0.0)
        step *= 2
    cum = logf + carry_ref[...]
    carry_ref[...] = cum[tm - 1:tm, :]
    hi, mid, lo = _split3(cum * LOG2E)

    a_k = project(wa_ref, 1)
    a_q_ss = _head_meansq(a_q, bd)
    b_q = project(wb_ref, 0)
    qn = _head_norm(a_q, a_q_ss, gains[0:1])
    a_k_ss = _head_meansq(a_k, bd)
    b_k = project(wb_ref, 1)
    kn = _head_norm(a_k, a_k_ss, gains[1:2])

    lane = lax.broadcasted_iota(jnp.int32, (tm, LANES), 1)
    is_hi = (lane == HEAD_DIM) | (lane == HEAD_DIM + 3)
    is_mid = (lane == HEAD_DIM + 1) | (lane == HEAD_DIM + 4)
    ones_hi = jnp.where(lane < HEAD_DIM + 6, 1.0, 0.0)
    for hd in range(A_HEADS):
        if hd == A_HEADS // 2:
            b_q_ss = _head_meansq(b_q, bd)
            a_v = project(wa_ref, 2)
        pieces = jnp.where(is_hi, hi[:, hd:hd + 1],
                           jnp.where(is_mid, mid[:, hd:hd + 1], lo[:, hd:hd + 1]))
        blk = slice((hd // 2) * LANES, (hd // 2 + 1) * LANES)
        sq, sk = qn[:, blk], kn[:, blk]
        if hd % 2:
            sq, sk = pltpu.roll(sq, HEAD_DIM, axis=1), pltpu.roll(sk, HEAD_DIM, axis=1)
        qa = jnp.where(lane < HEAD_DIM, sq, jnp.where(lane < HEAD_DIM + 3, pieces, ones_hi))
        ka = jnp.where(lane < HEAD_DIM, sk, jnp.where(lane < HEAD_DIM + 3, 1.0,
                                                      jnp.where(lane < HEAD_DIM + 6, -pieces, 0.0)))
        aq_ref[:, hd * LANES:(hd + 1) * LANES] = qa.astype(BF16)
        ak_ref[:, hd * LANES:(hd + 1) * LANES] = ka.astype(BF16)

    rep = MIX_W // LANES
    ra = jnp.tile(ra_ref[...], (1, rep))
    rm = jnp.tile(rm_ref[...], (1, rep))
    rp = jnp.tile(rp_ref[...], (1, rep))

    def rope(v):
        half = ROPE_DIM // 2
        return v * ra + pltpu.roll(v, MIX_W - half, axis=1) * rm + pltpu.roll(v, half, axis=1) * rp

    bq_ref[...] = rope(_head_norm(b_q, b_q_ss, gains[2:3])).astype(BF16)
    b_k_ss = _head_meansq(b_k, bd)
    _store_transposed(avt_ref, a_v)
    b_v = project(wb_ref, 2)
    bk_ref[...] = rope(_head_norm(b_k, b_k_ss, gains[3:4])).astype(BF16)
    _store_transposed(bvt_ref, b_v)


def _rope_tables(positions):
    half = ROPE_DIM // 2
    inv = ROPE_THETA ** (-jnp.arange(0, ROPE_DIM, 2, dtype=F32) / ROPE_DIM)
    ang = positions.astype(F32).reshape(-1, 1) * inv
    cos = jnp.tile(jnp.cos(ang), (1, LANES // half))
    sin = jnp.tile(jnp.sin(ang), (1, LANES // half))
    in_head = jnp.arange(LANES) % HEAD_DIM
    ra = jnp.where(in_head < ROPE_DIM, cos, 1.0)
    rm = jnp.where(in_head < half, -sin, 0.0)
    rp = jnp.where((in_head >= half) & (in_head < ROPE_DIM), sin, 0.0)
    return ra, rm, rp


def _even_prep(x, gain, w_in, f_bias, qk_gains, rope, bsz, seq):
    t, d = x.shape
    tm = PREP_TILE
    a_w = 3 * MIX_W
    wa = w_in[:, :a_w].astype(BF16)
    wf = jnp.pad(w_in[:, a_w:a_w + A_HEADS], ((0, 0), (0, LANES - A_HEADS))).astype(BF16)
    wb = w_in[:, a_w + A_HEADS:].astype(BF16)
    fb = jnp.pad(f_bias, (0, LANES - A_HEADS)).reshape(1, LANES)
    n_s = seq // tm
    tok = lambda w: pl.BlockSpec((tm, w), lambda b, s: (b * n_s + s, 0))
    vt_spec = pl.BlockSpec((None, tm // ATT_TILE, MIX_W, ATT_TILE), lambda b, s: (b, s, 0, 0))
    stream = lambda w: jax.ShapeDtypeStruct((t, w), BF16)
    vt_shape = jax.ShapeDtypeStruct((bsz, seq // ATT_TILE, MIX_W, ATT_TILE), BF16)
    return pl.pallas_call(
        _even_prep_kernel,
        out_shape=[stream(A_HEADS * LANES), stream(A_HEADS * LANES), vt_shape,
                   stream(MIX_W), stream(MIX_W), vt_shape],
        grid=(bsz, n_s),
        in_specs=[tok(d), _const_spec((1, d)), _const_spec((d, a_w)), _const_spec((d, LANES)),
                  _const_spec((1, LANES)), _const_spec((d, a_w)), _const_spec((4, MIX_W)),
                  _const_spec((BD_WIDTH, BD_WIDTH)), tok(LANES), tok(LANES), tok(LANES)],
        out_specs=[tok(A_HEADS * LANES), tok(A_HEADS * LANES), vt_spec,
                   tok(MIX_W), tok(MIX_W), vt_spec],
        scratch_shapes=[pltpu.VMEM((1, LANES), F32)],
        compiler_params=_params(2),
        name="even_prep",
    )(x, gain.reshape(1, d), wa, wf, fb, wb, _qk_gain_rows(qk_gains), _block_diag_mean(), *rope)


def _lane_lo(shape):
    return lax.broadcasted_iota(jnp.int32, shape, len(shape) - 1) < HEAD_DIM


def _key_rows(g):
    if isinstance(g, int):
        return slice(g * ATT_TILE, (g + 1) * ATT_TILE)
    return pl.ds(pl.multiple_of(g * ATT_TILE, ATT_TILE), ATT_TILE)


SUM_ROWS = 16
MXU_LOOKAHEAD = 5


def _with_sum_rows(value_t):
    return jnp.concatenate([value_t, jnp.ones((SUM_ROWS, value_t.shape[1]), BF16)], axis=0)


def _normalise(acc, v_rows):
    return acc[:v_rows] / acc[v_rows:v_rows + 1]


def _causal_attention(tiles, n_sub, queries, key_block, value_t, diag_visible, v_rows, finish):
    tq = ATT_TILE
    items = [(i, g, sub) for i, tile in enumerate(tiles) for g in range(tile + 1)
             for sub in range(n_sub)]
    maxes = [[jnp.full((1, tq), NEG, F32)] * n_sub for _ in tiles]
    accs = [[jnp.zeros((v_rows + SUM_ROWS, tq), F32)] * n_sub for _ in tiles]

    def scores(i, g, sub):
        s = _dot_nt(key_block(g, sub), queries[i][sub])
        return jnp.where(diag_visible, s, NEG) if g == tiles[i] else s

    def exponentials(i, g, sub, s):
        m_old = maxes[i][sub]
        m_new = jnp.maximum(m_old, jnp.max(s, axis=0, keepdims=True))
        maxes[i][sub] = m_new
        return jnp.exp2(s - m_new).astype(BF16), jnp.exp2(m_old - m_new)

    def accumulate(i, g, sub, p, alpha):
        accs[i][sub] = alpha * accs[i][sub] + _dot(_with_sum_rows(value_t(g, sub)), p)
        if g == tiles[i] and sub == n_sub - 1:
            finish(i, accs[i])

    in_flight = []
    for item in items:
        in_flight.append((item, exponentials(*item, scores(*item))))
        if len(in_flight) > MXU_LOOKAHEAD:
            done, (p, alpha) = in_flight.pop(0)
            accumulate(*done, p, alpha)
    for done, (p, alpha) in in_flight:
        accumulate(*done, p, alpha)


HEADS_PER_STEP = 4
TILES_PER_HALF = 2


def _step_tiles(n_tiles, j):
    first = [TILES_PER_HALF * j + i for i in range(TILES_PER_HALF)]
    return first + [n_tiles - 1 - t for t in reversed(first)]


def _for_each_step(n_tiles, body):
    for j in range(n_tiles // (2 * TILES_PER_HALF)):
        pl.when(pl.program_id(2) == j)(functools.partial(body, _step_tiles(n_tiles, j)))


def _tile_views(front_ref, back_ref):
    rows = [slice(i * ATT_TILE, (i + 1) * ATT_TILE) for i in range(TILES_PER_HALF)]
    return [(front_ref, r) for r in rows] + [(back_ref, r) for r in rows]


def _mirrored_specs(n_tiles, q_width, out_width):
    rows = TILES_PER_HALF * ATT_TILE
    n_steps = n_tiles // (2 * TILES_PER_HALF)
    q_front = pl.BlockSpec((None, rows, q_width), lambda b, p, j: (b, j, p))
    q_back = pl.BlockSpec((None, rows, q_width), lambda b, p, j: (b, 2 * n_steps - 1 - j, p))
    o_front = pl.BlockSpec((None, rows, out_width), lambda b, p, j: (b, j, p))
    o_back = pl.BlockSpec((None, rows, out_width), lambda b, p, j: (b, n_steps - 1 - j, p))
    return q_front, q_back, o_front, o_back


def _fox_kernel(qa_ref, qb_ref, k_ref, vt_ref, oa_ref, ob_ref, *, n_tiles):
    tq = ATT_TILE

    def key_block(g, sub):
        return k_ref[_key_rows(g), sub * LANES:(sub + 1) * LANES]

    def value_t(g, sub):
        return vt_ref[g, sub * HEAD_DIM:(sub + 1) * HEAD_DIM, :]

    def body(tiles):
        queries = [[q_ref[rows, sub * LANES:(sub + 1) * LANES] for sub in range(HEADS_PER_STEP)]
                   for q_ref, rows in _tile_views(qa_ref, qb_ref)]
        outputs = _tile_views(oa_ref, ob_ref)
        causal = (lax.broadcasted_iota(jnp.int32, (tq, tq), 0)
                  <= lax.broadcasted_iota(jnp.int32, (tq, tq), 1))

        def finish(i, accs):
            o_ref, rows = outputs[i]
            o_t = jnp.concatenate([_normalise(acc, HEAD_DIM) for acc in accs], axis=0)
            o_ref[rows, :] = o_t.T.astype(o_ref.dtype)

        _causal_attention(tiles, HEADS_PER_STEP, queries, key_block, value_t, causal, HEAD_DIM,
                          finish)

    _for_each_step(n_tiles, body)


def _fox_attention(q, k, vt, bsz, seq):
    tq = ATT_TILE
    n_tiles = seq // tq
    q3 = q.reshape(bsz, seq, A_HEADS * LANES)
    k3 = k.reshape(bsz, seq, A_HEADS * LANES)
    out_width = HEADS_PER_STEP * HEAD_DIM
    q_a, q_b, o_a, o_b = _mirrored_specs(n_tiles, HEADS_PER_STEP * LANES, out_width)
    half_out = jax.ShapeDtypeStruct((bsz, seq // 2, MIX_W), BF16)
    out_a, out_b = pl.pallas_call(
        functools.partial(_fox_kernel, n_tiles=n_tiles),
        out_shape=[half_out, half_out],
        grid=(bsz, A_HEADS // HEADS_PER_STEP, n_tiles // (2 * TILES_PER_HALF)),
        in_specs=[q_a, q_b,
                  pl.BlockSpec((None, seq, HEADS_PER_STEP * LANES), lambda b, p, a: (b, 0, p)),
                  pl.BlockSpec((None, n_tiles, out_width, tq), lambda b, p, a: (b, 0, p, 0))],
        out_specs=[o_a, o_b],
        compiler_params=_params(3),
        name="fox_attention",
    )(q3, q3, k3, vt)
    return out_a, out_b


def _diff_kernel(qa_ref, qb_ref, k_ref, vt_ref, lam_ref, sg_ref, oa_ref, ob_ref, *,
                 lambda_init, n_tiles):
    tq = ATT_TILE
    lo = _lane_lo((tq, LANES))
    lp = lam_ref[...]
    lam = (jnp.exp(jnp.sum(lp[0:1] * lp[1:2], axis=1, keepdims=True))
           - jnp.exp(jnp.sum(lp[2:3] * lp[3:4], axis=1, keepdims=True)) + lambda_init)

    def half_of(q_ref, rows, sub):
        q = q_ref[rows, sub // 2 * LANES:(sub // 2 + 1) * LANES]
        zero = jnp.zeros_like(q)
        return jnp.where(lo, q, zero) if sub % 2 == 0 else jnp.where(lo, zero, q)

    def key_block(g, sub):
        return k_ref[_key_rows(g), sub // 2 * LANES:(sub // 2 + 1) * LANES]

    def value_t(g, sub):
        return vt_ref[g, sub // 2 * LANES:(sub // 2 + 1) * LANES, :]

    def body(tiles):
        queries = [[half_of(q_ref, rows, sub) for sub in range(HEADS_PER_STEP)]
                   for q_ref, rows in _tile_views(qa_ref, qb_ref)]
        outputs = _tile_views(oa_ref, ob_ref)
        chunk_causal = (lax.broadcasted_iota(jnp.int32, (tq, tq), 0) // CHUNK
                        <= lax.broadcasted_iota(jnp.int32, (tq, tq), 1) // CHUNK)

        def finish(i, accs):
            o_ref, rows = outputs[i]
            for hd in range(HEADS_PER_STEP // 2):
                o1, o2 = (_normalise(acc, LANES) for acc in accs[2 * hd:2 * hd + 2])
                o = (o1 - lam * o2).T
                o_ref[rows, hd * LANES:(hd + 1) * LANES] = (
                    _rms(o) * sg_ref[...] * (1.0 - lambda_init)).astype(o_ref.dtype)

        _causal_attention(tiles, HEADS_PER_STEP, queries, key_block, value_t, chunk_causal, LANES,
                          finish)

    _for_each_step(n_tiles, body)


def _diff_attention(q, k, vt, lam_params, subln_gain, lambda_init, bsz, seq):
    tq = ATT_TILE
    n_tiles = seq // tq
    q3, k3 = (a.reshape(bsz, seq, MIX_W) for a in (q, k))
    width = HEADS_PER_STEP // 2 * LANES
    q_a, q_b, o_a, o_b = _mirrored_specs(n_tiles, width, width)
    half_out = jax.ShapeDtypeStruct((bsz, seq // 2, MIX_W), BF16)
    out_a, out_b = pl.pallas_call(
        functools.partial(_diff_kernel, lambda_init=lambda_init, n_tiles=n_tiles),
        out_shape=[half_out, half_out],
        grid=(bsz, MIX_W // width, n_tiles // (2 * TILES_PER_HALF)),
        in_specs=[q_a, q_b,
                  pl.BlockSpec((None, seq, width), lambda b, h, a: (b, 0, h)),
                  pl.BlockSpec((None, n_tiles, width, tq), lambda b, h, a: (b, 0, h, 0)),
                  _const_spec((4, HEAD_DIM)), _const_spec((1, LANES))],
        out_specs=[o_a, o_b],
        compiler_params=_params(3),
        name="diff_attention",
    )(q3, q3, k3, vt, lam_params, subln_gain.reshape(1, LANES))
    return out_a, out_b


def _odd_prep_kernel(x_ref, g_ref, wc_ref, wd_ref, cw_ref, gains_ref, bd_ref,
                     c_ref, dq_ref, dk_ref, dvt_ref, carry_ref):
    tm = x_ref.shape[0]

    @pl.when(pl.program_id(1) == 0)
    def _():
        carry_ref[...] = jnp.zeros_like(carry_ref)

    h = (_rms(x_ref[...]) * g_ref[...]).astype(BF16)
    bd = bd_ref[...]
    gains = gains_ref[...]

    def project(w_ref, part):
        return _dot(h, w_ref[:, part * MIX_W:(part + 1) * MIX_W])

    d_q = project(wd_ref, 0)
    d_k = project(wd_ref, 1)
    d_q_ss = _head_meansq(d_q, bd)
    c_c = project(wc_ref, 1)
    d_k_ss = _head_meansq(d_k, bd)
    c_h = project(wc_ref, 2)
    c_b = project(wc_ref, 0)
    d_v = project(wd_ref, 2)

    u = c_c * c_h
    prev = carry_ref[...]
    carry_ref[...] = u[tm - 8:, :]
    row = lax.broadcasted_iota(jnp.int32, u.shape, 0)
    u1 = jnp.where(row == 0, prev[7:8], pltpu.roll(u, 1, axis=0))
    u2 = jnp.where(row == 0, prev[6:7], jnp.where(row == 1, prev[7:8], pltpu.roll(u, 2, axis=0)))
    cw = cw_ref[...]
    conv = cw[0:1] * u2 + cw[1:2] * u1 + cw[2:3] * u
    c_ref[...] = (c_b * conv).astype(BF16)

    dq_ref[...] = _head_norm(d_q, d_q_ss, gains[0:1]).astype(BF16)
    dk_ref[...] = _head_norm(d_k, d_k_ss, gains[1:2]).astype(BF16)
    _store_transposed(dvt_ref, d_v)


def _odd_prep(x, gain, w_in, conv_w, qk_gains, bsz, seq):
    t, d = x.shape
    tm = PREP_TILE
    w3 = 3 * MIX_W
    n_s = seq // tm
    tok = lambda w: pl.BlockSpec((tm, w), lambda b, s: (b * n_s + s, 0))
    stream = jax.ShapeDtypeStruct((t, MIX_W), BF16)
    return pl.pallas_call(
        _odd_prep_kernel,
        out_shape=[stream] * 3 + [jax.ShapeDtypeStruct((bsz, seq // ATT_TILE, MIX_W, ATT_TILE), BF16)],
        grid=(bsz, n_s),
        in_specs=[tok(d), _const_spec((1, d)), _const_spec((d, w3)), _const_spec((d, w3)),
                  _const_spec((3, MIX_W)), _const_spec((2, MIX_W)), _const_spec((BD_WIDTH, BD_WIDTH))],
        out_specs=[tok(MIX_W)] * 3 + [pl.BlockSpec((None, tm // ATT_TILE, MIX_W, ATT_TILE),
                                                   lambda b, s: (b, s, 0, 0))],
        scratch_shapes=[pltpu.VMEM((8, MIX_W), F32)],
        compiler_params=_params(2),
        name="odd_prep",
    )(x, gain.reshape(1, d), w_in[:, :w3].astype(BF16), w_in[:, w3:].astype(BF16), conv_w,
      _qk_gain_rows(qk_gains), _block_diag_mean())


BAND_GROUPS = 1 + D_LEFT // ATT_TILE


MAX_BAND_TILES = 8
BAND_WIDTH = BAND_GROUPS * ATT_TILE
BIAS_ROW = BAND_WIDTH + ATT_TILE


def _band_kernel(q_ref, k_ref, vt_ref, w_ref, o_ref, bias_ref):
    tq = ATT_TILE
    band_tiles = q_ref.shape[0] // tq
    j = pl.program_id(2)

    @pl.when((pl.program_id(1) == 0) & (j == 0))
    def _():
        key = lax.broadcasted_iota(jnp.int32, (BAND_WIDTH, tq), 0)
        chunk_start = lax.broadcasted_iota(jnp.int32, (BAND_WIDTH, tq), 1) // CHUNK * CHUNK
        in_band = (key >= chunk_start) & (key < chunk_start + D_LEFT + CHUNK)
        for sub in range(2):
            rows = jnp.broadcast_to(w_ref[sub], (BAND_WIDTH, BIAS_ROW))
            skew = pltpu.roll(rows, tq + 1, axis=1, stride=1, stride_axis=0)
            table = jnp.where(in_band, skew[:, :tq] * LOG2E, NEG)
            for grp in range(BAND_GROUPS):
                bias_ref[sub, grp] = table[grp * tq:(grp + 1) * tq, :]

    lo = _lane_lo((tq, LANES))
    before_start = jnp.where(j == 0, NEG, 0.0)
    q_subs, items = {}, []
    for tile in range(band_tiles):
        q = q_ref[tile * tq:(tile + 1) * tq, :]
        zero = jnp.zeros_like(q)
        q_subs[tile] = (jnp.where(lo, q, zero), jnp.where(lo, zero, q))
        items += [(tile, grp, sub) for grp in range(BAND_GROUPS) for sub in range(2)]

    def may_precede(tile, grp):
        return tile + grp < BAND_GROUPS - 1

    def group_index(tile, grp):
        g = band_tiles * j + tile + grp - (BAND_GROUPS - 1)
        return jnp.maximum(g, 0) if may_precede(tile, grp) else g

    def scores(tile, grp, sub):
        s = _dot_nt(k_ref[_key_rows(group_index(tile, grp)), :], q_subs[tile][sub]) + bias_ref[sub, grp]
        return s + before_start if may_precede(tile, grp) else s

    maxes, accs = {}, {}

    def exponentials(tile, grp, sub, s):
        mx = jnp.max(s, axis=0, keepdims=True)
        m_old = maxes.get((tile, sub))
        m_new = mx if m_old is None else jnp.maximum(m_old, mx)
        maxes[tile, sub] = m_new
        return jnp.exp2(s - m_new).astype(BF16), None if m_old is None else jnp.exp2(m_old - m_new)

    def accumulate(tile, grp, sub, p, alpha):
        value_t = _with_sum_rows(vt_ref[group_index(tile, grp), sub * HEAD_DIM:(sub + 1) * HEAD_DIM, :])
        acc_n = _dot(value_t, p)
        accs[tile, sub] = acc_n if alpha is None else alpha * accs[tile, sub] + acc_n
        if grp == BAND_GROUPS - 1 and sub == 1:
            o_t = jnp.concatenate([_normalise(accs[tile, s], HEAD_DIM) for s in range(2)], axis=0)
            o_ref[tile * tq:(tile + 1) * tq, :] = o_t.T.astype(o_ref.dtype)

    in_flight = []
    for item in items:
        in_flight.append((item, exponentials(*item, scores(*item))))
        if len(in_flight) > MXU_LOOKAHEAD:
            done, (p, alpha) = in_flight.pop(0)
            accumulate(*done, p, alpha)
    for done, (p, alpha) in in_flight:
        accumulate(*done, p, alpha)


def _band_bias_rows(rel_table):
    n_lo = ATT_TILE - 1 - REL_CLIP
    n_hi = BIAS_ROW - n_lo - (2 * REL_CLIP + 1)
    w = jnp.concatenate([jnp.repeat(rel_table[:, :1], n_lo, axis=1), rel_table,
                         jnp.repeat(rel_table[:, -1:], n_hi, axis=1)], axis=1)
    return w.astype(F32).reshape(rel_table.shape[0], 1, BIAS_ROW)


def _band_attention(q, k, vt, rel_table, bsz, seq):
    tq = ATT_TILE
    n_q = seq // tq
    band_tiles = min(MAX_BAND_TILES, n_q)
    q3, k3 = (a.reshape(bsz, seq, MIX_W) for a in (q, k))
    q_spec = pl.BlockSpec((None, band_tiles * tq, LANES), lambda p, b, j: (b, j, p))
    out = pl.pallas_call(
        _band_kernel,
        out_shape=jax.ShapeDtypeStruct((bsz, seq, MIX_W), BF16),
        grid=(MIX_W // LANES, bsz, n_q // band_tiles),
        in_specs=[q_spec,
                  pl.BlockSpec((None, seq, LANES), lambda p, b, j: (b, 0, p)),
                  pl.BlockSpec((None, n_q, LANES, tq), lambda p, b, j: (b, 0, p, 0)),
                  pl.BlockSpec((2, 1, BIAS_ROW), lambda p, b, j: (p, 0, 0))],
        out_specs=q_spec,
        scratch_shapes=[pltpu.VMEM((2, BAND_GROUPS, tq, tq), F32)],
        compiler_params=_params(3),
        name="band_attention",
    )(q3, k3, vt, _band_bias_rows(rel_table))
    return out.reshape(bsz * seq, MIX_W)


def _mem_kv_kernel(mem_ref, g_ref, w_ref, kg_ref, k_ref, v_ref):
    mem_n = (_rms(mem_ref[...]) * g_ref[...]).astype(BF16)
    kv = _dot(mem_n, w_ref[...])
    kg = kg_ref[...]
    for hd in range(X_HEADS):
        sl = slice(hd * X_HEAD_DIM, (hd + 1) * X_HEAD_DIM)
        k_ref[:, sl] = (_rms(kv[:, sl]) * kg).astype(BF16)
    v_ref[...] = kv[:, D_MODEL:].astype(BF16)


def _mem_kv(mem, gain, w_kv, k_gain):
    bsz, n_mem, d = mem.shape
    blk = pl.BlockSpec((None, n_mem, d), lambda b: (b, 0, 0))
    out = jax.ShapeDtypeStruct((bsz, n_mem, d), BF16)
    return pl.pallas_call(
        _mem_kv_kernel,
        out_shape=[out, out],
        grid=(bsz,),
        in_specs=[blk, _const_spec((1, d)), _const_spec((d, 2 * d)), _const_spec((1, X_HEAD_DIM))],
        out_specs=[blk, blk],
        compiler_params=_params(1),
        name="mem_kv",
    )(mem, gain.reshape(1, d), w_kv.astype(BF16), k_gain.reshape(1, X_HEAD_DIM))


def _cross_kernel(x_ref, left_a_ref, left_b_ref, right_a_ref, right_b_ref, wl_ref, wr_ref, g_ref,
                  wq_ref, qg_ref, k_ref, v_ref, wo_ref, o_ref):
    first_half = pl.program_id(1) < pl.num_programs(1) // 2
    left = jnp.where(first_half, left_a_ref[...], left_b_ref[...])
    right = jnp.where(first_half, right_a_ref[...], right_b_ref[...])
    x = x_ref[...] + _dot(left, wl_ref[...]) + _dot(right, wr_ref[...])
    h = (_rms(x) * g_ref[...]).astype(BF16)
    q = _dot(h, wq_ref[...])
    qg = qg_ref[...]
    head_cols = [slice(hd * X_HEAD_DIM, (hd + 1) * X_HEAD_DIM) for hd in range(X_HEADS)]

    def probabilities(sl):
        qh = (_rms(q[:, sl]) * qg * X_HEAD_DIM ** -0.5).astype(BF16)
        s = _dot_nt(qh, k_ref[:, sl])
        p = jnp.exp(s - jnp.max(s, axis=-1, keepdims=True))
        return p.astype(BF16), jnp.sum(p, axis=-1, keepdims=True)

    heads = []
    ahead = probabilities(head_cols[0])
    for hd, sl in enumerate(head_cols):
        p, l = ahead
        if hd + 1 < X_HEADS:
            ahead = probabilities(head_cols[hd + 1])
        heads.append((_dot(p, v_ref[:, sl]) / l).astype(BF16))
    o_ref[...] = x + _dot(jnp.concatenate(heads, axis=1), wo_ref[...])


def _halves(stream, bsz, seq):
    if isinstance(stream, tuple):
        return stream[0], stream[1], 0
    full = stream.reshape(bsz, seq, MIX_W)
    return full, full, seq // CROSS_TILE // 2


CROSS_TILE = 1024


def _mix_out_cross_attention(x, left, right, w_mix_out, gain, w_q, q_gain, k, v, w_o, bsz, seq):
    t, d = x.shape
    tq = CROSS_TILE
    n_s = seq // tq
    half = n_s // 2
    n_mem = k.shape[1]
    tok = pl.BlockSpec((tq, d), lambda b, s: (b * n_s + s, 0))
    mem_spec = pl.BlockSpec((None, n_mem, d), lambda b, s: (b, 0, 0))
    w_mix = w_mix_out.astype(BF16)
    stream_args, stream_specs = [], []
    for first, second, offset in (_halves(left, bsz, seq), _halves(right, bsz, seq)):
        stream_args += [first, second]
        stream_specs += [
            pl.BlockSpec((None, tq, MIX_W), lambda b, s: (b, jnp.minimum(s, half - 1), 0)),
            pl.BlockSpec((None, tq, MIX_W),
                         lambda b, s, offset=offset: (b, offset + jnp.maximum(s - half, 0), 0))]
    return pl.pallas_call(
        _cross_kernel,
        out_shape=jax.ShapeDtypeStruct((t, d), F32),
        grid=(bsz, n_s),
        in_specs=[tok, *stream_specs, _const_spec((MIX_W, d)), _const_spec((MIX_W, d)),
                  _const_spec((1, d)), _const_spec((d, d)), _const_spec((1, X_HEAD_DIM)),
                  mem_spec, mem_spec, _const_spec((d, d))],
        out_specs=tok,
        compiler_params=_params(2),
        name="cross_attention",
    )(x, *stream_args, w_mix[:MIX_W], w_mix[MIX_W:], gain.reshape(1, d), w_q.astype(BF16),
      q_gain.reshape(1, X_HEAD_DIM), k, v, w_o.astype(BF16))


def kernel(x, mem, positions, ln_gains, ffn1_w_in, ffn1_w_out, ffn2_w_in, ffn2_w_out, even_w_in, even_f_bias, even_qk_gains, even_lambda, even_subln_gain, even_w_out, odd_w_in, odd_conv_w, odd_qk_gains, odd_rel_bias, odd_w_out, x_w_q, x_w_kv, x_qk_gains, x_w_o):
    bsz, seq, d = x.shape
    depth = ln_gains.shape[0]
    assert d == D_MODEL and ffn1_w_in.shape[-1] == 2 * D_FF and mem.shape[-1] == D_MODEL
    assert even_w_in.shape[-1] == 6 * MIX_W + A_HEADS and odd_w_in.shape[-1] == 6 * MIX_W
    assert seq % (2 * TILES_PER_HALF * ATT_TILE) == 0 and seq % (2 * CROSS_TILE) == 0
    assert (bsz * seq) % (FFN_TILES_PER_STEP * TOKEN_TILE) == 0
    rope = _rope_tables(positions)
    x = x.reshape(bsz * seq, d)
    (ffn1_w_in, ffn1_w_out, ffn2_w_in, ffn2_w_out, even_w_in, even_w_out, odd_w_in, odd_w_out,
     x_w_q, x_w_kv, x_w_o) = (
        w.astype(BF16) for w in (ffn1_w_in, ffn1_w_out, ffn2_w_in, ffn2_w_out, even_w_in,
                                 even_w_out, odd_w_in, odd_w_out, x_w_q, x_w_kv, x_w_o))
    for layer in range(depth):
        g = ln_gains[layer]
        x = _ffn(x, g[0], ffn1_w_in[layer], ffn1_w_out[layer])
        if layer % 2 == 0:
            e = layer // 2
            lambda_init = 0.8 - 0.6 * math.exp(-0.3 * layer)
            aq, ak, avt, bq, bk, bvt = _even_prep(
                x, g[1], even_w_in[e], even_f_bias[e], even_qk_gains[e], rope, bsz, seq)
            left = _fox_attention(aq, ak, avt, bsz, seq)
            right = _diff_attention(bq, bk, bvt, even_lambda[e], even_subln_gain[e], lambda_init,
                                    bsz, seq)
            w_mix_out = even_w_out[e]
        else:
            o = layer // 2
            left, dq, dk, dvt = _odd_prep(x, g[1], odd_w_in[o], odd_conv_w[o], odd_qk_gains[o],
                                          bsz, seq)
            right = _band_attention(dq, dk, dvt, odd_rel_bias[o], bsz, seq)
            w_mix_out = odd_w_out[o]
        mk, mv = _mem_kv(mem, g[3], x_w_kv[layer], x_qk_gains[layer, 1])
        x = _mix_out_cross_attention(x, left, right, w_mix_out, g[2], x_w_q[layer],
                                     x_qk_gains[layer, 0], mk, mv, x_w_o[layer], bsz, seq)
        x = _ffn(x, g[4], ffn2_w_in[layer], ffn2_w_out[layer])
    return x.reshape(bsz, seq, d)
```

```python
import functools
import math

import jax
import jax.numpy as jnp
from jax import lax
from jax.experimental import pallas as pl
from jax.experimental.pallas import tpu as pltpu

D_MODEL = 1024
CHUNK = 64
HEAD_DIM = 64
ROPE_DIM = HEAD_DIM // 4
ROPE_THETA = 500000.0
RMS_EPS = 1e-6
D_FF = 2816
A_HEADS = 8
MIX_W = 512
LANES = 128
D_LEFT = 8 * CHUNK
REL_CLIP = 128
X_HEADS = 4
X_HEAD_DIM = D_MODEL // X_HEADS
NEG = -1e30
LOG2E = math.log2(math.e)
Q_SCALE = HEAD_DIM ** -0.5 * LOG2E

VMEM_LIMIT = 56 * 1024 * 1024
FFN_CHUNKS = (768, 768, 768, 512)
TOKEN_TILE = 512
PREP_TILE = 1024
ATT_TILE = 256

BF16 = jnp.bfloat16
F32 = jnp.float32


def _params(n_axes):
    return pltpu.CompilerParams(dimension_semantics=("arbitrary",) * n_axes,
                                vmem_limit_bytes=VMEM_LIMIT)


def _const_spec(shape):
    nd = len(shape)
    return pl.BlockSpec(shape, lambda *_: (0,) * nd, pipeline_mode=pl.Buffered(1))


def _rms(x):
    return x * lax.rsqrt(jnp.mean(x * x, axis=-1, keepdims=True) + RMS_EPS)


def _dot(a, b):
    return jnp.dot(a, b, preferred_element_type=F32)


def _dot_nt(a, b):
    return lax.dot_general(a, b, (((1,), (1,)), ((), ())), preferred_element_type=F32)


def _ffn_kernel(x_ref, g_ref, wi_ref, wo_ref, o_ref):
    n_tiles = x_ref.shape[0] // TOKEN_TILE
    bounds = [sum(FFN_CHUNKS[:c]) for c in range(len(FFN_CHUNKS) + 1)]
    chunks = list(zip(bounds[:-1], bounds[1:]))

    def rows(i):
        return slice(i * TOKEN_TILE, (i + 1) * TOKEN_TILE)

    def normed(i):
        x = x_ref[rows(i), :]
        return x, (_rms(x) * g_ref[...]).astype(BF16)

    def chunk_out(xn, cols):
        lo, hi = cols
        gate = _dot(xn, wi_ref[:, lo:hi])
        up = _dot(xn, wi_ref[:, D_FF + lo:D_FF + hi])
        act = (gate * (1.0 / (1.0 + jnp.exp(-gate))) * up).astype(BF16)
        return _dot(act, wo_ref[lo:hi, :])

    x, xn = normed(0)
    previous = None
    for i in range(n_tiles):
        y = chunk_out(xn, chunks[0])
        if previous is not None:
            o_ref[rows(i - 1), :] = previous[0] + 0.5 * previous[1]
        for c, cols in enumerate(chunks[1:], start=1):
            y = y + chunk_out(xn, cols)
            if c == 1 and i + 1 < n_tiles:
                following = normed(i + 1)
        previous = (x, y)
        if i + 1 < n_tiles:
            x, xn = following
    o_ref[rows(n_tiles - 1), :] = previous[0] + 0.5 * previous[1]


FFN_TILES_PER_STEP = 2


def _ffn(x, gain, w_in, w_out):
    t, d = x.shape
    tm = TOKEN_TILE * FFN_TILES_PER_STEP
    return pl.pallas_call(
        _ffn_kernel,
        out_shape=jax.ShapeDtypeStruct((t, d), F32),
        grid=(t // tm,),
        in_specs=[pl.BlockSpec((tm, d), lambda i: (i, 0)),
                  _const_spec((1, d)), _const_spec((d, 2 * D_FF)), _const_spec((D_FF, d))],
        out_specs=pl.BlockSpec((tm, d), lambda i: (i, 0)),
        compiler_params=_params(1),
        name="ffn",
    )(x, gain.reshape(1, d), w_in.astype(BF16), w_out.astype(BF16))


BD_WIDTH = 256


def _head_meansq(y, mean_bd):
    sq = (y * y).astype(BF16)
    return jnp.concatenate([_dot(sq[:, c:c + BD_WIDTH], mean_bd) for c in range(0, MIX_W, BD_WIDTH)],
                           axis=1)


def _head_norm(y, meansq, gain):
    return y * lax.rsqrt(meansq + RMS_EPS) * gain


def _block_diag_mean():
    r = jnp.arange(BD_WIDTH) // HEAD_DIM
    return jnp.where(r[:, None] == r[None, :], 1.0 / HEAD_DIM, 0.0).astype(BF16)


def _tile_heads(v):
    return jnp.tile(v, (1, MIX_W // HEAD_DIM))


def _qk_gain_rows(qk_gains):
    scale = jnp.where(jnp.arange(qk_gains.shape[0]) % 2 == 0, Q_SCALE, 1.0).astype(F32)
    return _tile_heads(qk_gains * scale[:, None])


def _store_transposed(vt_ref, v):
    for i in range(v.shape[0] // ATT_TILE):
        vt_ref[i] = v[i * ATT_TILE:(i + 1) * ATT_TILE, :].T.astype(BF16)


def _split3(f):
    hi = f.astype(BF16).astype(F32)
    rest = f - hi
    mid = rest.astype(BF16).astype(F32)
    return hi, mid, rest - mid


def _even_prep_kernel(x_ref, g_ref, wa_ref, wf_ref, fb_ref, wb_ref, gains_ref, bd_ref,
                      ra_ref, rm_ref, rp_ref,
                      aq_ref, ak_ref, avt_ref, bq_ref, bk_ref, bvt_ref, carry_ref):
    tm = x_ref.shape[0]

    @pl.when(pl.program_id(1) == 0)
    def _():
        carry_ref[...] = jnp.zeros_like(carry_ref)

    h = (_rms(x_ref[...]) * g_ref[...]).astype(BF16)
    bd = bd_ref[...]
    gains = gains_ref[...]

    def project(w_ref, part):
        return _dot(h, w_ref[:, part * MIX_W:(part + 1) * MIX_W])

    z = _dot(h, wf_ref[...]) + fb_ref[...]
    a_q = project(wa_ref, 0)

    logf = jnp.minimum(z, 0.0) - jnp.log1p(jnp.exp(-jnp.abs(z)))
    row = lax.broadcasted_iota(jnp.int32, logf.shape, 0)
    step = 1
    while step < tm:
        logf = logf + jnp.where(row >= step, pltpu.roll(logf, step, axis=0), 0.0)
        step *= 2
    cum = logf + carry_ref[...]
    carry_ref[...] = cum[tm - 1:tm, :]
    hi, mid, lo = _split3(cum * LOG2E)

    a_k = project(wa_ref, 1)
    a_q_ss = _head_meansq(a_q, bd)
    b_q = project(wb_ref, 0)
    qn = _head_norm(a_q, a_q_ss, gains[0:1])
    a_k_ss = _head_meansq(a_k, bd)
    b_k = project(wb_ref, 1)
    kn = _head_norm(a_k, a_k_ss, gains[1:2])

    lane = lax.broadcasted_iota(jnp.int32, (tm, LANES), 1)
    is_hi = (lane == HEAD_DIM) | (lane == HEAD_DIM + 3)
    is_mid = (lane == HEAD_DIM + 1) | (lane == HEAD_DIM + 4)
    ones_hi = jnp.where(lane < HEAD_DIM + 6, 1.0, 0.0)
    for hd in range(A_HEADS):
        if hd == A_HEADS // 2:
            b_q_ss = _head_meansq(b_q, bd)
            a_v = project(wa_ref, 2)
        pieces = jnp.where(is_hi, hi[:, hd:hd + 1],
                           jnp.where(is_mid, mid[:, hd:hd + 1], lo[:, hd:hd + 1]))
        blk = slice((hd // 2) * LANES, (hd // 2 + 1) * LANES)
        sq, sk = qn[:, blk], kn[:, blk]
        if hd % 2:
            sq, sk = pltpu.roll(sq, HEAD_DIM, axis=1), pltpu.roll(sk, HEAD_DIM, axis=1)
        qa = jnp.where(lane < HEAD_DIM, sq, jnp.where(lane < HEAD_DIM + 3, pieces, ones_hi))
        ka = jnp.where(lane < HEAD_DIM, sk, jnp.where(lane < HEAD_DIM + 3, 1.0,
                                                      jnp.where(lane < HEAD_DIM + 6, -pieces, 0.0)))
        aq_ref[:, hd * LANES:(hd + 1) * LANES] = qa.astype(BF16)
        ak_ref[:, hd * LANES:(hd + 1) * LANES] = ka.astype(BF16)

    rep = MIX_W // LANES
    ra = jnp.tile(ra_ref[...], (1, rep))
    rm = jnp.tile(rm_ref[...], (1, rep))
    rp = jnp.tile(rp_ref[...], (1, rep))

    def rope(v):
        half = ROPE_DIM // 2
        return v * ra + pltpu.roll(v, MIX_W - half, axis=1) * rm + pltpu.roll(v, half, axis=1) * rp

    bq_ref[...] = rope(_head_norm(b_q, b_q_ss, gains[2:3])).astype(BF16)
    b_k_ss = _head_meansq(b_k, bd)
    _store_transposed(avt_ref, a_v)
    b_v = project(wb_ref, 2)
    bk_ref[...] = rope(_head_norm(b_k, b_k_ss, gains[3:4])).astype(BF16)
    _store_transposed(bvt_ref, b_v)


def _rope_tables(positions):
    half = ROPE_DIM // 2
    inv = ROPE_THETA ** (-jnp.arange(0, ROPE_DIM, 2, dtype=F32) / ROPE_DIM)
    ang = positions.astype(F32).reshape(-1, 1) * inv
    cos = jnp.tile(jnp.cos(ang), (1, LANES // half))
    sin = jnp.tile(jnp.sin(ang), (1, LANES // half))
    in_head = jnp.arange(LANES) % HEAD_DIM
    ra = jnp.where(in_head < ROPE_DIM, cos, 1.0)
    rm = jnp.where(in_head < half, -sin, 0.0)
    rp = jnp.where((in_head >= half) & (in_head < ROPE_DIM), sin, 0.0)
    return ra, rm, rp


def _even_prep(x, gain, w_in, f_bias, qk_gains, rope, bsz, seq):
    t, d = x.shape
    tm = PREP_TILE
    a_w = 3 * MIX_W
    wa = w_in[:, :a_w].astype(BF16)
    wf = jnp.pad(w_in[:, a_w:a_w + A_HEADS], ((0, 0), (0, LANES - A_HEADS))).astype(BF16)
    wb = w_in[:, a_w + A_HEADS:].astype(BF16)
    fb = jnp.pad(f_bias, (0, LANES - A_HEADS)).reshape(1, LANES)
    n_s = seq // tm
    tok = lambda w: pl.BlockSpec((tm, w), lambda b, s: (b * n_s + s, 0))
    vt_spec = pl.BlockSpec((None, tm // ATT_TILE, MIX_W, ATT_TILE), lambda b, s: (b, s, 0, 0))
    stream = lambda w: jax.ShapeDtypeStruct((t, w), BF16)
    vt_shape = jax.ShapeDtypeStruct((bsz, seq // ATT_TILE, MIX_W, ATT_TILE), BF16)
    return pl.pallas_call(
        _even_prep_kernel,
        out_shape=[stream(A_HEADS * LANES), stream(A_HEADS * LANES), vt_shape,
                   stream(MIX_W), stream(MIX_W), vt_shape],
        grid=(bsz, n_s),
        in_specs=[tok(d), _const_spec((1, d)), _const_spec((d, a_w)), _const_spec((d, LANES)),
                  _const_spec((1, LANES)), _const_spec((d, a_w)), _const_spec((4, MIX_W)),
                  _const_spec((BD_WIDTH, BD_WIDTH)), tok(LANES), tok(LANES), tok(LANES)],
        out_specs=[tok(A_HEADS * LANES), tok(A_HEADS * LANES), vt_spec,
                   tok(MIX_W), tok(MIX_W), vt_spec],
        scratch_shapes=[pltpu.VMEM((1, LANES), F32)],
        compiler_params=_params(2),
        name="even_prep",
    )(x, gain.reshape(1, d), wa, wf, fb, wb, _qk_gain_rows(qk_gains), _block_diag_mean(), *rope)


def _lane_lo(shape):
    return lax.broadcasted_iota(jnp.int32, shape, len(shape) - 1) < HEAD_DIM


def _key_rows(g):
    if isinstance(g, int):
        return slice(g * ATT_TILE, (g + 1) * ATT_TILE)
    return pl.ds(pl.multiple_of(g * ATT_TILE, ATT_TILE), ATT_TILE)


SUM_ROWS = 16
MXU_LOOKAHEAD = 5
BAND_LOOKAHEAD = 6


def _with_sum_rows(value_t):
    return jnp.concatenate([value_t, jnp.ones((SUM_ROWS, value_t.shape[1]), BF16)], axis=0)


def _normalise(acc, v_rows):
    return acc[:v_rows] / acc[v_rows:v_rows + 1]


def _causal_attention(tiles, n_sub, queries, key_block, value_t, diag_visible, v_rows, finish):
    tq = ATT_TILE
    items = [(i, g, sub) for i, tile in enumerate(tiles) for g in range(tile + 1)
             for sub in range(n_sub)]
    maxes = [[jnp.full((1, tq), NEG, F32)] * n_sub for _ in tiles]
    accs = [[jnp.zeros((v_rows + SUM_ROWS, tq), F32)] * n_sub for _ in tiles]

    def scores(i, g, sub):
        s = _dot_nt(key_block(g, sub), queries[i][sub])
        return jnp.where(diag_visible, s, NEG) if g == tiles[i] else s

    def exponentials(i, g, sub, s):
        m_old = maxes[i][sub]
        m_new = jnp.maximum(m_old, jnp.max(s, axis=0, keepdims=True))
        maxes[i][sub] = m_new
        return jnp.exp2(s - m_new).astype(BF16), jnp.exp2(m_old - m_new)

    def accumulate(i, g, sub, p, alpha):
        accs[i][sub] = alpha * accs[i][sub] + _dot(_with_sum_rows(value_t(g, sub)), p)
        if g == tiles[i] and sub == n_sub - 1:
            finish(i, accs[i])

    in_flight = []
    for item in items:
        in_flight.append((item, exponentials(*item, scores(*item))))
        if len(in_flight) > MXU_LOOKAHEAD:
            done, (p, alpha) = in_flight.pop(0)
            accumulate(*done, p, alpha)
    for done, (p, alpha) in in_flight:
        accumulate(*done, p, alpha)


HEADS_PER_STEP = 4
TILES_PER_HALF = 4


def _step_tiles(n_tiles, j):
    first = [TILES_PER_HALF * j + i for i in range(TILES_PER_HALF)]
    return first + [n_tiles - 1 - t for t in reversed(first)]


def _for_each_step(n_tiles, body):
    for j in range(n_tiles // (2 * TILES_PER_HALF)):
        pl.when(pl.program_id(2) == j)(functools.partial(body, _step_tiles(n_tiles, j)))


def _tile_views(front_ref, back_ref):
    rows = [slice(i * ATT_TILE, (i + 1) * ATT_TILE) for i in range(TILES_PER_HALF)]
    return [(front_ref, r) for r in rows] + [(back_ref, r) for r in rows]


def _mirrored_specs(n_tiles, q_width, out_width):
    rows = TILES_PER_HALF * ATT_TILE
    n_steps = n_tiles // (2 * TILES_PER_HALF)
    q_front = pl.BlockSpec((None, rows, q_width), lambda b, p, j: (b, j, p))
    q_back = pl.BlockSpec((None, rows, q_width), lambda b, p, j: (b, 2 * n_steps - 1 - j, p))
    o_front = pl.BlockSpec((None, rows, out_width), lambda b, p, j: (b, j, p))
    o_back = pl.BlockSpec((None, rows, out_width), lambda b, p, j: (b, n_steps - 1 - j, p))
    return q_front, q_back, o_front, o_back


def _fox_kernel(qa_ref, qb_ref, k_ref, vt_ref, oa_ref, ob_ref, *, n_tiles):
    tq = ATT_TILE

    def key_block(g, sub):
        return k_ref[_key_rows(g), sub * LANES:(sub + 1) * LANES]

    def value_t(g, sub):
        return vt_ref[g, sub * HEAD_DIM:(sub + 1) * HEAD_DIM, :]

    def body(tiles):
        queries = [[q_ref[rows, sub * LANES:(sub + 1) * LANES] for sub in range(HEADS_PER_STEP)]
                   for q_ref, rows in _tile_views(qa_ref, qb_ref)]
        outputs = _tile_views(oa_ref, ob_ref)
        causal = (lax.broadcasted_iota(jnp.int32, (tq, tq), 0)
                  <= lax.broadcasted_iota(jnp.int32, (tq, tq), 1))

        def finish(i, accs):
            o_ref, rows = outputs[i]
            o_t = jnp.concatenate([_normalise(acc, HEAD_DIM) for acc in accs], axis=0)
            o_ref[rows, :] = o_t.T.astype(o_ref.dtype)

        _causal_attention(tiles, HEADS_PER_STEP, queries, key_block, value_t, causal, HEAD_DIM,
                          finish)

    _for_each_step(n_tiles, body)


def _fox_attention(q, k, vt, bsz, seq):
    tq = ATT_TILE
    n_tiles = seq // tq
    q3 = q.reshape(bsz, seq, A_HEADS * LANES)
    k3 = k.reshape(bsz, seq, A_HEADS * LANES)
    out_width = HEADS_PER_STEP * HEAD_DIM
    q_a, q_b, o_a, o_b = _mirrored_specs(n_tiles, HEADS_PER_STEP * LANES, out_width)
    half_out = jax.ShapeDtypeStruct((bsz, seq // 2, MIX_W), BF16)
    out_a, out_b = pl.pallas_call(
        functools.partial(_fox_kernel, n_tiles=n_tiles),
        out_shape=[half_out, half_out],
        grid=(bsz, A_HEADS // HEADS_PER_STEP, n_tiles // (2 * TILES_PER_HALF)),
        in_specs=[q_a, q_b,
                  pl.BlockSpec((None, seq, HEADS_PER_STEP * LANES), lambda b, p, a: (b, 0, p)),
                  pl.BlockSpec((None, n_tiles, out_width, tq), lambda b, p, a: (b, 0, p, 0))],
        out_specs=[o_a, o_b],
        compiler_params=_params(3),
        name="fox_attention",
    )(q3, q3, k3, vt)
    return out_a, out_b


def _diff_kernel(qa_ref, qb_ref, k_ref, vt_ref, lam_ref, sg_ref, oa_ref, ob_ref, *,
                 lambda_init, n_tiles):
    tq = ATT_TILE
    lo = _lane_lo((tq, LANES))
    lp = lam_ref[...]
    lam = (jnp.exp(jnp.sum(lp[0:1] * lp[1:2], axis=1, keepdims=True))
           - jnp.exp(jnp.sum(lp[2:3] * lp[3:4], axis=1, keepdims=True)) + lambda_init)

    def half_of(q_ref, rows, sub):
        q = q_ref[rows, sub // 2 * LANES:(sub // 2 + 1) * LANES]
        zero = jnp.zeros_like(q)
        return jnp.where(lo, q, zero) if sub % 2 == 0 else jnp.where(lo, zero, q)

    def key_block(g, sub):
        return k_ref[_key_rows(g), sub // 2 * LANES:(sub // 2 + 1) * LANES]

    def value_t(g, sub):
        return vt_ref[g, sub // 2 * LANES:(sub // 2 + 1) * LANES, :]

    def body(tiles):
        queries = [[half_of(q_ref, rows, sub) for sub in range(HEADS_PER_STEP)]
                   for q_ref, rows in _tile_views(qa_ref, qb_ref)]
        outputs = _tile_views(oa_ref, ob_ref)
        chunk_causal = (lax.broadcasted_iota(jnp.int32, (tq, tq), 0) // CHUNK
                        <= lax.broadcasted_iota(jnp.int32, (tq, tq), 1) // CHUNK)

        def finish(i, accs):
            o_ref, rows = outputs[i]
            for hd in range(HEADS_PER_STEP // 2):
                o1, o2 = (_normalise(acc, LANES) for acc in accs[2 * hd:2 * hd + 2])
                o = (o1 - lam * o2).T
                o_ref[rows, hd * LANES:(hd + 1) * LANES] = (
                    _rms(o) * sg_ref[...] * (1.0 - lambda_init)).astype(o_ref.dtype)

        _causal_attention(tiles, HEADS_PER_STEP, queries, key_block, value_t, chunk_causal, LANES,
                          finish)

    _for_each_step(n_tiles, body)


def _diff_attention(q, k, vt, lam_params, subln_gain, lambda_init, bsz, seq):
    tq = ATT_TILE
    n_tiles = seq // tq
    q3, k3 = (a.reshape(bsz, seq, MIX_W) for a in (q, k))
    width = HEADS_PER_STEP // 2 * LANES
    q_a, q_b, o_a, o_b = _mirrored_specs(n_tiles, width, width)
    half_out = jax.ShapeDtypeStruct((bsz, seq // 2, MIX_W), BF16)
    out_a, out_b = pl.pallas_call(
        functools.partial(_diff_kernel, lambda_init=lambda_init, n_tiles=n_tiles),
        out_shape=[half_out, half_out],
        grid=(bsz, MIX_W // width, n_tiles // (2 * TILES_PER_HALF)),
        in_specs=[q_a, q_b,
                  pl.BlockSpec((None, seq, width), lambda b, h, a: (b, 0, h)),
                  pl.BlockSpec((None, n_tiles, width, tq), lambda b, h, a: (b, 0, h, 0)),
                  _const_spec((4, HEAD_DIM)), _const_spec((1, LANES))],
        out_specs=[o_a, o_b],
        compiler_params=_params(3),
        name="diff_attention",
    )(q3, q3, k3, vt, lam_params, subln_gain.reshape(1, LANES))
    return out_a, out_b


def _odd_prep_kernel(x_ref, g_ref, wc_ref, wd_ref, cw_ref, gains_ref, bd_ref,
                     c_ref, dq_ref, dk_ref, dvt_ref, carry_ref):
    tm = x_ref.shape[0]

    @pl.when(pl.program_id(1) == 0)
    def _():
        carry_ref[...] = jnp.zeros_like(carry_ref)

    h = (_rms(x_ref[...]) * g_ref[...]).astype(BF16)
    bd = bd_ref[...]
    gains = gains_ref[...]

    def project(w_ref, part):
        return _dot(h, w_ref[:, part * MIX_W:(part + 1) * MIX_W])

    d_q = project(wd_ref, 0)
    d_k = project(wd_ref, 1)
    d_q_ss = _head_meansq(d_q, bd)
    c_c = project(wc_ref, 1)
    d_k_ss = _head_meansq(d_k, bd)
    c_h = project(wc_ref, 2)
    c_b = project(wc_ref, 0)
    d_v = project(wd_ref, 2)

    u = c_c * c_h
    prev = carry_ref[...]
    carry_ref[...] = u[tm - 8:, :]
    row = lax.broadcasted_iota(jnp.int32, u.shape, 0)
    u1 = jnp.where(row == 0, prev[7:8], pltpu.roll(u, 1, axis=0))
    u2 = jnp.where(row == 0, prev[6:7], jnp.where(row == 1, prev[7:8], pltpu.roll(u, 2, axis=0)))
    cw = cw_ref[...]
    conv = cw[0:1] * u2 + cw[1:2] * u1 + cw[2:3] * u
    c_ref[...] = (c_b * conv).astype(BF16)

    dq_ref[...] = _head_norm(d_q, d_q_ss, gains[0:1]).astype(BF16)
    dk_ref[...] = _head_norm(d_k, d_k_ss, gains[1:2]).astype(BF16)
    _store_transposed(dvt_ref, d_v)


def _odd_prep(x, gain, w_in, conv_w, qk_gains, bsz, seq):
    t, d = x.shape
    tm = PREP_TILE
    w3 = 3 * MIX_W
    n_s = seq // tm
    tok = lambda w: pl.BlockSpec((tm, w), lambda b, s: (b * n_s + s, 0))
    stream = jax.ShapeDtypeStruct((t, MIX_W), BF16)
    return pl.pallas_call(
        _odd_prep_kernel,
        out_shape=[stream] * 3 + [jax.ShapeDtypeStruct((bsz, seq // ATT_TILE, MIX_W, ATT_TILE), BF16)],
        grid=(bsz, n_s),
        in_specs=[tok(d), _const_spec((1, d)), _const_spec((d, w3)), _const_spec((d, w3)),
                  _const_spec((3, MIX_W)), _const_spec((2, MIX_W)), _const_spec((BD_WIDTH, BD_WIDTH))],
        out_specs=[tok(MIX_W)] * 3 + [pl.BlockSpec((None, tm // ATT_TILE, MIX_W, ATT_TILE),
                                                   lambda b, s: (b, s, 0, 0))],
        scratch_shapes=[pltpu.VMEM((8, MIX_W), F32)],
        compiler_params=_params(2),
        name="odd_prep",
    )(x, gain.reshape(1, d), w_in[:, :w3].astype(BF16), w_in[:, w3:].astype(BF16), conv_w,
      _qk_gain_rows(qk_gains), _block_diag_mean())


BAND_GROUPS = 1 + D_LEFT // ATT_TILE


MAX_BAND_TILES = 8
BAND_WIDTH = BAND_GROUPS * ATT_TILE
BIAS_ROW = BAND_WIDTH + ATT_TILE


def _band_kernel(q_ref, k_ref, vt_ref, w_ref, o_ref, bias_ref):
    tq = ATT_TILE
    band_tiles = q_ref.shape[0] // tq
    j = pl.program_id(2)

    @pl.when((pl.program_id(1) == 0) & (j == 0))
    def _():
        key = lax.broadcasted_iota(jnp.int32, (BAND_WIDTH, tq), 0)
        chunk_start = lax.broadcasted_iota(jnp.int32, (BAND_WIDTH, tq), 1) // CHUNK * CHUNK
        in_band = (key >= chunk_start) & (key < chunk_start + D_LEFT + CHUNK)
        for sub in range(2):
            rows = jnp.broadcast_to(w_ref[sub], (BAND_WIDTH, BIAS_ROW))
            skew = pltpu.roll(rows, tq + 1, axis=1, stride=1, stride_axis=0)
            table = jnp.where(in_band, skew[:, :tq] * LOG2E, NEG)
            for grp in range(BAND_GROUPS):
                bias_ref[sub, grp] = table[grp * tq:(grp + 1) * tq, :]

    lo = _lane_lo((tq, LANES))
    before_start = jnp.where(j == 0, NEG, 0.0)
    q_subs, items = {}, []
    for tile in range(band_tiles):
        q = q_ref[tile * tq:(tile + 1) * tq, :]
        zero = jnp.zeros_like(q)
        q_subs[tile] = (jnp.where(lo, q, zero), jnp.where(lo, zero, q))
        items += [(tile, grp, sub) for grp in range(BAND_GROUPS) for sub in range(2)]

    def may_precede(tile, grp):
        return tile + grp < BAND_GROUPS - 1

    def group_index(tile, grp):
        g = band_tiles * j + tile + grp - (BAND_GROUPS - 1)
        return jnp.maximum(g, 0) if may_precede(tile, grp) else g

    def scores(tile, grp, sub):
        s = _dot_nt(k_ref[_key_rows(group_index(tile, grp)), :], q_subs[tile][sub]) + bias_ref[sub, grp]
        return s + before_start if may_precede(tile, grp) else s

    maxes, accs = {}, {}

    def exponentials(tile, grp, sub, s):
        mx = jnp.max(s, axis=0, keepdims=True)
        m_old = maxes.get((tile, sub))
        m_new = mx if m_old is None else jnp.maximum(m_old, mx)
        maxes[tile, sub] = m_new
        return jnp.exp2(s - m_new).astype(BF16), None if m_old is None else jnp.exp2(m_old - m_new)

    def accumulate(tile, grp, sub, p, alpha):
        value_t = _with_sum_rows(vt_ref[group_index(tile, grp), sub * HEAD_DIM:(sub + 1) * HEAD_DIM, :])
        acc_n = _dot(value_t, p)
        accs[tile, sub] = acc_n if alpha is None else alpha * accs[tile, sub] + acc_n
        if grp == BAND_GROUPS - 1 and sub == 1:
            o_t = jnp.concatenate([_normalise(accs[tile, s], HEAD_DIM) for s in range(2)], axis=0)
            o_ref[tile * tq:(tile + 1) * tq, :] = o_t.T.astype(o_ref.dtype)

    in_flight = []
    for item in items:
        in_flight.append((item, exponentials(*item, scores(*item))))
        if len(in_flight) > BAND_LOOKAHEAD:
            done, (p, alpha) = in_flight.pop(0)
            accumulate(*done, p, alpha)
    for done, (p, alpha) in in_flight:
        accumulate(*done, p, alpha)


def _band_bias_rows(rel_table):
    n_lo = ATT_TILE - 1 - REL_CLIP
    n_hi = BIAS_ROW - n_lo - (2 * REL_CLIP + 1)
    w = jnp.concatenate([jnp.repeat(rel_table[:, :1], n_lo, axis=1), rel_table,
                         jnp.repeat(rel_table[:, -1:], n_hi, axis=1)], axis=1)
    return w.astype(F32).reshape(rel_table.shape[0], 1, BIAS_ROW)


def _band_attention(q, k, vt, rel_table, bsz, seq):
    tq = ATT_TILE
    n_q = seq // tq
    band_tiles = min(MAX_BAND_TILES, n_q)
    q3, k3 = (a.reshape(bsz, seq, MIX_W) for a in (q, k))
    q_spec = pl.BlockSpec((None, band_tiles * tq, LANES), lambda p, b, j: (b, j, p))
    out = pl.pallas_call(
        _band_kernel,
        out_shape=jax.ShapeDtypeStruct((bsz, seq, MIX_W), BF16),
        grid=(MIX_W // LANES, bsz, n_q // band_tiles),
        in_specs=[q_spec,
                  pl.BlockSpec((None, seq, LANES), lambda p, b, j: (b, 0, p)),
                  pl.BlockSpec((None, n_q, LANES, tq), lambda p, b, j: (b, 0, p, 0)),
                  pl.BlockSpec((2, 1, BIAS_ROW), lambda p, b, j: (p, 0, 0))],
        out_specs=q_spec,
        scratch_shapes=[pltpu.VMEM((2, BAND_GROUPS, tq, tq), F32)],
        compiler_params=_params(3),
        name="band_attention",
    )(q3, k3, vt, _band_bias_rows(rel_table))
    return out.reshape(bsz * seq, MIX_W)


def _mem_kv_kernel(mem_ref, g_ref, w_ref, kg_ref, k_ref, v_ref):
    mem_n = (_rms(mem_ref[...]) * g_ref[...]).astype(BF16)
    kv = _dot(mem_n, w_ref[...])
    kg = kg_ref[...]
    for hd in range(X_HEADS):
        sl = slice(hd * X_HEAD_DIM, (hd + 1) * X_HEAD_DIM)
        k_ref[:, sl] = (_rms(kv[:, sl]) * kg).astype(BF16)
    v_ref[...] = kv[:, D_MODEL:].astype(BF16)


def _mem_kv(mem, gain, w_kv, k_gain):
    bsz, n_mem, d = mem.shape
    blk = pl.BlockSpec((None, n_mem, d), lambda b: (b, 0, 0))
    out = jax.ShapeDtypeStruct((bsz, n_mem, d), BF16)
    return pl.pallas_call(
        _mem_kv_kernel,
        out_shape=[out, out],
        grid=(bsz,),
        in_specs=[blk, _const_spec((1, d)), _const_spec((d, 2 * d)), _const_spec((1, X_HEAD_DIM))],
        out_specs=[blk, blk],
        compiler_params=_params(1),
        name="mem_kv",
    )(mem, gain.reshape(1, d), w_kv.astype(BF16), k_gain.reshape(1, X_HEAD_DIM))


def _cross_kernel(x_ref, left_a_ref, left_b_ref, right_a_ref, right_b_ref, wl_ref, wr_ref, g_ref,
                  wq_ref, qg_ref, k_ref, v_ref, wo_ref, o_ref):
    first_half = pl.program_id(1) < pl.num_programs(1) // 2
    left = jnp.where(first_half, left_a_ref[...], left_b_ref[...])
    right = jnp.where(first_half, right_a_ref[...], right_b_ref[...])
    x = x_ref[...] + _dot(left, wl_ref[...]) + _dot(right, wr_ref[...])
    h = (_rms(x) * g_ref[...]).astype(BF16)
    q = _dot(h, wq_ref[...])
    qg = qg_ref[...]
    head_cols = [slice(hd * X_HEAD_DIM, (hd + 1) * X_HEAD_DIM) for hd in range(X_HEADS)]

    def probabilities(sl):
        qh = (_rms(q[:, sl]) * qg * X_HEAD_DIM ** -0.5).astype(BF16)
        s = _dot_nt(qh, k_ref[:, sl])
        p = jnp.exp(s - jnp.max(s, axis=-1, keepdims=True))
        return p.astype(BF16), jnp.sum(p, axis=-1, keepdims=True)

    heads = []
    ahead = probabilities(head_cols[0])
    for hd, sl in enumerate(head_cols):
        p, l = ahead
        if hd + 1 < X_HEADS:
            ahead = probabilities(head_cols[hd + 1])
        heads.append((_dot(p, v_ref[:, sl]) / l).astype(BF16))
    o_ref[...] = x + _dot(jnp.concatenate(heads, axis=1), wo_ref[...])


def _halves(stream, bsz, seq):
    if isinstance(stream, tuple):
        return stream[0], stream[1], 0
    full = stream.reshape(bsz, seq, MIX_W)
    return full, full, seq // CROSS_TILE // 2


CROSS_TILE = 1024


def _mix_out_cross_attention(x, left, right, w_mix_out, gain, w_q, q_gain, k, v, w_o, bsz, seq):
    t, d = x.shape
    tq = CROSS_TILE
    n_s = seq // tq
    half = n_s // 2
    n_mem = k.shape[1]
    tok = pl.BlockSpec((tq, d), lambda b, s: (b * n_s + s, 0))
    mem_spec = pl.BlockSpec((None, n_mem, d), lambda b, s: (b, 0, 0))
    w_mix = w_mix_out.astype(BF16)
    stream_args, stream_specs = [], []
    for first, second, offset in (_halves(left, bsz, seq), _halves(right, bsz, seq)):
        stream_args += [first, second]
        stream_specs += [
            pl.BlockSpec((None, tq, MIX_W), lambda b, s: (b, jnp.minimum(s, half - 1), 0)),
            pl.BlockSpec((None, tq, MIX_W),
                         lambda b, s, offset=offset: (b, offset + jnp.maximum(s - half, 0), 0))]
    return pl.pallas_call(
        _cross_kernel,
        out_shape=jax.ShapeDtypeStruct((t, d), F32),
        grid=(bsz, n_s),
        in_specs=[tok, *stream_specs, _const_spec((MIX_W, d)), _const_spec((MIX_W, d)),
                  _const_spec((1, d)), _const_spec((d, d)), _const_spec((1, X_HEAD_DIM)),
                  mem_spec, mem_spec, _const_spec((d, d))],
        out_specs=tok,
        compiler_params=_params(2),
        name="cross_attention",
    )(x, *stream_args, w_mix[:MIX_W], w_mix[MIX_W:], gain.reshape(1, d), w_q.astype(BF16),
      q_gain.reshape(1, X_HEAD_DIM), k, v, w_o.astype(BF16))


def kernel(x, mem, positions, ln_gains, ffn1_w_in, ffn1_w_out, ffn2_w_in, ffn2_w_out, even_w_in, even_f_bias, even_qk_gains, even_lambda, even_subln_gain, even_w_out, odd_w_in, odd_conv_w, odd_qk_gains, odd_rel_bias, odd_w_out, x_w_q, x_w_kv, x_qk_gains, x_w_o):
    bsz, seq, d = x.shape
    depth = ln_gains.shape[0]
    assert d == D_MODEL and ffn1_w_in.shape[-1] == 2 * D_FF and mem.shape[-1] == D_MODEL
    assert even_w_in.shape[-1] == 6 * MIX_W + A_HEADS and odd_w_in.shape[-1] == 6 * MIX_W
    assert seq % (2 * TILES_PER_HALF * ATT_TILE) == 0 and seq % (2 * CROSS_TILE) == 0
    assert (bsz * seq) % (FFN_TILES_PER_STEP * TOKEN_TILE) == 0
    rope = _rope_tables(positions)
    x = x.reshape(bsz * seq, d)
    (ffn1_w_in, ffn1_w_out, ffn2_w_in, ffn2_w_out, even_w_in, even_w_out, odd_w_in, odd_w_out,
     x_w_q, x_w_kv, x_w_o) = (
        w.astype(BF16) for w in (ffn1_w_in, ffn1_w_out, ffn2_w_in, ffn2_w_out, even_w_in,
                                 even_w_out, odd_w_in, odd_w_out, x_w_q, x_w_kv, x_w_o))
    for layer in range(depth):
        g = ln_gains[layer]
        x = _ffn(x, g[0], ffn1_w_in[layer], ffn1_w_out[layer])
        if layer % 2 == 0:
            e = layer // 2
            lambda_init = 0.8 - 0.6 * math.exp(-0.3 * layer)
            aq, ak, avt, bq, bk, bvt = _even_prep(
                x, g[1], even_w_in[e], even_f_bias[e], even_qk_gains[e], rope, bsz, seq)
            left = _fox_attention(aq, ak, avt, bsz, seq)
            right = _diff_attention(bq, bk, bvt, even_lambda[e], even_subln_gain[e], lambda_init,
                                    bsz, seq)
            w_mix_out = even_w_out[e]
        else:
            o = layer // 2
            left, dq, dk, dvt = _odd_prep(x, g[1], odd_w_in[o], odd_conv_w[o], odd_qk_gains[o],
                                          bsz, seq)
            right = _band_attention(dq, dk, dvt, odd_rel_bias[o], bsz, seq)
            w_mix_out = odd_w_out[o]
        mk, mv = _mem_kv(mem, g[3], x_w_kv[layer], x_qk_gains[layer, 1])
        x = _mix_out_cross_attention(x, left, right, w_mix_out, g[2], x_w_q[layer],
                                     x_qk_gains[layer, 0], mk, mv, x_w_o[layer], bsz, seq)
        x = _ffn(x, g[4], ffn2_w_in[layer], ffn2_w_out[layer])
    return x.reshape(bsz, seq, d)
```

```python
import functools
import math

import jax
import jax.numpy as jnp
from jax import lax
from jax.experimental import pallas as pl
from jax.experimental.pallas import tpu as pltpu

D_MODEL = 1024
CHUNK = 64
HEAD_DIM = 64
ROPE_DIM = HEAD_DIM // 4
ROPE_THETA = 500000.0
RMS_EPS = 1e-6
D_FF = 2816
A_HEADS = 8
MIX_W = 512
LANES = 128
D_LEFT = 8 * CHUNK
REL_CLIP = 128
X_HEADS = 4
X_HEAD_DIM = D_MODEL // X_HEADS
NEG = -1e30
LOG2E = math.log2(math.e)
Q_SCALE = HEAD_DIM ** -0.5 * LOG2E

VMEM_LIMIT = 56 * 1024 * 1024
FFN_CHUNKS = (768, 768, 768, 512)
TOKEN_TILE = 512
PREP_TILE = 1024
ATT_TILE = 256

BF16 = jnp.bfloat16
F32 = jnp.float32


def _params(n_axes):
    return pltpu.CompilerParams(dimension_semantics=("arbitrary",) * n_axes,
                                vmem_limit_bytes=VMEM_LIMIT)


def _const_spec(shape):
    nd = len(shape)
    return pl.BlockSpec(shape, lambda *_: (0,) * nd, pipeline_mode=pl.Buffered(1))


def _rms(x):
    return x * lax.rsqrt(jnp.mean(x * x, axis=-1, keepdims=True) + RMS_EPS)


def _dot(a, b):
    return jnp.dot(a, b, preferred_element_type=F32)


def _dot_nt(a, b):
    return lax.dot_general(a, b, (((1,), (1,)), ((), ())), preferred_element_type=F32)


def _ffn_kernel(x_ref, g_ref, wi_ref, wo_ref, o_ref):
    n_tiles = x_ref.shape[0] // TOKEN_TILE
    bounds = [sum(FFN_CHUNKS[:c]) for c in range(len(FFN_CHUNKS) + 1)]
    chunks = list(zip(bounds[:-1], bounds[1:]))

    def rows(i):
        return slice(i * TOKEN_TILE, (i + 1) * TOKEN_TILE)

    def normed(i):
        x = x_ref[rows(i), :]
        return x, (_rms(x) * g_ref[...]).astype(BF16)

    def chunk_out(xn, cols):
        lo, hi = cols
        gate = _dot(xn, wi_ref[:, lo:hi])
        up = _dot(xn, wi_ref[:, D_FF + lo:D_FF + hi])
        act = (gate * (1.0 / (1.0 + jnp.exp(-gate))) * up).astype(BF16)
        return _dot(act, wo_ref[lo:hi, :])

    x, xn = normed(0)
    previous = None
    for i in range(n_tiles):
        y = chunk_out(xn, chunks[0])
        if previous is not None:
            o_ref[rows(i - 1), :] = previous[0] + 0.5 * previous[1]
        for c, cols in enumerate(chunks[1:], start=1):
            y = y + chunk_out(xn, cols)
            if c == 1 and i + 1 < n_tiles:
                following = normed(i + 1)
        previous = (x, y)
        if i + 1 < n_tiles:
            x, xn = following
    o_ref[rows(n_tiles - 1), :] = previous[0] + 0.5 * previous[1]


FFN_TILES_PER_STEP = 2


def _ffn(x, gain, w_in, w_out):
    t, d = x.shape
    tm = TOKEN_TILE * FFN_TILES_PER_STEP
    return pl.pallas_call(
        _ffn_kernel,
        out_shape=jax.ShapeDtypeStruct((t, d), F32),
        grid=(t // tm,),
        in_specs=[pl.BlockSpec((tm, d), lambda i: (i, 0)),
                  _const_spec((1, d)), _const_spec((d, 2 * D_FF)), _const_spec((D_FF, d))],
        out_specs=pl.BlockSpec((tm, d), lambda i: (i, 0)),
        compiler_params=_params(1),
        name="ffn",
    )(x, gain.reshape(1, d), w_in.astype(BF16), w_out.astype(BF16))


BD_WIDTH = 256


def _head_meansq(y, mean_bd):
    sq = (y * y).astype(BF16)
    return jnp.concatenate([_dot(sq[:, c:c + BD_WIDTH], mean_bd) for c in range(0, MIX_W, BD_WIDTH)],
                           axis=1)


def _head_norm(y, meansq, gain):
    return y * lax.rsqrt(meansq + RMS_EPS) * gain


def _block_diag_mean():
    r = jnp.arange(BD_WIDTH) // HEAD_DIM
    return jnp.where(r[:, None] == r[None, :], 1.0 / HEAD_DIM, 0.0).astype(BF16)


def _tile_heads(v):
    return jnp.tile(v, (1, MIX_W // HEAD_DIM))


def _qk_gain_rows(qk_gains):
    scale = jnp.where(jnp.arange(qk_gains.shape[0]) % 2 == 0, Q_SCALE, 1.0).astype(F32)
    return _tile_heads(qk_gains * scale[:, None])


def _store_transposed(vt_ref, v):
    for i in range(v.shape[0] // ATT_TILE):
        vt_ref[i] = v[i * ATT_TILE:(i + 1) * ATT_TILE, :].T.astype(BF16)


def _split3(f):
    hi = f.astype(BF16).astype(F32)
    rest = f - hi
    mid = rest.astype(BF16).astype(F32)
    return hi, mid, rest - mid


def _even_prep_kernel(x_ref, g_ref, wa_ref, wf_ref, fb_ref, wb_ref, gains_ref, bd_ref,
                      ra_ref, rm_ref, rp_ref,
                      aq_ref, ak_ref, avt_ref, bq_ref, bk_ref, bvt_ref, carry_ref):
    tm = x_ref.shape[0]

    @pl.when(pl.program_id(1) == 0)
    def _():
        carry_ref[...] = jnp.zeros_like(carry_ref)

    h = (_rms(x_ref[...]) * g_ref[...]).astype(BF16)
    bd = bd_ref[...]
    gains = gains_ref[...]

    def project(w_ref, part):
        return _dot(h, w_ref[:, part * MIX_W:(part + 1) * MIX_W])

    z = _dot(h, wf_ref[...]) + fb_ref[...]
    a_q = project(wa_ref, 0)

    logf = jnp.minimum(z, 0.0) - jnp.log1p(jnp.exp(-jnp.abs(z)))
    row = lax.broadcasted_iota(jnp.int32, logf.shape, 0)
    step = 1
    while step < tm:
        logf = logf + jnp.where(row >= step, pltpu.roll(logf, step, axis=0), 0.0)
        step *= 2
    cum = logf + carry_ref[...]
    carry_ref[...] = cum[tm - 1:tm, :]
    hi, mid, lo = _split3(cum * LOG2E)

    a_k = project(wa_ref, 1)
    a_q_ss = _head_meansq(a_q, bd)
    b_q = project(wb_ref, 0)
    qn = _head_norm(a_q, a_q_ss, gains[0:1])
    a_k_ss = _head_meansq(a_k, bd)
    b_k = project(wb_ref, 1)
    kn = _head_norm(a_k, a_k_ss, gains[1:2])

    lane = lax.broadcasted_iota(jnp.int32, (tm, LANES), 1)
    is_hi = (lane == HEAD_DIM) | (lane == HEAD_DIM + 3)
    is_mid = (lane == HEAD_DIM + 1) | (lane == HEAD_DIM + 4)
    ones_hi = jnp.where(lane < HEAD_DIM + 6, 1.0, 0.0)
    for hd in range(A_HEADS):
        if hd == A_HEADS // 2:
            b_q_ss = _head_meansq(b_q, bd)
            a_v = project(wa_ref, 2)
        pieces = jnp.where(is_hi, hi[:, hd:hd + 1],
                           jnp.where(is_mid, mid[:, hd:hd + 1], lo[:, hd:hd + 1]))
        blk = slice((hd // 2) * LANES, (hd // 2 + 1) * LANES)
        sq, sk = qn[:, blk], kn[:, blk]
        if hd % 2:
            sq, sk = pltpu.roll(sq, HEAD_DIM, axis=1), pltpu.roll(sk, HEAD_DIM, axis=1)
        qa = jnp.where(lane < HEAD_DIM, sq, jnp.where(lane < HEAD_DIM + 3, pieces, ones_hi))
        ka = jnp.where(lane < HEAD_DIM, sk, jnp.where(lane < HEAD_DIM + 3, 1.0,
                                                      jnp.where(lane < HEAD_DIM + 6, -pieces, 0.0)))
        aq_ref[:, hd * LANES:(hd + 1) * LANES] = qa.astype(BF16)
        ak_ref[:, hd * LANES:(hd + 1) * LANES] = ka.astype(BF16)

    rep = MIX_W // LANES
    ra = jnp.tile(ra_ref[...], (1, rep))
    rm = jnp.tile(rm_ref[...], (1, rep))
    rp = jnp.tile(rp_ref[...], (1, rep))

    def rope(v):
        half = ROPE_DIM // 2
        return v * ra + pltpu.roll(v, MIX_W - half, axis=1) * rm + pltpu.roll(v, half, axis=1) * rp

    bq_ref[...] = rope(_head_norm(b_q, b_q_ss, gains[2:3])).astype(BF16)
    b_k_ss = _head_meansq(b_k, bd)
    _store_transposed(avt_ref, a_v)
    b_v = project(wb_ref, 2)
    bk_ref[...] = rope(_head_norm(b_k, b_k_ss, gains[3:4])).astype(BF16)
    _store_transposed(bvt_ref, b_v)


def _rope_tables(positions):
    half = ROPE_DIM // 2
    inv = ROPE_THETA ** (-jnp.arange(0, ROPE_DIM, 2, dtype=F32) / ROPE_DIM)
    ang = positions.astype(F32).reshape(-1, 1) * inv
    cos = jnp.tile(jnp.cos(ang), (1, LANES // half))
    sin = jnp.tile(jnp.sin(ang), (1, LANES // half))
    in_head = jnp.arange(LANES) % HEAD_DIM
    ra = jnp.where(in_head < ROPE_DIM, cos, 1.0)
    rm = jnp.where(in_head < half, -sin, 0.0)
    rp = jnp.where((in_head >= half) & (in_head < ROPE_DIM), sin, 0.0)
    return ra, rm, rp


def _even_prep(x, gain, w_in, f_bias, qk_gains, rope, bsz, seq):
    t, d = x.shape
    tm = PREP_TILE
    a_w = 3 * MIX_W
    wa = w_in[:, :a_w].astype(BF16)
    wf = jnp.pad(w_in[:, a_w:a_w + A_HEADS], ((0, 0), (0, LANES - A_HEADS))).astype(BF16)
    wb = w_in[:, a_w + A_HEADS:].astype(BF16)
    fb = jnp.pad(f_bias, (0, LANES - A_HEADS)).reshape(1, LANES)
    n_s = seq // tm
    tok = lambda w: pl.BlockSpec((tm, w), lambda b, s: (b * n_s + s, 0))
    vt_spec = pl.BlockSpec((None, tm // ATT_TILE, MIX_W, ATT_TILE), lambda b, s: (b, s, 0, 0))
    stream = lambda w: jax.ShapeDtypeStruct((t, w), BF16)
    vt_shape = jax.ShapeDtypeStruct((bsz, seq // ATT_TILE, MIX_W, ATT_TILE), BF16)
    return pl.pallas_call(
        _even_prep_kernel,
        out_shape=[stream(A_HEADS * LANES), stream(A_HEADS * LANES), vt_shape,
                   stream(MIX_W), stream(MIX_W), vt_shape],
        grid=(bsz, n_s),
        in_specs=[tok(d), _const_spec((1, d)), _const_spec((d, a_w)), _const_spec((d, LANES)),
                  _const_spec((1, LANES)), _const_spec((d, a_w)), _const_spec((4, MIX_W)),
                  _const_spec((BD_WIDTH, BD_WIDTH)), tok(LANES), tok(LANES), tok(LANES)],
        out_specs=[tok(A_HEADS * LANES), tok(A_HEADS * LANES), vt_spec,
                   tok(MIX_W), tok(MIX_W), vt_spec],
        scratch_shapes=[pltpu.VMEM((1, LANES), F32)],
        compiler_params=_params(2),
        name="even_prep",
    )(x, gain.reshape(1, d), wa, wf, fb, wb, _qk_gain_rows(qk_gains), _block_diag_mean(), *rope)


def _lane_lo(shape):
    return lax.broadcasted_iota(jnp.int32, shape, len(shape) - 1) < HEAD_DIM


def _key_rows(g):
    if isinstance(g, int):
        return slice(g * ATT_TILE, (g + 1) * ATT_TILE)
    return pl.ds(pl.multiple_of(g * ATT_TILE, ATT_TILE), ATT_TILE)


SUM_ROWS = 16
MXU_LOOKAHEAD = 5
BAND_LOOKAHEAD = 6


def _with_sum_rows(value_t):
    return jnp.concatenate([value_t, jnp.ones((SUM_ROWS, value_t.shape[1]), BF16)], axis=0)


def _normalise(acc, v_rows):
    return acc[:v_rows] / acc[v_rows:v_rows + 1]


def _causal_attention(tiles, n_sub, queries, key_block, value_t, diag_visible, v_rows, finish):
    tq = ATT_TILE
    items = [(i, g, sub) for i, tile in enumerate(tiles) for g in range(tile + 1)
             for sub in range(n_sub)]
    maxes = [[jnp.full((1, tq), NEG, F32)] * n_sub for _ in tiles]
    accs = [[jnp.zeros((v_rows + SUM_ROWS, tq), F32)] * n_sub for _ in tiles]

    def scores(i, g, sub):
        s = _dot_nt(key_block(g, sub), queries[i][sub])
        return jnp.where(diag_visible, s, NEG) if g == tiles[i] else s

    def exponentials(i, g, sub, s):
        m_old = maxes[i][sub]
        m_new = jnp.maximum(m_old, jnp.max(s, axis=0, keepdims=True))
        maxes[i][sub] = m_new
        return jnp.exp2(s - m_new).astype(BF16), jnp.exp2(m_old - m_new)

    def accumulate(i, g, sub, p, alpha):
        accs[i][sub] = alpha * accs[i][sub] + _dot(_with_sum_rows(value_t(g, sub)), p)
        if g == tiles[i] and sub == n_sub - 1:
            finish(i, accs[i])

    in_flight = []
    for item in items:
        in_flight.append((item, exponentials(*item, scores(*item))))
        if len(in_flight) > MXU_LOOKAHEAD:
            done, (p, alpha) = in_flight.pop(0)
            accumulate(*done, p, alpha)
    for done, (p, alpha) in in_flight:
        accumulate(*done, p, alpha)


HEADS_PER_STEP = 4


def _tile_rows(i):
    return slice(i * ATT_TILE, (i + 1) * ATT_TILE)


def _fox_kernel(q_ref, k_ref, vt_ref, o_ref):
    tq = ATT_TILE
    tiles = list(range(q_ref.shape[0] // tq))

    def key_block(g, sub):
        return k_ref[_key_rows(g), sub * LANES:(sub + 1) * LANES]

    def value_t(g, sub):
        return vt_ref[g, sub * HEAD_DIM:(sub + 1) * HEAD_DIM, :]

    queries = [[q_ref[_tile_rows(t), sub * LANES:(sub + 1) * LANES] for sub in range(HEADS_PER_STEP)]
               for t in tiles]
    causal = (lax.broadcasted_iota(jnp.int32, (tq, tq), 0)
              <= lax.broadcasted_iota(jnp.int32, (tq, tq), 1))

    def finish(t, accs):
        o_t = jnp.concatenate([_normalise(acc, HEAD_DIM) for acc in accs], axis=0)
        o_ref[_tile_rows(t), :] = o_t.T.astype(o_ref.dtype)

    _causal_attention(tiles, HEADS_PER_STEP, queries, key_block, value_t, causal, HEAD_DIM, finish)


def _fox_attention(q, k, vt, bsz, seq):
    tq = ATT_TILE
    n_tiles = seq // tq
    q3 = q.reshape(bsz, seq, A_HEADS * LANES)
    k3 = k.reshape(bsz, seq, A_HEADS * LANES)
    out_width = HEADS_PER_STEP * HEAD_DIM
    qk_spec = pl.BlockSpec((None, seq, HEADS_PER_STEP * LANES), lambda b, p: (b, 0, p))
    out = pl.pallas_call(
        _fox_kernel,
        out_shape=jax.ShapeDtypeStruct((bsz, seq, MIX_W), BF16),
        grid=(bsz, A_HEADS // HEADS_PER_STEP),
        in_specs=[qk_spec, qk_spec,
                  pl.BlockSpec((None, n_tiles, out_width, tq), lambda b, p: (b, 0, p, 0))],
        out_specs=pl.BlockSpec((None, seq, out_width), lambda b, p: (b, 0, p)),
        compiler_params=_params(2),
        name="fox_attention",
    )(q3, k3, vt)
    return out.reshape(bsz * seq, MIX_W)


def _diff_kernel(q_ref, k_ref, vt_ref, lam_ref, sg_ref, o_ref, *, lambda_init):
    tq = ATT_TILE
    tiles = list(range(q_ref.shape[0] // tq))
    lo = _lane_lo((tq, LANES))
    lp = lam_ref[...]
    lam = (jnp.exp(jnp.sum(lp[0:1] * lp[1:2], axis=1, keepdims=True))
           - jnp.exp(jnp.sum(lp[2:3] * lp[3:4], axis=1, keepdims=True)) + lambda_init)

    def half_of(t, sub):
        q = q_ref[_tile_rows(t), sub // 2 * LANES:(sub // 2 + 1) * LANES]
        zero = jnp.zeros_like(q)
        return jnp.where(lo, q, zero) if sub % 2 == 0 else jnp.where(lo, zero, q)

    def key_block(g, sub):
        return k_ref[_key_rows(g), sub // 2 * LANES:(sub // 2 + 1) * LANES]

    def value_t(g, sub):
        return vt_ref[g, sub // 2 * LANES:(sub // 2 + 1) * LANES, :]

    queries = [[half_of(t, sub) for sub in range(HEADS_PER_STEP)] for t in tiles]
    chunk_causal = (lax.broadcasted_iota(jnp.int32, (tq, tq), 0) // CHUNK
                    <= lax.broadcasted_iota(jnp.int32, (tq, tq), 1) // CHUNK)

    def finish(t, accs):
        for hd in range(HEADS_PER_STEP // 2):
            o1, o2 = (_normalise(acc, LANES) for acc in accs[2 * hd:2 * hd + 2])
            o = (o1 - lam * o2).T
            o_ref[_tile_rows(t), hd * LANES:(hd + 1) * LANES] = (
                _rms(o) * sg_ref[...] * (1.0 - lambda_init)).astype(o_ref.dtype)

    _causal_attention(tiles, HEADS_PER_STEP, queries, key_block, value_t, chunk_causal, LANES,
                      finish)


def _diff_attention(q, k, vt, lam_params, subln_gain, lambda_init, bsz, seq):
    tq = ATT_TILE
    n_tiles = seq // tq
    q3, k3 = (a.reshape(bsz, seq, MIX_W) for a in (q, k))
    width = HEADS_PER_STEP // 2 * LANES
    seq_spec = pl.BlockSpec((None, seq, width), lambda b, h: (b, 0, h))
    out = pl.pallas_call(
        functools.partial(_diff_kernel, lambda_init=lambda_init),
        out_shape=jax.ShapeDtypeStruct((bsz, seq, MIX_W), BF16),
        grid=(bsz, MIX_W // width),
        in_specs=[seq_spec, seq_spec,
                  pl.BlockSpec((None, n_tiles, width, tq), lambda b, h: (b, 0, h, 0)),
                  _const_spec((4, HEAD_DIM)), _const_spec((1, LANES))],
        out_specs=seq_spec,
        compiler_params=_params(2),
        name="diff_attention",
    )(q3, k3, vt, lam_params, subln_gain.reshape(1, LANES))
    return out.reshape(bsz * seq, MIX_W)


def _odd_prep_kernel(x_ref, g_ref, wc_ref, wd_ref, cw_ref, gains_ref, bd_ref,
                     c_ref, dq_ref, dk_ref, dvt_ref, carry_ref):
    tm = x_ref.shape[0]

    @pl.when(pl.program_id(1) == 0)
    def _():
        carry_ref[...] = jnp.zeros_like(carry_ref)

    h = (_rms(x_ref[...]) * g_ref[...]).astype(BF16)
    bd = bd_ref[...]
    gains = gains_ref[...]

    def project(w_ref, part):
        return _dot(h, w_ref[:, part * MIX_W:(part + 1) * MIX_W])

    d_q = project(wd_ref, 0)
    d_k = project(wd_ref, 1)
    d_q_ss = _head_meansq(d_q, bd)
    c_c = project(wc_ref, 1)
    d_k_ss = _head_meansq(d_k, bd)
    c_h = project(wc_ref, 2)
    c_b = project(wc_ref, 0)
    d_v = project(wd_ref, 2)

    u = c_c * c_h
    prev = carry_ref[...]
    carry_ref[...] = u[tm - 8:, :]
    row = lax.broadcasted_iota(jnp.int32, u.shape, 0)
    u1 = jnp.where(row == 0, prev[7:8], pltpu.roll(u, 1, axis=0))
    u2 = jnp.where(row == 0, prev[6:7], jnp.where(row == 1, prev[7:8], pltpu.roll(u, 2, axis=0)))
    cw = cw_ref[...]
    conv = cw[0:1] * u2 + cw[1:2] * u1 + cw[2:3] * u
    c_ref[...] = (c_b * conv).astype(BF16)

    dq_ref[...] = _head_norm(d_q, d_q_ss, gains[0:1]).astype(BF16)
    dk_ref[...] = _head_norm(d_k, d_k_ss, gains[1:2]).astype(BF16)
    _store_transposed(dvt_ref, d_v)


def _odd_prep(x, gain, w_in, conv_w, qk_gains, bsz, seq):
    t, d = x.shape
    tm = PREP_TILE
    w3 = 3 * MIX_W
    n_s = seq // tm
    tok = lambda w: pl.BlockSpec((tm, w), lambda b, s: (b * n_s + s, 0))
    stream = jax.ShapeDtypeStruct((t, MIX_W), BF16)
    return pl.pallas_call(
        _odd_prep_kernel,
        out_shape=[stream] * 3 + [jax.ShapeDtypeStruct((bsz, seq // ATT_TILE, MIX_W, ATT_TILE), BF16)],
        grid=(bsz, n_s),
        in_specs=[tok(d), _const_spec((1, d)), _const_spec((d, w3)), _const_spec((d, w3)),
                  _const_spec((3, MIX_W)), _const_spec((2, MIX_W)), _const_spec((BD_WIDTH, BD_WIDTH))],
        out_specs=[tok(MIX_W)] * 3 + [pl.BlockSpec((None, tm // ATT_TILE, MIX_W, ATT_TILE),
                                                   lambda b, s: (b, s, 0, 0))],
        scratch_shapes=[pltpu.VMEM((8, MIX_W), F32)],
        compiler_params=_params(2),
        name="odd_prep",
    )(x, gain.reshape(1, d), w_in[:, :w3].astype(BF16), w_in[:, w3:].astype(BF16), conv_w,
      _qk_gain_rows(qk_gains), _block_diag_mean())


BAND_GROUPS = 1 + D_LEFT // ATT_TILE


MAX_BAND_TILES = 8
BAND_WIDTH = BAND_GROUPS * ATT_TILE
BIAS_ROW = BAND_WIDTH + ATT_TILE


def _band_kernel(q_ref, k_ref, vt_ref, w_ref, o_ref, bias_ref):
    tq = ATT_TILE
    band_tiles = q_ref.shape[0] // tq
    j = pl.program_id(2)

    @pl.when((pl.program_id(1) == 0) & (j == 0))
    def _():
        key = lax.broadcasted_iota(jnp.int32, (BAND_WIDTH, tq), 0)
        chunk_start = lax.broadcasted_iota(jnp.int32, (BAND_WIDTH, tq), 1) // CHUNK * CHUNK
        in_band = (key >= chunk_start) & (key < chunk_start + D_LEFT + CHUNK)
        for sub in range(2):
            rows = jnp.broadcast_to(w_ref[sub], (BAND_WIDTH, BIAS_ROW))
            skew = pltpu.roll(rows, tq + 1, axis=1, stride=1, stride_axis=0)
            table = jnp.where(in_band, skew[:, :tq] * LOG2E, NEG)
            for grp in range(BAND_GROUPS):
                bias_ref[sub, grp] = table[grp * tq:(grp + 1) * tq, :]

    lo = _lane_lo((tq, LANES))
    before_start = jnp.where(j == 0, NEG, 0.0)
    q_subs, items = {}, []
    for tile in range(band_tiles):
        q = q_ref[tile * tq:(tile + 1) * tq, :]
        zero = jnp.zeros_like(q)
        q_subs[tile] = (jnp.where(lo, q, zero), jnp.where(lo, zero, q))
        items += [(tile, grp, sub) for grp in range(BAND_GROUPS) for sub in range(2)]

    def may_precede(tile, grp):
        return tile + grp < BAND_GROUPS - 1

    def group_index(tile, grp):
        g = band_tiles * j + tile + grp - (BAND_GROUPS - 1)
        return jnp.maximum(g, 0) if may_precede(tile, grp) else g

    def scores(tile, grp, sub):
        s = _dot_nt(k_ref[_key_rows(group_index(tile, grp)), :], q_subs[tile][sub]) + bias_ref[sub, grp]
        return s + before_start if may_precede(tile, grp) else s

    maxes, accs = {}, {}

    def exponentials(tile, grp, sub, s):
        mx = jnp.max(s, axis=0, keepdims=True)
        m_old = maxes.get((tile, sub))
        m_new = mx if m_old is None else jnp.maximum(m_old, mx)
        maxes[tile, sub] = m_new
        return jnp.exp2(s - m_new).astype(BF16), None if m_old is None else jnp.exp2(m_old - m_new)

    def accumulate(tile, grp, sub, p, alpha):
        value_t = _with_sum_rows(vt_ref[group_index(tile, grp), sub * HEAD_DIM:(sub + 1) * HEAD_DIM, :])
        acc_n = _dot(value_t, p)
        accs[tile, sub] = acc_n if alpha is None else alpha * accs[tile, sub] + acc_n
        if grp == BAND_GROUPS - 1 and sub == 1:
            o_t = jnp.concatenate([_normalise(accs[tile, s], HEAD_DIM) for s in range(2)], axis=0)
            o_ref[tile * tq:(tile + 1) * tq, :] = o_t.T.astype(o_ref.dtype)

    in_flight = []
    for item in items:
        in_flight.append((item, exponentials(*item, scores(*item))))
        if len(in_flight) > BAND_LOOKAHEAD:
            done, (p, alpha) = in_flight.pop(0)
            accumulate(*done, p, alpha)
    for done, (p, alpha) in in_flight:
        accumulate(*done, p, alpha)


def _band_bias_rows(rel_table):
    n_lo = ATT_TILE - 1 - REL_CLIP
    n_hi = BIAS_ROW - n_lo - (2 * REL_CLIP + 1)
    w = jnp.concatenate([jnp.repeat(rel_table[:, :1], n_lo, axis=1), rel_table,
                         jnp.repeat(rel_table[:, -1:], n_hi, axis=1)], axis=1)
    return w.astype(F32).reshape(rel_table.shape[0], 1, BIAS_ROW)


def _band_attention(q, k, vt, rel_table, bsz, seq):
    tq = ATT_TILE
    n_q = seq // tq
    band_tiles = min(MAX_BAND_TILES, n_q)
    q3, k3 = (a.reshape(bsz, seq, MIX_W) for a in (q, k))
    q_spec = pl.BlockSpec((None, band_tiles * tq, LANES), lambda p, b, j: (b, j, p))
    out = pl.pallas_call(
        _band_kernel,
        out_shape=jax.ShapeDtypeStruct((bsz, seq, MIX_W), BF16),
        grid=(MIX_W // LANES, bsz, n_q // band_tiles),
        in_specs=[q_spec,
                  pl.BlockSpec((None, seq, LANES), lambda p, b, j: (b, 0, p)),
                  pl.BlockSpec((None, n_q, LANES, tq), lambda p, b, j: (b, 0, p, 0)),
                  pl.BlockSpec((2, 1, BIAS_ROW), lambda p, b, j: (p, 0, 0))],
        out_specs=q_spec,
        scratch_shapes=[pltpu.VMEM((2, BAND_GROUPS, tq, tq), F32)],
        compiler_params=_params(3),
        name="band_attention",
    )(q3, k3, vt, _band_bias_rows(rel_table))
    return out.reshape(bsz * seq, MIX_W)


def _mem_kv_kernel(mem_ref, g_ref, w_ref, kg_ref, k_ref, v_ref):
    mem_n = (_rms(mem_ref[...]) * g_ref[...]).astype(BF16)
    kv = _dot(mem_n, w_ref[...])
    kg = kg_ref[...]
    for hd in range(X_HEADS):
        sl = slice(hd * X_HEAD_DIM, (hd + 1) * X_HEAD_DIM)
        k_ref[:, sl] = (_rms(kv[:, sl]) * kg).astype(BF16)
    v_ref[...] = kv[:, D_MODEL:].astype(BF16)


def _mem_kv(mem, gain, w_kv, k_gain):
    bsz, n_mem, d = mem.shape
    blk = pl.BlockSpec((None, n_mem, d), lambda b: (b, 0, 0))
    out = jax.ShapeDtypeStruct((bsz, n_mem, d), BF16)
    return pl.pallas_call(
        _mem_kv_kernel,
        out_shape=[out, out],
        grid=(bsz,),
        in_specs=[blk, _const_spec((1, d)), _const_spec((d, 2 * d)), _const_spec((1, X_HEAD_DIM))],
        out_specs=[blk, blk],
        compiler_params=_params(1),
        name="mem_kv",
    )(mem, gain.reshape(1, d), w_kv.astype(BF16), k_gain.reshape(1, X_HEAD_DIM))


def _cross_kernel(x_ref, left_ref, right_ref, wl_ref, wr_ref, g_ref, wq_ref, qg_ref, k_ref, v_ref,
                  wo_ref, o_ref):
    x = x_ref[...] + _dot(left_ref[...], wl_ref[...]) + _dot(right_ref[...], wr_ref[...])
    h = (_rms(x) * g_ref[...]).astype(BF16)
    q = _dot(h, wq_ref[...])
    qg = qg_ref[...]
    head_cols = [slice(hd * X_HEAD_DIM, (hd + 1) * X_HEAD_DIM) for hd in range(X_HEADS)]

    def probabilities(sl):
        qh = (_rms(q[:, sl]) * qg * X_HEAD_DIM ** -0.5).astype(BF16)
        s = _dot_nt(qh, k_ref[:, sl])
        p = jnp.exp(s - jnp.max(s, axis=-1, keepdims=True))
        return p.astype(BF16), jnp.sum(p, axis=-1, keepdims=True)

    heads = []
    ahead = probabilities(head_cols[0])
    for hd, sl in enumerate(head_cols):
        p, l = ahead
        if hd + 1 < X_HEADS:
            ahead = probabilities(head_cols[hd + 1])
        heads.append((_dot(p, v_ref[:, sl]) / l).astype(BF16))
    o_ref[...] = x + _dot(jnp.concatenate(heads, axis=1), wo_ref[...])


CROSS_TILE = 1024


def _mix_out_cross_attention(x, left, right, w_mix_out, gain, w_q, q_gain, k, v, w_o, bsz, seq):
    t, d = x.shape
    tq = CROSS_TILE
    n_s = seq // tq
    n_mem = k.shape[1]
    tok = lambda width: pl.BlockSpec((tq, width), lambda b, s: (b * n_s + s, 0))
    mem_spec = pl.BlockSpec((None, n_mem, d), lambda b, s: (b, 0, 0))
    w_mix = w_mix_out.astype(BF16)
    return pl.pallas_call(
        _cross_kernel,
        out_shape=jax.ShapeDtypeStruct((t, d), F32),
        grid=(bsz, n_s),
        in_specs=[tok(d), tok(MIX_W), tok(MIX_W), _const_spec((MIX_W, d)), _const_spec((MIX_W, d)),
                  _const_spec((1, d)), _const_spec((d, d)), _const_spec((1, X_HEAD_DIM)),
                  mem_spec, mem_spec, _const_spec((d, d))],
        out_specs=tok(d),
        compiler_params=_params(2),
        name="cross_attention",
    )(x, left, right, w_mix[:MIX_W], w_mix[MIX_W:], gain.reshape(1, d), w_q.astype(BF16),
      q_gain.reshape(1, X_HEAD_DIM), k, v, w_o.astype(BF16))


def kernel(x, mem, positions, ln_gains, ffn1_w_in, ffn1_w_out, ffn2_w_in, ffn2_w_out, even_w_in, even_f_bias, even_qk_gains, even_lambda, even_subln_gain, even_w_out, odd_w_in, odd_conv_w, odd_qk_gains, odd_rel_bias, odd_w_out, x_w_q, x_w_kv, x_qk_gains, x_w_o):
    bsz, seq, d = x.shape
    depth = ln_gains.shape[0]
    assert d == D_MODEL and ffn1_w_in.shape[-1] == 2 * D_FF and mem.shape[-1] == D_MODEL
    assert even_w_in.shape[-1] == 6 * MIX_W + A_HEADS and odd_w_in.shape[-1] == 6 * MIX_W
    assert seq % ATT_TILE == 0 and seq % PREP_TILE == 0 and seq % CROSS_TILE == 0
    assert (bsz * seq) % (FFN_TILES_PER_STEP * TOKEN_TILE) == 0
    rope = _rope_tables(positions)
    x = x.reshape(bsz * seq, d)
    (ffn1_w_in, ffn1_w_out, ffn2_w_in, ffn2_w_out, even_w_in, even_w_out, odd_w_in, odd_w_out,
     x_w_q, x_w_kv, x_w_o) = (
        w.astype(BF16) for w in (ffn1_w_in, ffn1_w_out, ffn2_w_in, ffn2_w_out, even_w_in,
                                 even_w_out, odd_w_in, odd_w_out, x_w_q, x_w_kv, x_w_o))
    for layer in range(depth):
        g = ln_gains[layer]
        x = _ffn(x, g[0], ffn1_w_in[layer], ffn1_w_out[layer])
        if layer % 2 == 0:
            e = layer // 2
            lambda_init = 0.8 - 0.6 * math.exp(-0.3 * layer)
            aq, ak, avt, bq, bk, bvt = _even_prep(
                x, g[1], even_w_in[e], even_f_bias[e], even_qk_gains[e], rope, bsz, seq)
            left = _fox_attention(aq, ak, avt, bsz, seq)
            right = _diff_attention(bq, bk, bvt, even_lambda[e], even_subln_gain[e], lambda_init,
                                    bsz, seq)
            w_mix_out = even_w_out[e]
        else:
            o = layer // 2
            left, dq, dk, dvt = _odd_prep(x, g[1], odd_w_in[o], odd_conv_w[o], odd_qk_gains[o],
                                          bsz, seq)
            right = _band_attention(dq, dk, dvt, odd_rel_bias[o], bsz, seq)
            w_mix_out = odd_w_out[o]
        mk, mv = _mem_kv(mem, g[3], x_w_kv[layer], x_qk_gains[layer, 1])
        x = _mix_out_cross_attention(x, left, right, w_mix_out, g[2], x_w_q[layer],
                                     x_qk_gains[layer, 0], mk, mv, x_w_o[layer], bsz, seq)
        x = _ffn(x, g[4], ffn2_w_in[layer], ffn2_w_out[layer])
    return x.reshape(bsz, seq, d)
```

```python
import functools
import math

import jax
import jax.numpy as jnp
from jax import lax
from jax.experimental import pallas as pl
from jax.experimental.pallas import tpu as pltpu

D_MODEL = 1024
CHUNK = 64
HEAD_DIM = 64
ROPE_DIM = HEAD_DIM // 4
ROPE_THETA = 500000.0
RMS_EPS = 1e-6
D_FF = 2816
A_HEADS = 8
MIX_W = 512
LANES = 128
D_LEFT = 8 * CHUNK
REL_CLIP = 128
X_HEADS = 4
X_HEAD_DIM = D_MODEL // X_HEADS
NEG = -1e30
LOG2E = math.log2(math.e)
Q_SCALE = HEAD_DIM ** -0.5 * LOG2E

VMEM_LIMIT = 56 * 1024 * 1024
FFN_CHUNKS = (768, 768, 768, 512)
TOKEN_TILE = 256
PREP_TILE = 1024
ATT_TILE = 256

BF16 = jnp.bfloat16
F32 = jnp.float32


def _params(n_axes):
    return pltpu.CompilerParams(dimension_semantics=("arbitrary",) * n_axes,
                                vmem_limit_bytes=VMEM_LIMIT)


def _const_spec(shape):
    nd = len(shape)
    return pl.BlockSpec(shape, lambda *_: (0,) * nd, pipeline_mode=pl.Buffered(1))


def _rms(x):
    return x * lax.rsqrt(jnp.mean(x * x, axis=-1, keepdims=True) + RMS_EPS)


def _dot(a, b):
    return jnp.dot(a, b, preferred_element_type=F32)


def _dot_nt(a, b):
    return lax.dot_general(a, b, (((1,), (1,)), ((), ())), preferred_element_type=F32)


def _ffn_kernel(x_ref, g_ref, wi_ref, wo_ref, o_ref):
    n_tiles = x_ref.shape[0] // TOKEN_TILE
    bounds = [sum(FFN_CHUNKS[:c]) for c in range(len(FFN_CHUNKS) + 1)]
    chunks = list(zip(bounds[:-1], bounds[1:]))

    def rows(i):
        return slice(i * TOKEN_TILE, (i + 1) * TOKEN_TILE)

    def normed(i):
        x = x_ref[rows(i), :]
        return x, (_rms(x) * g_ref[...]).astype(BF16)

    def chunk_out(xn, cols):
        lo, hi = cols
        gate = _dot(xn, wi_ref[:, lo:hi])
        up = _dot(xn, wi_ref[:, D_FF + lo:D_FF + hi])
        act = (gate * (1.0 / (1.0 + jnp.exp(-gate))) * up).astype(BF16)
        return _dot(act, wo_ref[lo:hi, :])

    x, xn = normed(0)
    previous = None
    for i in range(n_tiles):
        y = chunk_out(xn, chunks[0])
        if previous is not None:
            o_ref[rows(i - 1), :] = previous[0] + 0.5 * previous[1]
        for c, cols in enumerate(chunks[1:], start=1):
            y = y + chunk_out(xn, cols)
            if c == 1 and i + 1 < n_tiles:
                following = normed(i + 1)
        previous = (x, y)
        if i + 1 < n_tiles:
            x, xn = following
    o_ref[rows(n_tiles - 1), :] = previous[0] + 0.5 * previous[1]


FFN_TILES_PER_STEP = 4


def _ffn(x, gain, w_in, w_out):
    t, d = x.shape
    tm = TOKEN_TILE * FFN_TILES_PER_STEP
    return pl.pallas_call(
        _ffn_kernel,
        out_shape=jax.ShapeDtypeStruct((t, d), F32),
        grid=(t // tm,),
        in_specs=[pl.BlockSpec((tm, d), lambda i: (i, 0)),
                  _const_spec((1, d)), _const_spec((d, 2 * D_FF)), _const_spec((D_FF, d))],
        out_specs=pl.BlockSpec((tm, d), lambda i: (i, 0)),
        compiler_params=_params(1),
        name="ffn",
    )(x, gain.reshape(1, d), w_in.astype(BF16), w_out.astype(BF16))


BD_WIDTH = 256


def _head_meansq(y, mean_bd):
    sq = (y * y).astype(BF16)
    return jnp.concatenate([_dot(sq[:, c:c + BD_WIDTH], mean_bd) for c in range(0, MIX_W, BD_WIDTH)],
                           axis=1)


def _head_norm(y, meansq, gain):
    return y * lax.rsqrt(meansq + RMS_EPS) * gain


def _block_diag_mean():
    r = jnp.arange(BD_WIDTH) // HEAD_DIM
    return jnp.where(r[:, None] == r[None, :], 1.0 / HEAD_DIM, 0.0).astype(BF16)


def _tile_heads(v):
    return jnp.tile(v, (1, MIX_W // HEAD_DIM))


def _qk_gain_rows(qk_gains):
    scale = jnp.where(jnp.arange(qk_gains.shape[0]) % 2 == 0, Q_SCALE, 1.0).astype(F32)
    return _tile_heads(qk_gains * scale[:, None])


def _store_transposed(vt_ref, v):
    for i in range(v.shape[0] // ATT_TILE):
        vt_ref[i] = v[i * ATT_TILE:(i + 1) * ATT_TILE, :].T.astype(BF16)


def _split3(f):
    hi = f.astype(BF16).astype(F32)
    rest = f - hi
    mid = rest.astype(BF16).astype(F32)
    return hi, mid, rest - mid


def _even_prep_kernel(x_ref, g_ref, wa_ref, wf_ref, fb_ref, wb_ref, gains_ref, bd_ref,
                      ra_ref, rm_ref, rp_ref,
                      aq_ref, ak_ref, avt_ref, bq_ref, bk_ref, bvt_ref, carry_ref):
    tm = x_ref.shape[0]

    @pl.when(pl.program_id(1) == 0)
    def _():
        carry_ref[...] = jnp.zeros_like(carry_ref)

    h = (_rms(x_ref[...]) * g_ref[...]).astype(BF16)
    bd = bd_ref[...]
    gains = gains_ref[...]

    def project(w_ref, part):
        return _dot(h, w_ref[:, part * MIX_W:(part + 1) * MIX_W])

    z = _dot(h, wf_ref[...]) + fb_ref[...]
    a_q = project(wa_ref, 0)

    logf = jnp.minimum(z, 0.0) - jnp.log1p(jnp.exp(-jnp.abs(z)))
    row = lax.broadcasted_iota(jnp.int32, logf.shape, 0)
    step = 1
    while step < tm:
        logf = logf + jnp.where(row >= step, pltpu.roll(logf, step, axis=0), 0.0)
        step *= 2
    cum = logf + carry_ref[...]
    carry_ref[...] = cum[tm - 1:tm, :]
    hi, mid, lo = _split3(cum * LOG2E)

    a_k = project(wa_ref, 1)
    a_q_ss = _head_meansq(a_q, bd)
    b_q = project(wb_ref, 0)
    qn = _head_norm(a_q, a_q_ss, gains[0:1])
    a_k_ss = _head_meansq(a_k, bd)
    b_k = project(wb_ref, 1)
    kn = _head_norm(a_k, a_k_ss, gains[1:2])

    lane = lax.broadcasted_iota(jnp.int32, (tm, LANES), 1)
    is_hi = (lane == HEAD_DIM) | (lane == HEAD_DIM + 3)
    is_mid = (lane == HEAD_DIM + 1) | (lane == HEAD_DIM + 4)
    ones_hi = jnp.where(lane < HEAD_DIM + 6, 1.0, 0.0)
    for hd in range(A_HEADS):
        if hd == A_HEADS // 2:
            b_q_ss = _head_meansq(b_q, bd)
            a_v = project(wa_ref, 2)
        pieces = jnp.where(is_hi, hi[:, hd:hd + 1],
                           jnp.where(is_mid, mid[:, hd:hd + 1], lo[:, hd:hd + 1]))
        blk = slice((hd // 2) * LANES, (hd // 2 + 1) * LANES)
        sq, sk = qn[:, blk], kn[:, blk]
        if hd % 2:
            sq, sk = pltpu.roll(sq, HEAD_DIM, axis=1), pltpu.roll(sk, HEAD_DIM, axis=1)
        qa = jnp.where(lane < HEAD_DIM, sq, jnp.where(lane < HEAD_DIM + 3, pieces, ones_hi))
        ka = jnp.where(lane < HEAD_DIM, sk, jnp.where(lane < HEAD_DIM + 3, 1.0,
                                                      jnp.where(lane < HEAD_DIM + 6, -pieces, 0.0)))
        aq_ref[:, hd * LANES:(hd + 1) * LANES] = qa.astype(BF16)
        ak_ref[:, hd * LANES:(hd + 1) * LANES] = ka.astype(BF16)

    rep = MIX_W // LANES
    ra = jnp.tile(ra_ref[...], (1, rep))
    rm = jnp.tile(rm_ref[...], (1, rep))
    rp = jnp.tile(rp_ref[...], (1, rep))

    def rope(v):
        half = ROPE_DIM // 2
        return v * ra + pltpu.roll(v, MIX_W - half, axis=1) * rm + pltpu.roll(v, half, axis=1) * rp

    bq_ref[...] = rope(_head_norm(b_q, b_q_ss, gains[2:3])).astype(BF16)
    b_k_ss = _head_meansq(b_k, bd)
    _store_transposed(avt_ref, a_v)
    b_v = project(wb_ref, 2)
    bk_ref[...] = rope(_head_norm(b_k, b_k_ss, gains[3:4])).astype(BF16)
    _store_transposed(bvt_ref, b_v)


def _rope_tables(positions):
    half = ROPE_DIM // 2
    inv = ROPE_THETA ** (-jnp.arange(0, ROPE_DIM, 2, dtype=F32) / ROPE_DIM)
    ang = positions.astype(F32).reshape(-1, 1) * inv
    cos = jnp.tile(jnp.cos(ang), (1, LANES // half))
    sin = jnp.tile(jnp.sin(ang), (1, LANES // half))
    in_head = jnp.arange(LANES) % HEAD_DIM
    ra = jnp.where(in_head < ROPE_DIM, cos, 1.0)
    rm = jnp.where(in_head < half, -sin, 0.0)
    rp = jnp.where((in_head >= half) & (in_head < ROPE_DIM), sin, 0.0)
    return ra, rm, rp


def _even_prep(x, gain, w_in, f_bias, qk_gains, rope, bsz, seq):
    t, d = x.shape
    tm = PREP_TILE
    a_w = 3 * MIX_W
    wa = w_in[:, :a_w].astype(BF16)
    wf = jnp.pad(w_in[:, a_w:a_w + A_HEADS], ((0, 0), (0, LANES - A_HEADS))).astype(BF16)
    wb = w_in[:, a_w + A_HEADS:].astype(BF16)
    fb = jnp.pad(f_bias, (0, LANES - A_HEADS)).reshape(1, LANES)
    n_s = seq // tm
    tok = lambda w: pl.BlockSpec((tm, w), lambda b, s: (b * n_s + s, 0))
    vt_spec = pl.BlockSpec((None, tm // ATT_TILE, MIX_W, ATT_TILE), lambda b, s: (b, s, 0, 0))
    stream = lambda w: jax.ShapeDtypeStruct((t, w), BF16)
    vt_shape = jax.ShapeDtypeStruct((bsz, seq // ATT_TILE, MIX_W, ATT_TILE), BF16)
    return pl.pallas_call(
        _even_prep_kernel,
        out_shape=[stream(A_HEADS * LANES), stream(A_HEADS * LANES), vt_shape,
                   stream(MIX_W), stream(MIX_W), vt_shape],
        grid=(bsz, n_s),
        in_specs=[tok(d), _const_spec((1, d)), _const_spec((d, a_w)), _const_spec((d, LANES)),
                  _const_spec((1, LANES)), _const_spec((d, a_w)), _const_spec((4, MIX_W)),
                  _const_spec((BD_WIDTH, BD_WIDTH)), tok(LANES), tok(LANES), tok(LANES)],
        out_specs=[tok(A_HEADS * LANES), tok(A_HEADS * LANES), vt_spec,
                   tok(MIX_W), tok(MIX_W), vt_spec],
        scratch_shapes=[pltpu.VMEM((1, LANES), F32)],
        compiler_params=_params(2),
        name="even_prep",
    )(x, gain.reshape(1, d), wa, wf, fb, wb, _qk_gain_rows(qk_gains), _block_diag_mean(), *rope)


def _lane_lo(shape):
    return lax.broadcasted_iota(jnp.int32, shape, len(shape) - 1) < HEAD_DIM


def _key_rows(g):
    if isinstance(g, int):
        return slice(g * ATT_TILE, (g + 1) * ATT_TILE)
    return pl.ds(pl.multiple_of(g * ATT_TILE, ATT_TILE), ATT_TILE)


SUM_ROWS = 16
MXU_LOOKAHEAD = 5
BAND_LOOKAHEAD = 6


def _with_sum_rows(value_t):
    return jnp.concatenate([value_t, jnp.ones((SUM_ROWS, value_t.shape[1]), BF16)], axis=0)


def _normalise(acc, v_rows):
    return acc[:v_rows] / acc[v_rows:v_rows + 1]


def _causal_attention(tiles, n_sub, queries, key_block, value_t, diag_visible, v_rows, finish):
    tq = ATT_TILE
    items = [(i, g, sub) for i, tile in enumerate(tiles) for g in range(tile + 1)
             for sub in range(n_sub)]
    maxes = [[jnp.full((1, tq), NEG, F32)] * n_sub for _ in tiles]
    accs = [[jnp.zeros((v_rows + SUM_ROWS, tq), F32)] * n_sub for _ in tiles]

    def scores(i, g, sub):
        s = _dot_nt(key_block(g, sub), queries[i][sub])
        return jnp.where(diag_visible, s, NEG) if g == tiles[i] else s

    def exponentials(i, g, sub, s):
        m_old = maxes[i][sub]
        m_new = jnp.maximum(m_old, jnp.max(s, axis=0, keepdims=True))
        maxes[i][sub] = m_new
        return jnp.exp2(s - m_new).astype(BF16), jnp.exp2(m_old - m_new)

    def accumulate(i, g, sub, p, alpha):
        accs[i][sub] = alpha * accs[i][sub] + _dot(_with_sum_rows(value_t(g, sub)), p)
        if g == tiles[i] and sub == n_sub - 1:
            finish(i, accs[i])

    in_flight = []
    for item in items:
        in_flight.append((item, exponentials(*item, scores(*item))))
        if len(in_flight) > MXU_LOOKAHEAD:
            done, (p, alpha) = in_flight.pop(0)
            accumulate(*done, p, alpha)
    for done, (p, alpha) in in_flight:
        accumulate(*done, p, alpha)


HEADS_PER_STEP = 4


def _tile_rows(i):
    return slice(i * ATT_TILE, (i + 1) * ATT_TILE)


def _fox_kernel(q_ref, k_ref, vt_ref, o_ref):
    tq = ATT_TILE
    tiles = list(range(q_ref.shape[0] // tq))

    def key_block(g, sub):
        return k_ref[_key_rows(g), sub * LANES:(sub + 1) * LANES]

    def value_t(g, sub):
        return vt_ref[g, sub * HEAD_DIM:(sub + 1) * HEAD_DIM, :]

    queries = [[q_ref[_tile_rows(t), sub * LANES:(sub + 1) * LANES] for sub in range(HEADS_PER_STEP)]
               for t in tiles]
    causal = (lax.broadcasted_iota(jnp.int32, (tq, tq), 0)
              <= lax.broadcasted_iota(jnp.int32, (tq, tq), 1))

    def finish(t, accs):
        o_t = jnp.concatenate([_normalise(acc, HEAD_DIM) for acc in accs], axis=0)
        o_ref[_tile_rows(t), :] = o_t.T.astype(o_ref.dtype)

    _causal_attention(tiles, HEADS_PER_STEP, queries, key_block, value_t, causal, HEAD_DIM, finish)


def _fox_attention(q, k, vt, bsz, seq):
    tq = ATT_TILE
    n_tiles = seq // tq
    q3 = q.reshape(bsz, seq, A_HEADS * LANES)
    k3 = k.reshape(bsz, seq, A_HEADS * LANES)
    out_width = HEADS_PER_STEP * HEAD_DIM
    qk_spec = pl.BlockSpec((None, seq, HEADS_PER_STEP * LANES), lambda b, p: (b, 0, p))
    out = pl.pallas_call(
        _fox_kernel,
        out_shape=jax.ShapeDtypeStruct((bsz, seq, MIX_W), BF16),
        grid=(bsz, A_HEADS // HEADS_PER_STEP),
        in_specs=[qk_spec, qk_spec,
                  pl.BlockSpec((None, n_tiles, out_width, tq), lambda b, p: (b, 0, p, 0))],
        out_specs=pl.BlockSpec((None, seq, out_width), lambda b, p: (b, 0, p)),
        compiler_params=_params(2),
        name="fox_attention",
    )(q3, k3, vt)
    return out.reshape(bsz * seq, MIX_W)


def _diff_kernel(q_ref, k_ref, vt_ref, lam_ref, sg_ref, o_ref, *, lambda_init):
    tq = ATT_TILE
    tiles = list(range(q_ref.shape[0] // tq))
    lo = _lane_lo((tq, LANES))
    lp = lam_ref[...]
    lam = (jnp.exp(jnp.sum(lp[0:1] * lp[1:2], axis=1, keepdims=True))
           - jnp.exp(jnp.sum(lp[2:3] * lp[3:4], axis=1, keepdims=True)) + lambda_init)

    def half_of(t, sub):
        q = q_ref[_tile_rows(t), sub // 2 * LANES:(sub // 2 + 1) * LANES]
        zero = jnp.zeros_like(q)
        return jnp.where(lo, q, zero) if sub % 2 == 0 else jnp.where(lo, zero, q)

    def key_block(g, sub):
        return k_ref[_key_rows(g), sub // 2 * LANES:(sub // 2 + 1) * LANES]

    def value_t(g, sub):
        return vt_ref[g, sub // 2 * LANES:(sub // 2 + 1) * LANES, :]

    queries = [[half_of(t, sub) for sub in range(HEADS_PER_STEP)] for t in tiles]
    chunk_causal = (lax.broadcasted_iota(jnp.int32, (tq, tq), 0) // CHUNK
                    <= lax.broadcasted_iota(jnp.int32, (tq, tq), 1) // CHUNK)

    def finish(t, accs):
        for hd in range(HEADS_PER_STEP // 2):
            o1, o2 = (_normalise(acc, LANES) for acc in accs[2 * hd:2 * hd + 2])
            o = (o1 - lam * o2).T
            o_ref[_tile_rows(t), hd * LANES:(hd + 1) * LANES] = (
                _rms(o) * sg_ref[...] * (1.0 - lambda_init)).astype(o_ref.dtype)

    _causal_attention(tiles, HEADS_PER_STEP, queries, key_block, value_t, chunk_causal, LANES,
                      finish)


def _diff_attention(q, k, vt, lam_params, subln_gain, lambda_init, bsz, seq):
    tq = ATT_TILE
    n_tiles = seq // tq
    q3, k3 = (a.reshape(bsz, seq, MIX_W) for a in (q, k))
    width = HEADS_PER_STEP // 2 * LANES
    seq_spec = pl.BlockSpec((None, seq, width), lambda b, h: (b, 0, h))
    out = pl.pallas_call(
        functools.partial(_diff_kernel, lambda_init=lambda_init),
        out_shape=jax.ShapeDtypeStruct((bsz, seq, MIX_W), BF16),
        grid=(bsz, MIX_W // width),
        in_specs=[seq_spec, seq_spec,
                  pl.BlockSpec((None, n_tiles, width, tq), lambda b, h: (b, 0, h, 0)),
                  _const_spec((4, HEAD_DIM)), _const_spec((1, LANES))],
        out_specs=seq_spec,
        compiler_params=_params(2),
        name="diff_attention",
    )(q3, k3, vt, lam_params, subln_gain.reshape(1, LANES))
    return out.reshape(bsz * seq, MIX_W)


def _odd_prep_kernel(x_ref, g_ref, wc_ref, wd_ref, cw_ref, gains_ref, bd_ref,
                     c_ref, dq_ref, dk_ref, dvt_ref, carry_ref):
    tm = x_ref.shape[0]

    @pl.when(pl.program_id(1) == 0)
    def _():
        carry_ref[...] = jnp.zeros_like(carry_ref)

    h = (_rms(x_ref[...]) * g_ref[...]).astype(BF16)
    bd = bd_ref[...]
    gains = gains_ref[...]

    def project(w_ref, part):
        return _dot(h, w_ref[:, part * MIX_W:(part + 1) * MIX_W])

    d_q = project(wd_ref, 0)
    d_k = project(wd_ref, 1)
    d_q_ss = _head_meansq(d_q, bd)
    c_c = project(wc_ref, 1)
    d_k_ss = _head_meansq(d_k, bd)
    c_h = project(wc_ref, 2)
    c_b = project(wc_ref, 0)
    d_v = project(wd_ref, 2)

    u = c_c * c_h
    prev = carry_ref[...]
    carry_ref[...] = u[tm - 8:, :]
    row = lax.broadcasted_iota(jnp.int32, u.shape, 0)
    u1 = jnp.where(row == 0, prev[7:8], pltpu.roll(u, 1, axis=0))
    u2 = jnp.where(row == 0, prev[6:7], jnp.where(row == 1, prev[7:8], pltpu.roll(u, 2, axis=0)))
    cw = cw_ref[...]
    conv = cw[0:1] * u2 + cw[1:2] * u1 + cw[2:3] * u
    c_ref[...] = (c_b * conv).astype(BF16)

    dq_ref[...] = _head_norm(d_q, d_q_ss, gains[0:1]).astype(BF16)
    dk_ref[...] = _head_norm(d_k, d_k_ss, gains[1:2]).astype(BF16)
    _store_transposed(dvt_ref, d_v)


def _odd_prep(x, gain, w_in, conv_w, qk_gains, bsz, seq):
    t, d = x.shape
    tm = PREP_TILE
    w3 = 3 * MIX_W
    n_s = seq // tm
    tok = lambda w: pl.BlockSpec((tm, w), lambda b, s: (b * n_s + s, 0))
    stream = jax.ShapeDtypeStruct((t, MIX_W), BF16)
    return pl.pallas_call(
        _odd_prep_kernel,
        out_shape=[stream] * 3 + [jax.ShapeDtypeStruct((bsz, seq // ATT_TILE, MIX_W, ATT_TILE), BF16)],
        grid=(bsz, n_s),
        in_specs=[tok(d), _const_spec((1, d)), _const_spec((d, w3)), _const_spec((d, w3)),
                  _const_spec((3, MIX_W)), _const_spec((2, MIX_W)), _const_spec((BD_WIDTH, BD_WIDTH))],
        out_specs=[tok(MIX_W)] * 3 + [pl.BlockSpec((None, tm // ATT_TILE, MIX_W, ATT_TILE),
                                                   lambda b, s: (b, s, 0, 0))],
        scratch_shapes=[pltpu.VMEM((8, MIX_W), F32)],
        compiler_params=_params(2),
        name="odd_prep",
    )(x, gain.reshape(1, d), w_in[:, :w3].astype(BF16), w_in[:, w3:].astype(BF16), conv_w,
      _qk_gain_rows(qk_gains), _block_diag_mean())


BAND_GROUPS = 1 + D_LEFT // ATT_TILE


MAX_BAND_TILES = 8
BAND_WIDTH = BAND_GROUPS * ATT_TILE
BIAS_ROW = BAND_WIDTH + ATT_TILE


def _band_kernel(q_ref, k_ref, vt_ref, w_ref, o_ref, bias_ref):
    tq = ATT_TILE
    band_tiles = q_ref.shape[0] // tq
    j = pl.program_id(2)

    @pl.when((pl.program_id(1) == 0) & (j == 0))
    def _():
        key = lax.broadcasted_iota(jnp.int32, (BAND_WIDTH, tq), 0)
        chunk_start = lax.broadcasted_iota(jnp.int32, (BAND_WIDTH, tq), 1) // CHUNK * CHUNK
        in_band = (key >= chunk_start) & (key < chunk_start + D_LEFT + CHUNK)
        for sub in range(2):
            rows = jnp.broadcast_to(w_ref[sub], (BAND_WIDTH, BIAS_ROW))
            skew = pltpu.roll(rows, tq + 1, axis=1, stride=1, stride_axis=0)
            table = jnp.where(in_band, skew[:, :tq] * LOG2E, NEG)
            for grp in range(BAND_GROUPS):
                bias_ref[sub, grp] = table[grp * tq:(grp + 1) * tq, :]

    lo = _lane_lo((tq, LANES))
    before_start = jnp.where(j == 0, NEG, 0.0)
    q_subs, items = {}, []
    for tile in range(band_tiles):
        q = q_ref[tile * tq:(tile + 1) * tq, :]
        zero = jnp.zeros_like(q)
        q_subs[tile] = (jnp.where(lo, q, zero), jnp.where(lo, zero, q))
        items += [(tile, grp, sub) for grp in range(BAND_GROUPS) for sub in range(2)]

    def may_precede(tile, grp):
        return tile + grp < BAND_GROUPS - 1

    def group_index(tile, grp):
        g = band_tiles * j + tile + grp - (BAND_GROUPS - 1)
        return jnp.maximum(g, 0) if may_precede(tile, grp) else g

    def scores(tile, grp, sub):
        s = _dot_nt(k_ref[_key_rows(group_index(tile, grp)), :], q_subs[tile][sub]) + bias_ref[sub, grp]
        return s + before_start if may_precede(tile, grp) else s

    maxes, accs = {}, {}

    def exponentials(tile, grp, sub, s):
        mx = jnp.max(s, axis=0, keepdims=True)
        m_old = maxes.get((tile, sub))
        m_new = mx if m_old is None else jnp.maximum(m_old, mx)
        maxes[tile, sub] = m_new
        return jnp.exp2(s - m_new).astype(BF16), None if m_old is None else jnp.exp2(m_old - m_new)

    def accumulate(tile, grp, sub, p, alpha):
        value_t = _with_sum_rows(vt_ref[group_index(tile, grp), sub * HEAD_DIM:(sub + 1) * HEAD_DIM, :])
        acc_n = _dot(value_t, p)
        accs[tile, sub] = acc_n if alpha is None else alpha * accs[tile, sub] + acc_n
        if grp == BAND_GROUPS - 1 and sub == 1:
            o_t = jnp.concatenate([_normalise(accs[tile, s], HEAD_DIM) for s in range(2)], axis=0)
            o_ref[tile * tq:(tile + 1) * tq, :] = o_t.T.astype(o_ref.dtype)

    in_flight = []
    for item in items:
        in_flight.append((item, exponentials(*item, scores(*item))))
        if len(in_flight) > BAND_LOOKAHEAD:
            done, (p, alpha) = in_flight.pop(0)
            accumulate(*done, p, alpha)
    for done, (p, alpha) in in_flight:
        accumulate(*done, p, alpha)


def _band_bias_rows(rel_table):
    n_lo = ATT_TILE - 1 - REL_CLIP
    n_hi = BIAS_ROW - n_lo - (2 * REL_CLIP + 1)
    w = jnp.concatenate([jnp.repeat(rel_table[:, :1], n_lo, axis=1), rel_table,
                         jnp.repeat(rel_table[:, -1:], n_hi, axis=1)], axis=1)
    return w.astype(F32).reshape(rel_table.shape[0], 1, BIAS_ROW)


def _band_attention(q, k, vt, rel_table, bsz, seq):
    tq = ATT_TILE
    n_q = seq // tq
    band_tiles = min(MAX_BAND_TILES, n_q)
    q3, k3 = (a.reshape(bsz, seq, MIX_W) for a in (q, k))
    q_spec = pl.BlockSpec((None, band_tiles * tq, LANES), lambda p, b, j: (b, j, p))
    out = pl.pallas_call(
        _band_kernel,
        out_shape=jax.ShapeDtypeStruct((bsz, seq, MIX_W), BF16),
        grid=(MIX_W // LANES, bsz, n_q // band_tiles),
        in_specs=[q_spec,
                  pl.BlockSpec((None, seq, LANES), lambda p, b, j: (b, 0, p)),
                  pl.BlockSpec((None, n_q, LANES, tq), lambda p, b, j: (b, 0, p, 0)),
                  pl.BlockSpec((2, 1, BIAS_ROW), lambda p, b, j: (p, 0, 0))],
        out_specs=q_spec,
        scratch_shapes=[pltpu.VMEM((2, BAND_GROUPS, tq, tq), F32)],
        compiler_params=_params(3),
        name="band_attention",
    )(q3, k3, vt, _band_bias_rows(rel_table))
    return out.reshape(bsz * seq, MIX_W)


def _mem_kv_kernel(mem_ref, g_ref, w_ref, kg_ref, k_ref, v_ref):
    mem_n = (_rms(mem_ref[...]) * g_ref[...]).astype(BF16)
    kv = _dot(mem_n, w_ref[...])
    kg = kg_ref[...]
    for hd in range(X_HEADS):
        sl = slice(hd * X_HEAD_DIM, (hd + 1) * X_HEAD_DIM)
        k_ref[:, sl] = (_rms(kv[:, sl]) * kg).astype(BF16)
    v_ref[...] = kv[:, D_MODEL:].astype(BF16)


def _mem_kv(mem, gain, w_kv, k_gain):
    bsz, n_mem, d = mem.shape
    blk = pl.BlockSpec((None, n_mem, d), lambda b: (b, 0, 0))
    out = jax.ShapeDtypeStruct((bsz, n_mem, d), BF16)
    return pl.pallas_call(
        _mem_kv_kernel,
        out_shape=[out, out],
        grid=(bsz,),
        in_specs=[blk, _const_spec((1, d)), _const_spec((d, 2 * d)), _const_spec((1, X_HEAD_DIM))],
        out_specs=[blk, blk],
        compiler_params=_params(1),
        name="mem_kv",
    )(mem, gain.reshape(1, d), w_kv.astype(BF16), k_gain.reshape(1, X_HEAD_DIM))


def _cross_kernel(x_ref, left_ref, right_ref, wl_ref, wr_ref, g_ref, wq_ref, qg_ref, k_ref, v_ref,
                  wo_ref, o_ref):
    x = x_ref[...] + _dot(left_ref[...], wl_ref[...]) + _dot(right_ref[...], wr_ref[...])
    h = (_rms(x) * g_ref[...]).astype(BF16)
    q = _dot(h, wq_ref[...])
    qg = qg_ref[...]
    head_cols = [slice(hd * X_HEAD_DIM, (hd + 1) * X_HEAD_DIM) for hd in range(X_HEADS)]

    def probabilities(sl):
        qh = (_rms(q[:, sl]) * qg * X_HEAD_DIM ** -0.5).astype(BF16)
        s = _dot_nt(qh, k_ref[:, sl])
        p = jnp.exp(s - jnp.max(s, axis=-1, keepdims=True))
        return p.astype(BF16), jnp.sum(p, axis=-1, keepdims=True)

    heads = []
    ahead = probabilities(head_cols[0])
    for hd, sl in enumerate(head_cols):
        p, l = ahead
        if hd + 1 < X_HEADS:
            ahead = probabilities(head_cols[hd + 1])
        heads.append((_dot(p, v_ref[:, sl]) / l).astype(BF16))
    o_ref[...] = x + _dot(jnp.concatenate(heads, axis=1), wo_ref[...])


CROSS_TILE = 1024


def _mix_out_cross_attention(x, left, right, w_mix_out, gain, w_q, q_gain, k, v, w_o, bsz, seq):
    t, d = x.shape
    tq = CROSS_TILE
    n_s = seq // tq
    n_mem = k.shape[1]
    tok = lambda width: pl.BlockSpec((tq, width), lambda b, s: (b * n_s + s, 0))
    mem_spec = pl.BlockSpec((None, n_mem, d), lambda b, s: (b, 0, 0))
    w_mix = w_mix_out.astype(BF16)
    return pl.pallas_call(
        _cross_kernel,
        out_shape=jax.ShapeDtypeStruct((t, d), F32),
        grid=(bsz, n_s),
        in_specs=[tok(d), tok(MIX_W), tok(MIX_W), _const_spec((MIX_W, d)), _const_spec((MIX_W, d)),
                  _const_spec((1, d)), _const_spec((d, d)), _const_spec((1, X_HEAD_DIM)),
                  mem_spec, mem_spec, _const_spec((d, d))],
        out_specs=tok(d),
        compiler_params=_params(2),
        name="cross_attention",
    )(x, left, right, w_mix[:MIX_W], w_mix[MIX_W:], gain.reshape(1, d), w_q.astype(BF16),
      q_gain.reshape(1, X_HEAD_DIM), k, v, w_o.astype(BF16))


def kernel(x, mem, positions, ln_gains, ffn1_w_in, ffn1_w_out, ffn2_w_in, ffn2_w_out, even_w_in, even_f_bias, even_qk_gains, even_lambda, even_subln_gain, even_w_out, odd_w_in, odd_conv_w, odd_qk_gains, odd_rel_bias, odd_w_out, x_w_q, x_w_kv, x_qk_gains, x_w_o):
    bsz, seq, d = x.shape
    depth = ln_gains.shape[0]
    assert d == D_MODEL and ffn1_w_in.shape[-1] == 2 * D_FF and mem.shape[-1] == D_MODEL
    assert even_w_in.shape[-1] == 6 * MIX_W + A_HEADS and odd_w_in.shape[-1] == 6 * MIX_W
    assert seq % ATT_TILE == 0 and seq % PREP_TILE == 0 and seq % CROSS_TILE == 0
    assert (bsz * seq) % (FFN_TILES_PER_STEP * TOKEN_TILE) == 0
    rope = _rope_tables(positions)
    x = x.reshape(bsz * seq, d)
    (ffn1_w_in, ffn1_w_out, ffn2_w_in, ffn2_w_out, even_w_in, even_w_out, odd_w_in, odd_w_out,
     x_w_q, x_w_kv, x_w_o) = (
        w.astype(BF16) for w in (ffn1_w_in, ffn1_w_out, ffn2_w_in, ffn2_w_out, even_w_in,
                                 even_w_out, odd_w_in, odd_w_out, x_w_q, x_w_kv, x_w_o))
    for layer in range(depth):
        g = ln_gains[layer]
        x = _ffn(x, g[0], ffn1_w_in[layer], ffn1_w_out[layer])
        if layer % 2 == 0:
            e = layer // 2
            lambda_init = 0.8 - 0.6 * math.exp(-0.3 * layer)
            aq, ak, avt, bq, bk, bvt = _even_prep(
                x, g[1], even_w_in[e], even_f_bias[e], even_qk_gains[e], rope, bsz, seq)
            left = _fox_attention(aq, ak, avt, bsz, seq)
            right = _diff_attention(bq, bk, bvt, even_lambda[e], even_subln_gain[e], lambda_init,
                                    bsz, seq)
            w_mix_out = even_w_out[e]
        else:
            o = layer // 2
            left, dq, dk, dvt = _odd_prep(x, g[1], odd_w_in[o], odd_conv_w[o], odd_qk_gains[o],
                                          bsz, seq)
            right = _band_attention(dq, dk, dvt, odd_rel_bias[o], bsz, seq)
            w_mix_out = odd_w_out[o]
        mk, mv = _mem_kv(mem, g[3], x_w_kv[layer], x_qk_gains[layer, 1])
        x = _mix_out_cross_attention(x, left, right, w_mix_out, g[2], x_w_q[layer],
                                     x_qk_gains[layer, 0], mk, mv, x_w_o[layer], bsz, seq)
        x = _ffn(x, g[4], ffn2_w_in[layer], ffn2_w_out[layer])
    return x.reshape(bsz, seq, d)
```
